```python
import math
import jax
import jax.numpy as jnp
from jax import lax
import numpy as np

D_MODEL = 1024
BATCH = 8
SEQ = 2048
DEPTH = 2
DEC_BATCH = 128
DEC_SEQ = 1
PAST_LEN = 16384
PAGE_SIZE = 128

N_META = 16
D_FF = ((8 * D_MODEL // 3 + 255) // 256) * 256
NORM_EPS = 1e-6
FFN_RES_SCALE = 0.5
S5_WIDTH = 3 * D_MODEL // 8
S5_GROUP = 16
S5_GROUPS = S5_WIDTH // S5_GROUP
S5_STATE = 64
RWKV_HEAD = 64
RWKV_WIDTH = 3 * D_MODEL // 8
RWKV_HEADS = RWKV_WIDTH // RWKV_HEAD
DECAY_RANK = 64
ICL_RANK = 64
GATE_RANK = 128
RWKV_PROJ = 3 * RWKV_WIDTH + DECAY_RANK + ICL_RANK + GATE_RANK
RWKV_LN_EPS = 64e-5
LRU_WIDTH = D_MODEL // 4
LRU_BLOCKS = 4
LRU_BLOCK = LRU_WIDTH // LRU_BLOCKS
CONV_WIDTH = 4
LRU_C = 8.0
D_MIX = S5_WIDTH + RWKV_WIDTH + LRU_WIDTH
D_IN = S5_WIDTH + RWKV_PROJ + 2 * LRU_WIDTH

kernel_name = 'hybrid_s5_rwkv7_rglru_decode_step'


def rmsnorm(x, g):
    xf = x.astype(jnp.float32)
    y = xf * lax.rsqrt(jnp.mean(xf * xf, axis=-1, keepdims=True) + NORM_EPS)
    return (y * g.astype(jnp.float32)).astype(x.dtype)


def swiglu(x, w_gate, w_up, w_down):
    return (jax.nn.silu(x @ w_gate) * (x @ w_up)) @ w_down


def s5_group_mixer(u, h0_re, h0_im, lam_re, lam_im, log_dt, b_re, b_im, c_re, c_im, d_skip, glu_w, glu_b):
    f32 = jnp.float32
    bsz, L, _ = u.shape
    uf = u.astype(f32)
    ug = uf.reshape(bsz, L, S5_GROUPS, S5_GROUP)
    lr = lam_re.astype(f32)
    li = lam_im.astype(f32)
    dt = jnp.exp(log_dt.astype(f32))[:, None]
    mag = jnp.exp(lr * dt)
    ab_re = mag * jnp.cos(li * dt)
    ab_im = mag * jnp.sin(li * dt)
    den = lr * lr + li * li
    f_re = ((ab_re - 1.0) * lr + ab_im * li) / den
    f_im = (ab_im * lr - (ab_re - 1.0) * li) / den
    br = b_re.astype(f32)
    bi = b_im.astype(f32)
    bb_re = f_re[..., None] * br - f_im[..., None] * bi
    bb_im = f_re[..., None] * bi + f_im[..., None] * br
    x_re = jnp.einsum('gnc,blgc->blgn', bb_re, ug)
    x_im = jnp.einsum('gnc,blgc->blgn', bb_im, ug)
    h0r = h0_re.astype(f32)
    h0i = h0_im.astype(f32)
    x_re = x_re.at[:, 0].add(ab_re * h0r - ab_im * h0i)
    x_im = x_im.at[:, 0].add(ab_re * h0i + ab_im * h0r)
    a_re = jnp.broadcast_to(ab_re, (1, L) + ab_re.shape)
    a_im = jnp.broadcast_to(ab_im, (1, L) + ab_im.shape)

    def combine(e1, e2):
        a1r, a1i, b1r, b1i = e1
        a2r, a2i, b2r, b2i = e2
        return (a1r * a2r - a1i * a2i, a1r * a2i + a1i * a2r,
                a2r * b1r - a2i * b1i + b2r, a2r * b1i + a2i * b1r + b2i)

    _, _, h_re, h_im = lax.associative_scan(combine, (a_re, a_im, x_re, x_im), axis=1)
    y = (jnp.einsum('gcn,blgn->blgc', c_re.astype(f32), h_re)
         - jnp.einsum('gcn,blgn->blgc', c_im.astype(f32), h_im)).reshape(bsz, L, S5_WIDTH)
    y = y + d_skip.astype(f32) * uf
    z = jax.nn.gelu(y)
    out = z * jax.nn.sigmoid(z @ glu_w.astype(f32) + glu_b.astype(f32))
    return out, h_re[:, -1], h_im[:, -1]


def rwkv7_group_mixer(p, shift0, state0, mu, w0, w_up, a0, a_up, g_up, k_k, k_a, r_k, ln_w, ln_b):
    f32 = jnp.float32
    bsz, L, _ = p.shape
    pf = p.astype(f32)
    prev = jnp.concatenate([shift0.astype(f32)[:, None], pf[:, :-1]], axis=1)
    xm = pf + (prev - pf) * mu.astype(f32)
    W = RWKV_WIDTH
    r, k, v, xw, xa, xg = jnp.split(xm, [W, 2 * W, 3 * W, 3 * W + DECAY_RANK, 3 * W + DECAY_RANK + ICL_RANK], axis=-1)
    logw = -jax.nn.softplus(-(w0.astype(f32) + jnp.tanh(xw) @ w_up.astype(f32))) - 0.5
    decay = jnp.exp(-jnp.exp(logw))
    a = jax.nn.sigmoid(a0.astype(f32) + xa @ a_up.astype(f32))
    g = jax.nn.sigmoid(xg) @ g_up.astype(f32)
    hs = (bsz, L, RWKV_HEADS, RWKV_HEAD)
    kk = (k * k_k.astype(f32)).reshape(hs)
    kk = kk / jnp.maximum(jnp.sqrt(jnp.sum(kk * kk, axis=-1, keepdims=True)), 1e-12)
    k = k * (1.0 + (a - 1.0) * k_a.astype(f32))
    rh, kh, vh, wh, ah = (t.reshape(hs) for t in (r, k, v, decay, a))

    def step(S, inp):
        r_t, w_t, k_t, v_t, kk_t, a_t = inp
        sa = jnp.einsum('bhvk,bhk->bhv', S, -kk_t)
        S = (S * w_t[:, :, None, :] + sa[..., None] * (kk_t * a_t)[:, :, None, :]
             + v_t[..., None] * k_t[:, :, None, :])
        return S, jnp.einsum('bhvk,bhk->bhv', S, r_t)

    tm = lambda t: jnp.swapaxes(t, 0, 1)
    S_fin, ys = lax.scan(step, state0.astype(f32), (tm(rh), tm(wh), tm(kh), tm(vh), tm(kk), tm(ah)))
    y = tm(ys)
    mean = jnp.mean(y, axis=-1, keepdims=True)
    var = jnp.mean(jnp.square(y - mean), axis=-1, keepdims=True)
    yn = ((y - mean) * lax.rsqrt(var + RWKV_LN_EPS)).reshape(bsz, L, W) * ln_w.astype(f32) + ln_b.astype(f32)
    bonus = jnp.sum(rh * kh * r_k.astype(f32), axis=-1, keepdims=True) * vh
    out = (yn + bonus.reshape(bsz, L, W)) * g
    return out, S_fin, p[:, -1]


def rglru_group_mixer(xb, gb, conv0, h0, conv_w, conv_b, w_a, b_a, w_x, b_x, lam):
    f32 = jnp.float32
    bsz, L, _ = xb.shape
    xp = jnp.concatenate([conv0.astype(f32), xb.astype(f32)], axis=1)
    xc = conv_b.astype(f32) + xp[:, 0:L] * conv_w[0].astype(f32)
    for j in range(1, CONV_WIDTH):
        xc = xc + xp[:, j:j + L] * conv_w[j].astype(f32)
    xh = xc.reshape(bsz, L, LRU_BLOCKS, LRU_BLOCK)
    gate_a = jax.nn.sigmoid(jnp.einsum('blhi,hij->blhj', xh, w_a.astype(f32)).reshape(bsz, L, LRU_WIDTH) + b_a.astype(f32))
    gate_x = jax.nn.sigmoid(jnp.einsum('blhi,hij->blhj', xh, w_x.astype(f32)).reshape(bsz, L, LRU_WIDTH) + b_x.astype(f32))
    log_a = LRU_C * gate_a * jax.nn.log_sigmoid(lam.astype(f32))
    a = jnp.exp(log_a)
    b = jnp.sqrt(-jnp.expm1(2.0 * log_a)) * (gate_x * xc)
    b = b.at[:, 0].add(a[:, 0] * h0.astype(f32))

    def combine(e1, e2):
        a1, b1 = e1
        a2, b2 = e2
        return (a1 * a2, a2 * b1 + b2)

    _, h = lax.associative_scan(combine, (a, b), axis=1)
    out = h * jax.nn.gelu(gb.astype(f32))
    return out, h[:, -1], xp[:, -(CONV_WIDTH - 1):]


def layer_forward(x, s5_re0, s5_im0, rwkv0, shift0, lru0, conv0,
                  ffn1_norm, ffn1_w_gate, ffn1_w_up, ffn1_w_down, mix_norm, w_in,
                  s5_lambda_re, s5_lambda_im, s5_log_dt, s5_b_re, s5_b_im, s5_c_re, s5_c_im, s5_d, s5_glu_w, s5_glu_b,
                  rwkv_mu, rwkv_w0, rwkv_w_up, rwkv_a0, rwkv_a_up, rwkv_g_up, rwkv_k_k, rwkv_k_a, rwkv_r_k, rwkv_ln_w, rwkv_ln_b,
                  lru_conv_w, lru_conv_b, lru_w_a, lru_b_a, lru_w_x, lru_b_x, lru_lambda,
                  w_out, ffn2_norm, ffn2_w_gate, ffn2_w_up, ffn2_w_down):
    h = x + FFN_RES_SCALE * swiglu(rmsnorm(x, ffn1_norm), ffn1_w_gate, ffn1_w_up, ffn1_w_down)
    proj = rmsnorm(h, mix_norm) @ w_in
    u_s5, p_rwkv, x_lru, g_lru = jnp.split(
        proj, [S5_WIDTH, S5_WIDTH + RWKV_PROJ, S5_WIDTH + RWKV_PROJ + LRU_WIDTH], axis=-1)
    o_s5, s5_re, s5_im = s5_group_mixer(u_s5, s5_re0, s5_im0, s5_lambda_re, s5_lambda_im, s5_log_dt,
                                        s5_b_re, s5_b_im, s5_c_re, s5_c_im, s5_d, s5_glu_w, s5_glu_b)
    o_rwkv, rwkv_s, shift = rwkv7_group_mixer(p_rwkv, shift0, rwkv0, rwkv_mu, rwkv_w0, rwkv_w_up, rwkv_a0,
                                              rwkv_a_up, rwkv_g_up, rwkv_k_k, rwkv_k_a, rwkv_r_k, rwkv_ln_w, rwkv_ln_b)
    o_lru, lru_h, conv = rglru_group_mixer(x_lru, g_lru, conv0, lru0, lru_conv_w, lru_conv_b,
                                           lru_w_a, lru_b_a, lru_w_x, lru_b_x, lru_lambda)
    mix = jnp.concatenate([o_s5, o_rwkv, o_lru], axis=-1).astype(x.dtype) @ w_out
    h = h + mix
    h = h + FFN_RES_SCALE * swiglu(rmsnorm(h, ffn2_norm), ffn2_w_gate, ffn2_w_up, ffn2_w_down)
    return h, s5_re, s5_im, rwkv_s, shift, lru_h, conv


def setup_inputs(seed: int = 0) -> dict:
    key = jax.random.key(seed)
    ks = iter(jax.random.split(key, 64))
    f32 = jnp.float32

    def nrm(shape, scale):
        return scale * jax.random.normal(next(ks), shape, f32)

    def unif(shape, lo, hi):
        return jax.random.uniform(next(ks), shape, f32, lo, hi)

    L = DEPTH
    lam_im0 = jnp.pi * jnp.arange(S5_STATE, dtype=f32)
    s_lru = unif((L, LRU_WIDTH), 0.9, 0.999) ** (1.0 / LRU_C)
    return {
        'x_prompt': nrm((BATCH, SEQ, D_MODEL), 1.0),
        'x_sample': nrm((DEC_BATCH, DEC_SEQ, D_MODEL), 1.0),
        'state_s5_re': nrm((L, DEC_BATCH, S5_GROUPS, S5_STATE), 0.5),
        'state_s5_im': nrm((L, DEC_BATCH, S5_GROUPS, S5_STATE), 0.5),
        'state_rwkv': nrm((L, DEC_BATCH, RWKV_HEADS, RWKV_HEAD, RWKV_HEAD), 0.2),
        'state_rwkv_shift': nrm((L, DEC_BATCH, RWKV_PROJ), 1.0),
        'state_lru': nrm((L, DEC_BATCH, LRU_WIDTH), 0.5),
        'state_lru_conv': nrm((L, DEC_BATCH, CONV_WIDTH - 1, LRU_WIDTH), 1.0),
        'meta_tokens': nrm((N_META, D_MODEL), 1.0),
        'ffn1_norm': 1.0 + nrm((L, D_MODEL), 0.02),
        'ffn1_w_gate': nrm((L, D_MODEL, D_FF), D_MODEL ** -0.5),
        'ffn1_w_up': nrm((L, D_MODEL, D_FF), D_MODEL ** -0.5),
        'ffn1_w_down': nrm((L, D_FF, D_MODEL), D_FF ** -0.5),
        'mix_norm': 1.0 + nrm((L, D_MODEL), 0.02),
        'w_in': nrm((L, D_MODEL, D_IN), D_MODEL ** -0.5),
        's5_lambda_re': -0.5 + nrm((L, S5_GROUPS, S5_STATE), 0.01),
        's5_lambda_im': lam_im0 + nrm((L, S5_GROUPS, S5_STATE), 0.01),
        's5_log_dt': unif((L, S5_GROUPS), math.log(1e-3), math.log(1e-1)),
        's5_b_re': nrm((L, S5_GROUPS, S5_STATE, S5_GROUP), (2 * S5_GROUP) ** -0.5),
        's5_b_im': nrm((L, S5_GROUPS, S5_STATE, S5_GROUP), (2 * S5_GROUP) ** -0.5),
        's5_c_re': nrm((L, S5_GROUPS, S5_GROUP, S5_STATE), (2 * S5_STATE) ** -0.5),
        's5_c_im': nrm((L, S5_GROUPS, S5_GROUP, S5_STATE), (2 * S5_STATE) ** -0.5),
        's5_d': nrm((L, S5_WIDTH), 0.5),
        's5_glu_w': nrm((L, S5_WIDTH, S5_WIDTH), S5_WIDTH ** -0.5),
        's5_glu_b': nrm((L, S5_WIDTH), 0.01),
        'rwkv_mu': unif((L, RWKV_PROJ), 0.0, 1.0),
        'rwkv_w0': unif((L, RWKV_WIDTH), -6.0, -1.0),
        'rwkv_w_up': nrm((L, DECAY_RANK, RWKV_WIDTH), 0.1),
        'rwkv_a0': nrm((L, RWKV_WIDTH), 0.1),
        'rwkv_a_up': nrm((L, ICL_RANK, RWKV_WIDTH), ICL_RANK ** -0.5),
        'rwkv_g_up': nrm((L, GATE_RANK, RWKV_WIDTH), GATE_RANK ** -0.5),
        'rwkv_k_k': 0.85 + nrm((L, RWKV_WIDTH), 0.02),
        'rwkv_k_a': 1.0 + nrm((L, RWKV_WIDTH), 0.02),
        'rwkv_r_k': nrm((L, RWKV_HEADS, RWKV_HEAD), 0.1),
        'rwkv_ln_w': 1.0 + nrm((L, RWKV_WIDTH), 0.02),
        'rwkv_ln_b': nrm((L, RWKV_WIDTH), 0.01),
        'lru_conv_w': nrm((L, CONV_WIDTH, LRU_WIDTH), CONV_WIDTH ** -0.5),
        'lru_conv_b': nrm((L, LRU_WIDTH), 0.01),
        'lru_w_a': nrm((L, LRU_BLOCKS, LRU_BLOCK, LRU_BLOCK), LRU_BLOCK ** -0.5),
        'lru_b_a': nrm((L, LRU_WIDTH), 0.01),
        'lru_w_x': nrm((L, LRU_BLOCKS, LRU_BLOCK, LRU_BLOCK), LRU_BLOCK ** -0.5),
        'lru_b_x': nrm((L, LRU_WIDTH), 0.01),
        'lru_lambda': jnp.log(s_lru) - jnp.log1p(-s_lru),
        'w_out': nrm((L, D_MIX, D_MODEL), D_MIX ** -0.5),
        'ffn2_norm': 1.0 + nrm((L, D_MODEL), 0.02),
        'ffn2_w_gate': nrm((L, D_MODEL, D_FF), D_MODEL ** -0.5),
        'ffn2_w_up': nrm((L, D_MODEL, D_FF), D_MODEL ** -0.5),
        'ffn2_w_down': nrm((L, D_FF, D_MODEL), D_FF ** -0.5),
        'final_norm': 1.0 + nrm((D_MODEL,), 0.02),
    }


def reference(x_prompt, x_sample, state_s5_re, state_s5_im, state_rwkv, state_rwkv_shift, state_lru, state_lru_conv,
              meta_tokens, ffn1_norm, ffn1_w_gate, ffn1_w_up, ffn1_w_down, mix_norm, w_in,
              s5_lambda_re, s5_lambda_im, s5_log_dt, s5_b_re, s5_b_im, s5_c_re, s5_c_im, s5_d, s5_glu_w, s5_glu_b,
              rwkv_mu, rwkv_w0, rwkv_w_up, rwkv_a0, rwkv_a_up, rwkv_g_up, rwkv_k_k, rwkv_k_a, rwkv_r_k, rwkv_ln_w, rwkv_ln_b,
              lru_conv_w, lru_conv_b, lru_w_a, lru_b_a, lru_w_x, lru_b_x, lru_lambda,
              w_out, ffn2_norm, ffn2_w_gate, ffn2_w_up, ffn2_w_down, final_norm):
    f32 = jnp.float32
    bp = x_prompt.shape[0]
    meta = jnp.broadcast_to(meta_tokens.astype(x_prompt.dtype)[None], (bp, N_META, D_MODEL))
    xp = jnp.concatenate([meta, x_prompt], axis=1)
    xs = x_sample
    p_list = []
    s_list = []
    for l in range(DEPTH):
        lp = (ffn1_norm[l], ffn1_w_gate[l], ffn1_w_up[l], ffn1_w_down[l], mix_norm[l], w_in[l],
              s5_lambda_re[l], s5_lambda_im[l], s5_log_dt[l], s5_b_re[l], s5_b_im[l], s5_c_re[l], s5_c_im[l],
              s5_d[l], s5_glu_w[l], s5_glu_b[l],
              rwkv_mu[l], rwkv_w0[l], rwkv_w_up[l], rwkv_a0[l], rwkv_a_up[l], rwkv_g_up[l], rwkv_k_k[l],
              rwkv_k_a[l], rwkv_r_k[l], rwkv_ln_w[l], rwkv_ln_b[l],
              lru_conv_w[l], lru_conv_b[l], lru_w_a[l], lru_b_a[l], lru_w_x[l], lru_b_x[l], lru_lambda[l],
              w_out[l], ffn2_norm[l], ffn2_w_gate[l], ffn2_w_up[l], ffn2_w_down[l])
        xp, *new_p = layer_forward(
            xp,
            jnp.zeros((bp, S5_GROUPS, S5_STATE), f32), jnp.zeros((bp, S5_GROUPS, S5_STATE), f32),
            jnp.zeros((bp, RWKV_HEADS, RWKV_HEAD, RWKV_HEAD), f32), jnp.zeros((bp, RWKV_PROJ), f32),
            jnp.zeros((bp, LRU_WIDTH), f32), jnp.zeros((bp, CONV_WIDTH - 1, LRU_WIDTH), f32),
            *lp)
        xs, *new_s = layer_forward(
            xs, state_s5_re[l], state_s5_im[l], state_rwkv[l], state_rwkv_shift[l],
            state_lru[l], state_lru_conv[l], *lp)
        p_list.append(new_p)
        s_list.append(new_s)
    p_new = [jnp.stack([st[i] for st in p_list], axis=0) for i in range(6)]
    s_new = [jnp.stack([st[i] for st in s_list], axis=0) for i in range(6)]
    y_prompt = rmsnorm(xp, final_norm)[:, N_META:]
    y_sample = rmsnorm(xs, final_norm)
    return (y_prompt, y_sample,
            p_new[0], p_new[1], p_new[2], p_new[3], p_new[4], p_new[5],
            s_new[0], s_new[1], s_new[2], s_new[3], s_new[4], s_new[5])
```

```python
import functools
import math

import jax
import jax.numpy as jnp
from jax import lax
from jax.experimental import pallas as pl
from jax.experimental.pallas import tpu as pltpu

F32 = jnp.float32
BF16 = jnp.bfloat16

D_MODEL = 1024
DEPTH = 2
N_META = 16
D_FF = 2816
NORM_EPS = 1e-6
FFN_RES_SCALE = 0.5
S5_WIDTH = 384
S5_GROUP = 16
S5_GROUPS = 24
S5_STATE = 64
S5_LANES = S5_GROUPS * S5_STATE
RWKV_HEAD = 64
RWKV_WIDTH = 384
RWKV_HEADS = 6
RWKV_PAIRS = 3
DECAY_RANK = 64
ICL_RANK = 64
GATE_RANK = 128
RWKV_PROJ = 1408
RWKV_LN_EPS = 64e-5
LRU_WIDTH = 256
LRU_BLOCKS = 4
LRU_BLOCK = 64
CONV_WIDTH = 4
LRU_C = 8.0
D_IN = 2304
SL_IN = S5_WIDTH + 2 * LRU_WIDTH
SL_OUT = S5_WIDTH + LRU_WIDTH
LANE = 128

V7X_VMEM_BYTES = 64 * 2 ** 20
VMEM_LIMIT = (V7X_VMEM_BYTES * 7) // 8


def _dot(a, b):
    return jnp.dot(a, b, preferred_element_type=F32)


def _dot_nt(a, b):
    return lax.dot_general(a, b, (((1,), (1,)), ((), ())), preferred_element_type=F32)


def _dot_tn(a, b):
    return lax.dot_general(a, b, (((0,), (0,)), ((), ())), preferred_element_type=F32)


def _split2(x):
    hi = x.astype(BF16)
    lo = (x - hi.astype(F32)).astype(BF16)
    return hi, lo


def _split3(x):
    hi = x.astype(BF16)
    r = x - hi.astype(F32)
    mid = r.astype(BF16)
    lo = (r - mid.astype(F32)).astype(BF16)
    return hi, mid, lo


def _sigmoid(x):
    return 1.0 / (1.0 + jnp.exp(-x))


def _gelu(x):
    c = math.sqrt(2.0 / math.pi)
    return 0.5 * x * (1.0 + jnp.tanh(c * (x + 0.044715 * (x * x * x))))


def _neg_expm1(z):
    t = jnp.tanh(0.5 * z)
    return -2.0 * t / (1.0 - t)


def _softplus(x):
    return jnp.maximum(x, 0.0) + jnp.log1p(jnp.exp(-jnp.abs(x)))


def _rms(x, g):
    return x * lax.rsqrt(jnp.mean(x * x, axis=-1, keepdims=True) + NORM_EPS) * g


def _swiglu_res(x, g_norm, wg_ref, wu_ref, wd_ref):
    xn = _rms(x, g_norm).astype(BF16)
    g = _dot(xn, wg_ref[...])
    u = _dot(xn, wu_ref[...])
    a = (g * _sigmoid(g) * u).astype(BF16)
    return x + FFN_RES_SCALE * _dot(a, wd_ref[...])


def _const_spec(shape):
    nd = len(shape)
    return pl.BlockSpec(shape, lambda *_: (0,) * nd, pipeline_mode=pl.Buffered(1))


def _k1_kernel(x_ref, n1_ref, wg_ref, wu_ref, wd_ref, nm_ref, win_ref, h_ref, prw_ref, psl_ref):
    x = x_ref[...]
    h = _swiglu_res(x, n1_ref[...], wg_ref, wu_ref, wd_ref)
    h_ref[...] = h
    hn = _rms(h, nm_ref[...]).astype(BF16)
    proj = _dot(hn, win_ref[...])
    psl_ref[:, 0:S5_WIDTH] = proj[:, 0:S5_WIDTH]
    prw_ref[...] = proj[:, S5_WIDTH:S5_WIDTH + RWKV_PROJ]
    psl_ref[:, S5_WIDTH:SL_IN] = proj[:, S5_WIDTH + RWKV_PROJ:D_IN]


def _k1(x, n1, wg, wu, wd, nm, win, tl):
    B, L, _ = x.shape
    grid = (B, L // tl)
    return pl.pallas_call(
        _k1_kernel,
        grid=grid,
        in_specs=[
            pl.BlockSpec((None, tl, D_MODEL), lambda b, i: (b, i, 0)),
            _const_spec((1, D_MODEL)),
            _const_spec((D_MODEL, D_FF)),
            _const_spec((D_MODEL, D_FF)),
            _const_spec((D_FF, D_MODEL)),
            _const_spec((1, D_MODEL)),
            _const_spec((D_MODEL, D_IN)),
        ],
        out_specs=[
            pl.BlockSpec((None, tl, D_MODEL), lambda b, i: (b, i, 0)),
            pl.BlockSpec((None, tl, RWKV_PROJ), lambda b, i: (b, i, 0)),
            pl.BlockSpec((tl, SL_IN), lambda b, i: (i, b)),
        ],
        out_shape=[
            jax.ShapeDtypeStruct((B, L, D_MODEL), F32),
            jax.ShapeDtypeStruct((B, L, RWKV_PROJ), F32),
            jax.ShapeDtypeStruct((L, B * SL_IN), F32),
        ],
        compiler_params=pltpu.CompilerParams(
            dimension_semantics=("parallel", "parallel"), vmem_limit_bytes=VMEM_LIMIT),
        name="ffn1_inproj",
    )(x, n1, wg, wu, wd, nm, win)


def _k3_kernel(h_ref, osl_ref, orw_ref, wout_ref, n2_ref, wg_ref, wu_ref, wd_ref, fn_ref, o_ref, *, final):
    osl = osl_ref[...]
    mix = (_dot(osl[:, 0:S5_WIDTH].astype(BF16), wout_ref[0:S5_WIDTH, :])
           + _dot(orw_ref[...].astype(BF16), wout_ref[S5_WIDTH:S5_WIDTH + RWKV_WIDTH, :])
           + _dot(osl[:, S5_WIDTH:SL_OUT].astype(BF16), wout_ref[S5_WIDTH + RWKV_WIDTH:D_MODEL, :]))
    h2 = h_ref[...] + mix
    h3 = _swiglu_res(h2, n2_ref[...], wg_ref, wu_ref, wd_ref)
    if final:
        h3 = _rms(h3, fn_ref[...])
    o_ref[...] = h3


def _k3(h, osl, orw, wout, n2, wg, wu, wd, fnorm, tl, final):
    B, L, _ = h.shape
    grid = (B, L // tl)
    return pl.pallas_call(
        functools.partial(_k3_kernel, final=final),
        grid=grid,
        in_specs=[
            pl.BlockSpec((None, tl, D_MODEL), lambda b, i: (b, i, 0)),
            pl.BlockSpec((tl, SL_OUT), lambda b, i: (i, b)),
            pl.BlockSpec((None, tl, RWKV_WIDTH), lambda b, i: (b, i, 0)),
            _const_spec((D_MODEL, D_MODEL)),
            _const_spec((1, D_MODEL)),
            _const_spec((D_MODEL, D_FF)),
            _const_spec((D_MODEL, D_FF)),
            _const_spec((D_FF, D_MODEL)),
            _const_spec((1, D_MODEL)),
        ],
        out_specs=pl.BlockSpec((None, tl, D_MODEL), lambda b, i: (b, i, 0)),
        out_shape=jax.ShapeDtypeStruct((B, L, D_MODEL), F32),
        compiler_params=pltpu.CompilerParams(
            dimension_semantics=("parallel", "parallel"), vmem_limit_bytes=VMEM_LIMIT),
        name="outproj_ffn2",
    )(h, osl, orw, wout, n2, wg, wu, wd, fnorm)


def _s5_prep_kernel(lr_ref, li_ref, ldt_ref, br_ref, bi_ref, ab_ref, bbr_ref, bbi_ref):
    lr = lr_ref[...]
    li = li_ref[...]
    dt = jnp.exp(ldt_ref[...])
    mag = jnp.exp(lr * dt)
    ab_re = mag * jnp.cos(li * dt)
    ab_im = mag * jnp.sin(li * dt)
    den = lr * lr + li * li
    f_re = ((ab_re - 1.0) * lr + ab_im * li) / den
    f_im = (ab_im * lr - (ab_re - 1.0) * li) / den
    ab_ref[0] = ab_re
    ab_ref[1] = ab_im
    br = br_ref[...]
    bi = bi_ref[...]
    bbr_ref[...] = f_re[:, None, :] * br - f_im[:, None, :] * bi
    bbi_ref[...] = f_re[:, None, :] * bi + f_im[:, None, :] * br


def _s5_prep(lam_re, lam_im, log_dt, b_re, b_im):
    G, N, C = S5_GROUPS, S5_STATE, S5_GROUP
    ldt = jnp.broadcast_to(log_dt[:, None], (G, N))
    br = jnp.swapaxes(b_re, 1, 2)
    bi = jnp.swapaxes(b_im, 1, 2)
    return pl.pallas_call(
        _s5_prep_kernel,
        out_shape=[
            jax.ShapeDtypeStruct((2, G, N), F32),
            jax.ShapeDtypeStruct((G, C, N), F32),
            jax.ShapeDtypeStruct((G, C, N), F32),
        ],
        name="s5_discretise",
    )(lam_re, lam_im, ldt, br, bi)


def _s5lru_kernel(psl_ref, ab_ref, win_ref, cre_ref, cim_ref, d_ref, gw_ref, gb_ref,
                  cw_ref, cb_ref, wa_ref, wx_ref, ba_ref, bx_ref, lam_ref,
                  h0_ref, l0_ref, c0_ref,
                  o_ref, hT_ref, lT_ref, cT_ref,
                  hs_scr, ls_scr, xc_scr, xr_scr, xi_scr, la_scr, lb_scr, *, B, Tc, unroll):
    c = pl.program_id(0)
    R = Tc * B
    hist = CONV_WIDTH - 1

    @pl.when(c == 0)
    def _():
        hs_scr[...] = h0_ref[...]
        ls_scr[...] = l0_ref[...]
        xc_scr[0:hist] = c0_ref[...]

    psl = psl_ref[...]
    u = psl[:, :, 0:S5_WIDTH].reshape(R, S5_WIDTH)
    gate_in = psl[:, :, S5_WIDTH + LRU_WIDTH:SL_IN].reshape(R, LRU_WIDTH)

    ub = u.astype(BF16)
    half = 4 * LANE
    for j in range(3):
        x = _dot(ub[:, LANE * j:LANE * (j + 1)], win_ref[j])
        xr_scr[:, :, half * j:half * (j + 1)] = x[:, 0:half].reshape(Tc, B, half)
        xi_scr[:, :, half * j:half * (j + 1)] = x[:, half:2 * half].reshape(Tc, B, half)

    xc_scr[hist:hist + Tc] = psl[:, :, S5_WIDTH:S5_WIDTH + LRU_WIDTH]
    cw = cw_ref[...]
    xc = cb_ref[...] + xc_scr[0:Tc] * cw[0:1]
    for j in range(1, CONV_WIDTH):
        xc = xc + xc_scr[j:j + Tc] * cw[j:j + 1]
    new_hist = xc_scr[Tc:Tc + hist]
    xc_scr[0:hist] = new_hist
    xc2 = xc.reshape(R, LRU_WIDTH)
    xcb = xc2.astype(BF16)
    gate_a = _sigmoid(_dot(xcb, wa_ref[...]) + ba_ref[...])
    gate_x = _sigmoid(_dot(xcb, wx_ref[...]) + bx_ref[...])
    log_a = LRU_C * gate_a * (-_softplus(-lam_ref[...]))
    la_scr[...] = jnp.exp(log_a).reshape(Tc, B, LRU_WIDTH)
    lb_scr[...] = (jnp.sqrt(_neg_expm1(2.0 * log_a)) * (gate_x * xc2)).reshape(Tc, B, LRU_WIDTH)

    ar = jnp.broadcast_to(ab_ref[0:1, :], (B, S5_LANES))
    ai = jnp.broadcast_to(ab_ref[1:2, :], (B, S5_LANES))

    def step(t, carry):
        hr, hi, hl = carry
        nr = ar * hr - ai * hi + xr_scr[t]
        ni = ar * hi + ai * hr + xi_scr[t]
        xr_scr[t] = nr
        xi_scr[t] = ni
        nl = la_scr[t] * hl + lb_scr[t]
        lb_scr[t] = nl
        return nr, ni, nl

    hr, hi, hl = lax.fori_loop(0, Tc, step, (hs_scr[0], hs_scr[1], ls_scr[...]), unroll=unroll)
    hs_scr[0] = hr
    hs_scr[1] = hi
    ls_scr[...] = hl

    hrb = xr_scr[...].reshape(R, S5_LANES).astype(BF16)
    hib = xi_scr[...].reshape(R, S5_LANES).astype(BF16)
    ys = []
    for j in range(3):
        sl = slice(half * j, half * (j + 1))
        ys.append(_dot(hrb[:, sl], cre_ref[j]) - _dot(hib[:, sl], cim_ref[j]))
    y = jnp.concatenate(ys, axis=-1) + d_ref[...] * u
    z = _gelu(y)
    o_s5 = z * _sigmoid(_dot(z.astype(BF16), gw_ref[...]) + gb_ref[...])
    o_lru = lb_scr[...].reshape(R, LRU_WIDTH) * _gelu(gate_in)
    o_ref[:, :, 0:S5_WIDTH] = o_s5.reshape(Tc, B, S5_WIDTH)
    o_ref[:, :, S5_WIDTH:SL_OUT] = o_lru.reshape(Tc, B, LRU_WIDTH)

    @pl.when(c == pl.num_programs(0) - 1)
    def _():
        hT_ref[...] = hs_scr[...]
        lT_ref[...] = ls_scr[...]
        cT_ref[...] = xc_scr[0:hist]


def _s5lru(psl3, sp, h0, l0, c0, Tc):
    L, B, _ = psl3.shape
    hist = CONV_WIDTH - 1
    unroll = True if Tc <= 2 else 2
    return pl.pallas_call(
        functools.partial(_s5lru_kernel, B=B, Tc=Tc, unroll=unroll),
        grid=(L // Tc,),
        in_specs=[
            pl.BlockSpec((Tc, B, SL_IN), lambda c: (c, 0, 0)),
            _const_spec((2, S5_LANES)),
            _const_spec((3, LANE, 8 * LANE)),
            _const_spec((3, 4 * LANE, LANE)),
            _const_spec((3, 4 * LANE, LANE)),
            _const_spec((1, S5_WIDTH)),
            _const_spec((S5_WIDTH, S5_WIDTH)),
            _const_spec((1, S5_WIDTH)),
            _const_spec((CONV_WIDTH, LRU_WIDTH)),
            _const_spec((1, LRU_WIDTH)),
            _const_spec((LRU_WIDTH, LRU_WIDTH)),
            _const_spec((LRU_WIDTH, LRU_WIDTH)),
            _const_spec((1, LRU_WIDTH)),
            _const_spec((1, LRU_WIDTH)),
            _const_spec((1, LRU_WIDTH)),
            _const_spec((2, B, S5_LANES)),
            _const_spec((B, LRU_WIDTH)),
            _const_spec((hist, B, LRU_WIDTH)),
        ],
        out_specs=[
            pl.BlockSpec((Tc, B, SL_OUT), lambda c: (c, 0, 0)),
            pl.BlockSpec((2, B, S5_LANES), lambda c: (0, 0, 0)),
            pl.BlockSpec((B, LRU_WIDTH), lambda c: (0, 0)),
            pl.BlockSpec((hist, B, LRU_WIDTH), lambda c: (0, 0, 0)),
        ],
        out_shape=[
            jax.ShapeDtypeStruct((L, B, SL_OUT), F32),
            jax.ShapeDtypeStruct((2, B, S5_LANES), F32),
            jax.ShapeDtypeStruct((B, LRU_WIDTH), F32),
            jax.ShapeDtypeStruct((hist, B, LRU_WIDTH), F32),
        ],
        scratch_shapes=[
            pltpu.VMEM((2, B, S5_LANES), F32),
            pltpu.VMEM((B, LRU_WIDTH), F32),
            pltpu.VMEM((Tc + hist, B, LRU_WIDTH), F32),
            pltpu.VMEM((Tc, B, S5_LANES), F32),
            pltpu.VMEM((Tc, B, S5_LANES), F32),
            pltpu.VMEM((Tc, B, LRU_WIDTH), F32),
            pltpu.VMEM((Tc, B, LRU_WIDTH), F32),
        ],
        compiler_params=pltpu.CompilerParams(
            dimension_semantics=("arbitrary",), vmem_limit_bytes=VMEM_LIMIT),
        name="s5_rglru_scan",
    )(psl3, sp["ab"], sp["win"], sp["cre"], sp["cim"], sp["d"], sp["glu_w"], sp["glu_b"],
      sp["conv_w"], sp["conv_b"], sp["wa"], sp["wx"], sp["ba"], sp["bx"], sp["lam"], h0, l0, c0)


def _segsum(x, ones_ref):
    outs = []
    for p in range(RWKV_PAIRS):
        hi, lo = _split2(x[:, LANE * p:LANE * (p + 1)])
        outs.append(_dot(hi, ones_ref[...]) + _dot(lo, ones_ref[...]))
    return jnp.concatenate(outs, axis=-1)


def _rwkv_token_math(xm, w0, wup_ref, a0, aup_ref, gup_ref, k_k, k_a, r_k, ones_ref):
    W = RWKV_WIDTH
    r = xm[:, 0:W]
    k = xm[:, W:2 * W]
    v = xm[:, 2 * W:3 * W]
    xwa = xm[:, 3 * W:3 * W + DECAY_RANK + ICL_RANK]
    xg = xm[:, 3 * W + DECAY_RANK + ICL_RANK:RWKV_PROJ]
    lw = w0 + _dot(jnp.tanh(xwa).astype(BF16), wup_ref[...])
    logw = -_softplus(-lw) - 0.5
    ld = -jnp.exp(logw)
    a = _sigmoid(a0 + _dot(xwa.astype(BF16), aup_ref[...]))
    g = _dot(_sigmoid(xg).astype(BF16), gup_ref[...])
    kkr = k * k_k
    kk = kkr / jnp.maximum(jnp.sqrt(_segsum(kkr * kkr, ones_ref)), 1e-12)
    k2 = k * (1.0 + (a - 1.0) * k_a)
    bonus = _segsum(r * k2 * r_k, ones_ref) * v
    return r, k2, v, kk, a, ld, g, bonus


def _rwkv_out(y, bonus, g, ln_w, ln_b, ones_ref):
    inv = 1.0 / RWKV_HEAD
    mean = _segsum(y, ones_ref) * inv
    d = y - mean
    var = _segsum(d * d, ones_ref) * inv
    yn = d * lax.rsqrt(var + RWKV_LN_EPS) * ln_w + ln_b
    return (yn + bonus) * g


def _rwkv_chunk_kernel(p_ref, s0_ref, sh0_ref, mu_ref, w0_ref, wup_ref, a0_ref, aup_ref, gup_ref,
                       kk_ref, ka_ref, rk_ref, lnw_ref, lnb_ref, ones_ref, tri_ref,
                       o_ref, sT_ref,
                       st_scr, xs_scr, a_scr, b_scr, k_scr, r_scr, v_scr, bp_scr, kp_scr, pt_scr, y_scr,
                       *, B, Tc):
    c = pl.program_id(0)
    R = B * Tc
    W = RWKV_WIDTH

    @pl.when(c == 0)
    def _():
        st_scr[...] = s0_ref[...]
        xs_scr[:, 7:8, :] = sh0_ref[...]

    p = p_ref[...]
    xs_scr[:, 8:8 + Tc, :] = p
    prev = xs_scr[:, 7:7 + Tc, :]
    xs_scr[:, 7:8, :] = p[:, Tc - 1:Tc, :]
    pf = p.reshape(R, RWKV_PROJ)
    xm = pf + (prev.reshape(R, RWKV_PROJ) - pf) * mu_ref[...]
    r, k2, v, kk, a, ld, g, bonus = _rwkv_token_math(
        xm, w0_ref[...], wup_ref, a0_ref[...], aup_ref, gup_ref, kk_ref[...], ka_ref[...], rk_ref[...], ones_ref)

    h1, h2, h3 = _split3(ld)
    tri = tri_ref[...]
    cls = []
    for b in range(B):
        rows = slice(b * Tc, (b + 1) * Tc)
        cls.append(_dot(tri, h1[rows]) + _dot(tri, h2[rows]) + _dot(tri, h3[rows]))
    cl = jnp.concatenate(cls, axis=0)
    cl3 = cl.reshape(B, Tc, W)
    clT = cl3[:, Tc - 1:Tc, :]
    to_end = jnp.exp(clT - cl3).reshape(R, W)
    inv_p = jnp.exp(-cl)
    beta = kk * a
    a_scr[...] = (-kk * jnp.exp(cl - ld)).astype(BF16)
    b_scr[...] = (beta * inv_p).astype(BF16)
    k_scr[...] = (k2 * inv_p).astype(BF16)
    r_scr[...] = (r * jnp.exp(cl)).astype(BF16)
    v_scr[...] = v.astype(BF16)
    bp_scr[...] = (beta * to_end).astype(BF16)
    kp_scr[...] = (k2 * to_end).astype(BF16)
    pt_scr[...] = jnp.exp(clT)

    lane = lax.broadcasted_iota(jnp.int32, (1, LANE), 1)
    m0 = (lane < RWKV_HEAD).astype(BF16)
    m1 = (lane >= RWKV_HEAD).astype(BF16)
    assert Tc & (Tc - 1) == 0
    ti = lax.broadcasted_iota(jnp.int32, (2 * Tc, 2 * Tc), 0) & (Tc - 1)
    si = lax.broadcasted_iota(jnp.int32, (2 * Tc, 2 * Tc), 1) & (Tc - 1)
    strict = (si < ti).astype(F32)
    incl = (si <= ti).astype(F32)
    n_doubling = max(1, (Tc - 1).bit_length())

    def stack2(x):
        return jnp.concatenate([x * m0, x * m1], axis=0)

    def per_seq(b, carry):
        rows = pl.ds(pl.multiple_of(b * Tc, Tc), Tc)
        for pr in range(RWKV_PAIRS):
            sl = slice(LANE * pr, LANE * (pr + 1))
            a2 = stack2(a_scr[rows, sl])
            r2 = stack2(r_scr[rows, sl])
            b2 = stack2(b_scr[rows, sl])
            k2_ = stack2(k_scr[rows, sl])
            v2 = stack2(v_scr[rows, sl])
            bp2 = stack2(bp_scr[rows, sl])
            kp2 = stack2(kp_scr[rows, sl])
            s = st_scr[b, pr]
            lhs = jnp.concatenate([a2, r2], axis=0)
            gb = _dot_nt(lhs, b2)
            gk = _dot_nt(lhs, k2_)
            gs = _dot_nt(lhs, s.astype(BF16))
            n = gb[0:2 * Tc] * strict
            wrb = gb[2 * Tc:4 * Tc] * incl
            m = gk[0:2 * Tc] * strict
            wrk = gk[2 * Tc:4 * Tc] * incl
            x = gs[0:2 * Tc] + _dot(m.astype(BF16), v2)
            nk = n
            for j in range(n_doubling):
                nkb = nk.astype(BF16)
                x = x + _dot(nkb, x.astype(BF16))
                if j + 1 < n_doubling:
                    nk = _dot(nkb, nkb)
            sab = x.astype(BF16)
            y2 = gs[2 * Tc:4 * Tc] + _dot(wrb.astype(BF16), sab) + _dot(wrk.astype(BF16), v2)
            y_scr[rows, sl] = y2[0:Tc] + y2[Tc:2 * Tc]
            upd = _dot_tn(jnp.concatenate([sab, v2], axis=0), jnp.concatenate([bp2, kp2], axis=0))
            st_scr[b, pr] = s * pt_scr[b, :, sl] + upd
        return carry

    lax.fori_loop(0, B, per_seq, 0)

    out = _rwkv_out(y_scr[...], bonus, g, lnw_ref[...], lnb_ref[...], ones_ref)
    o_ref[...] = out.reshape(B, Tc, W)

    @pl.when(c == pl.num_programs(0) - 1)
    def _():
        sT_ref[...] = st_scr[...]


def _rwkv_chunk(prw, rp, s0, sh0, Tc):
    B, L, _ = prw.shape
    R = B * Tc
    W = RWKV_WIDTH
    tri = jnp.tril(jnp.ones((Tc, Tc), F32)).astype(BF16)
    return pl.pallas_call(
        functools.partial(_rwkv_chunk_kernel, B=B, Tc=Tc),
        grid=(L // Tc,),
        in_specs=[
            pl.BlockSpec((B, Tc, RWKV_PROJ), lambda c: (0, c, 0)),
            _const_spec((B, RWKV_PAIRS, LANE, LANE)),
            _const_spec((B, 1, RWKV_PROJ)),
            _const_spec((1, RWKV_PROJ)),
            _const_spec((1, W)),
            _const_spec((LANE, W)),
            _const_spec((1, W)),
            _const_spec((LANE, W)),
            _const_spec((GATE_RANK, W)),
            _const_spec((1, W)),
            _const_spec((1, W)),
            _const_spec((1, W)),
            _const_spec((1, W)),
            _const_spec((1, W)),
            _const_spec((LANE, LANE)),
            _const_spec((Tc, Tc)),
        ],
        out_specs=[
            pl.BlockSpec((B, Tc, W), lambda c: (0, c, 0)),
            pl.BlockSpec((B, RWKV_PAIRS, LANE, LANE), lambda c: (0, 0, 0, 0)),
        ],
        out_shape=[
            jax.ShapeDtypeStruct((B, L, W), F32),
            jax.ShapeDtypeStruct((B, RWKV_PAIRS, LANE, LANE), F32),
        ],
        scratch_shapes=[
            pltpu.VMEM((B, RWKV_PAIRS, LANE, LANE), F32),
            pltpu.VMEM((B, Tc + 8, RWKV_PROJ), F32),
        ] + [pltpu.VMEM((R, W), BF16)] * 7 + [
            pltpu.VMEM((B, 1, W), F32),
            pltpu.VMEM((R, W), F32),
        ],
        compiler_params=pltpu.CompilerParams(
            dimension_semantics=("arbitrary",), vmem_limit_bytes=VMEM_LIMIT),
        name="rwkv7_chunked",
    )(prw, s0, sh0, rp["mu"], rp["w0"], rp["wup"], rp["a0"], rp["aup"], rp["gup"],
      rp["k_k"], rp["k_a"], rp["r_k"], rp["ln_w"], rp["ln_b"], rp["ones"], tri)


def _rwkv_step_kernel(p_ref, sh_ref, s_ref, mu_ref, w0_ref, wup_ref, a0_ref, aup_ref, gup_ref,
                      kk_ref, ka_ref, rk_ref, lnw_ref, lnb_ref, ones_ref, o_ref, sn_ref):
    pf = p_ref[...]
    xm = pf + (sh_ref[...] - pf) * mu_ref[...]
    r, k2, v, kk, a, ld, g, bonus = _rwkv_token_math(
        xm, w0_ref[...], wup_ref, a0_ref[...], aup_ref, gup_ref, kk_ref[...], ka_ref[...], rk_ref[...], ones_ref)
    w = jnp.exp(ld)
    beta = kk * a
    N = RWKV_HEAD
    eye = (lax.broadcasted_iota(jnp.int32, (N, N), 0) == lax.broadcasted_iota(jnp.int32, (N, N), 1)).astype(F32)
    ys = []
    for h in range(RWKV_HEADS):
        sl = slice(N * h, N * (h + 1))
        s = s_ref[:, h]
        sa = jnp.sum(s * (-kk[:, sl])[:, None, :], axis=-1, keepdims=True)
        vcol = jnp.sum(eye[None] * v[:, sl][:, None, :], axis=-1, keepdims=True)
        sn = s * w[:, sl][:, None, :] + sa * beta[:, sl][:, None, :] + vcol * k2[:, sl][:, None, :]
        sn_ref[:, h] = sn
        ycol = jnp.sum(sn * r[:, sl][:, None, :], axis=-1, keepdims=True)
        ys.append(jnp.sum(eye[None] * ycol, axis=1))
    y = jnp.concatenate(ys, axis=-1)
    o_ref[...] = _rwkv_out(y, bonus, g, lnw_ref[...], lnb_ref[...], ones_ref)


def _rwkv_step(prw, rp, s0, sh0, bb):
    B = prw.shape[0]
    W = RWKV_WIDTH
    N = RWKV_HEAD
    return pl.pallas_call(
        _rwkv_step_kernel,
        grid=(B // bb,),
        in_specs=[
            pl.BlockSpec((bb, RWKV_PROJ), lambda i: (i, 0)),
            pl.BlockSpec((bb, RWKV_PROJ), lambda i: (i, 0)),
            pl.BlockSpec((bb, RWKV_HEADS, N, N), lambda i: (i, 0, 0, 0)),
            _const_spec((1, RWKV_PROJ)),
            _const_spec((1, W)),
            _const_spec((LANE, W)),
            _const_spec((1, W)),
            _const_spec((LANE, W)),
            _const_spec((GATE_RANK, W)),
            _const_spec((1, W)),
            _const_spec((1, W)),
            _const_spec((1, W)),
            _const_spec((1, W)),
            _const_spec((1, W)),
            _const_spec((LANE, LANE)),
        ],
        out_specs=[
            pl.BlockSpec((bb, W), lambda i: (i, 0)),
            pl.BlockSpec((bb, RWKV_HEADS, N, N), lambda i: (i, 0, 0, 0)),
        ],
        out_shape=[
            jax.ShapeDtypeStruct((B, W), F32),
            jax.ShapeDtypeStruct((B, RWKV_HEADS, N, N), F32),
        ],
        compiler_params=pltpu.CompilerParams(
            dimension_semantics=("parallel",), vmem_limit_bytes=VMEM_LIMIT),
        name="rwkv7_step",
    )(prw, sh0, s0, rp["mu"], rp["w0"], rp["wup"], rp["a0"], rp["aup"], rp["gup"],
      rp["k_k"], rp["k_a"], rp["r_k"], rp["ln_w"], rp["ln_b"], rp["ones"])


def _block_diag(blocks):
    n, r, c = blocks.shape
    eye = jnp.eye(n, dtype=blocks.dtype)
    return (blocks[:, :, None, :] * eye[:, None, :, None]).reshape(n * r, n * c)


def _s5_layout(ab, bbr, bbi, c_re, c_im):
    G, C, N = S5_GROUPS, S5_GROUP, S5_STATE
    gl = 8

    def in_side(bb):
        return jnp.stack([_block_diag(bb[gl * j:gl * (j + 1)]) for j in range(G // gl)])

    def out_side(cc):
        ct = jnp.swapaxes(cc, 1, 2)
        return jnp.stack([_block_diag(ct[gl * j:gl * (j + 1)]) for j in range(G // gl)])

    win = jnp.concatenate([in_side(bbr), in_side(bbi)], axis=-1).astype(BF16)
    return {
        "ab": ab.reshape(2, S5_LANES),
        "win": win,
        "cre": out_side(c_re).astype(BF16),
        "cim": out_side(c_im).astype(BF16),
    }


def _row(x):
    return x.reshape(1, -1).astype(F32)


def _layer_params(l, a):
    ab, bbr, bbi = _s5_prep(a["s5_lambda_re"][l], a["s5_lambda_im"][l], a["s5_log_dt"][l],
                            a["s5_b_re"][l], a["s5_b_im"][l])
    sp = _s5_layout(ab, bbr, bbi, a["s5_c_re"][l], a["s5_c_im"][l])
    sp.update({
        "d": _row(a["s5_d"][l]),
        "glu_w": a["s5_glu_w"][l].astype(BF16),
        "glu_b": _row(a["s5_glu_b"][l]),
        "conv_w": a["lru_conv_w"][l].astype(F32),
        "conv_b": _row(a["lru_conv_b"][l]),
        "wa": _block_diag(a["lru_w_a"][l]).astype(BF16),
        "wx": _block_diag(a["lru_w_x"][l]).astype(BF16),
        "ba": _row(a["lru_b_a"][l]),
        "bx": _row(a["lru_b_x"][l]),
        "lam": _row(a["lru_lambda"][l]),
    })
    zpad = jnp.zeros((DECAY_RANK, RWKV_WIDTH), F32)
    rp = {
        "mu": _row(a["rwkv_mu"][l]),
        "w0": _row(a["rwkv_w0"][l]),
        "wup": jnp.concatenate([a["rwkv_w_up"][l], zpad], axis=0).astype(BF16),
        "a0": _row(a["rwkv_a0"][l]),
        "aup": jnp.concatenate([zpad, a["rwkv_a_up"][l]], axis=0).astype(BF16),
        "gup": a["rwkv_g_up"][l].astype(BF16),
        "k_k": _row(a["rwkv_k_k"][l]),
        "k_a": _row(a["rwkv_k_a"][l]),
        "r_k": _row(a["rwkv_r_k"][l]),
        "ln_w": _row(a["rwkv_ln_w"][l]),
        "ln_b": _row(a["rwkv_ln_b"][l]),
        "ones": _block_diag(jnp.ones((2, RWKV_HEAD, RWKV_HEAD), F32)).astype(BF16),
    }
    tp = {
        "n1": _row(a["ffn1_norm"][l]),
        "wg1": a["ffn1_w_gate"][l].astype(BF16),
        "wu1": a["ffn1_w_up"][l].astype(BF16),
        "wd1": a["ffn1_w_down"][l].astype(BF16),
        "nm": _row(a["mix_norm"][l]),
        "win": a["w_in"][l].astype(BF16),
        "wout": a["w_out"][l].astype(BF16),
        "n2": _row(a["ffn2_norm"][l]),
        "wg2": a["ffn2_w_gate"][l].astype(BF16),
        "wu2": a["ffn2_w_up"][l].astype(BF16),
        "wd2": a["ffn2_w_down"][l].astype(BF16),
    }
    return tp, sp, rp


def _pairs_to_heads(sp):
    N = RWKV_HEAD
    blocks = [sp[:, :, N * i:N * (i + 1), N * i:N * (i + 1)] for i in range(2)]
    return jnp.stack(blocks, axis=2).reshape(sp.shape[0], RWKV_HEADS, N, N)


def _heads_to_pairs(sh):
    B = sh.shape[0]
    N = RWKV_HEAD
    s = sh.reshape(B, RWKV_PAIRS, 2, N, N)
    eye = jnp.eye(2, dtype=sh.dtype)
    return (s[:, :, :, :, None, :] * eye[None, None, :, None, :, None]).reshape(B, RWKV_PAIRS, 2 * N, 2 * N)


def _seq_layer(x, st, tp, sp, rp, fnorm, tl, Tc, final):
    B, L, _ = x.shape
    h, prw, psl = _k1(x, tp["n1"], tp["wg1"], tp["wu1"], tp["wd1"], tp["nm"], tp["win"], tl)
    osl, hT, lT, cT = _s5lru(psl.reshape(L, B, SL_IN), sp, st["s5"], st["lru"], st["conv"], Tc)
    orw, sT = _rwkv_chunk(prw, rp, st["rwkv"], st["shift"], Tc)
    y = _k3(h, osl.reshape(L, B * SL_OUT), orw, tp["wout"], tp["n2"], tp["wg2"], tp["wu2"], tp["wd2"],
            fnorm, tl, final)
    new = {"s5": hT, "lru": lT, "conv": cT, "rwkv": sT, "shift": prw[:, L - 1:L, :]}
    return y, new


def _step_layer(x, st, tp, sp, rp, fnorm, final):
    B = x.shape[0]
    h, prw, psl = _k1(x[None], tp["n1"], tp["wg1"], tp["wu1"], tp["wd1"], tp["nm"], tp["win"], B)
    osl, hT, lT, cT = _s5lru(psl.reshape(1, B, SL_IN), sp, st["s5"], st["lru"], st["conv"], 1)
    orw, sT = _rwkv_step(prw[0], rp, st["rwkv"], st["shift"], 16)
    y = _k3(h, osl.reshape(B, SL_OUT), orw[None], tp["wout"], tp["n2"], tp["wg2"], tp["wu2"], tp["wd2"],
            fnorm, B, final)
    new = {"s5": hT, "lru": lT, "conv": cT, "rwkv": sT, "shift": prw[0]}
    return y[0], new


def kernel(x_prompt, x_sample, state_s5_re, state_s5_im, state_rwkv, state_rwkv_shift, state_lru, state_lru_conv, meta_tokens, ffn1_norm, ffn1_w_gate, ffn1_w_up, ffn1_w_down, mix_norm, w_in, s5_lambda_re, s5_lambda_im, s5_log_dt, s5_b_re, s5_b_im, s5_c_re, s5_c_im, s5_d, s5_glu_w, s5_glu_b, rwkv_mu, rwkv_w0, rwkv_w_up, rwkv_a0, rwkv_a_up, rwkv_g_up, rwkv_k_k, rwkv_k_a, rwkv_r_k, rwkv_ln_w, rwkv_ln_b, lru_conv_w, lru_conv_b, lru_w_a, lru_b_a, lru_w_x, lru_b_x, lru_lambda, w_out, ffn2_norm, ffn2_w_gate, ffn2_w_up, ffn2_w_down, final_norm):
    a = dict(locals())
    BP, SEQ, _ = x_prompt.shape
    BS = x_sample.shape[0]
    hist = CONV_WIDTH - 1
    fnorm = _row(final_norm)

    xm = jnp.broadcast_to(meta_tokens.astype(F32)[None], (BP, N_META, D_MODEL))
    xp = x_prompt
    xs = x_sample.reshape(BS, D_MODEL)
    p_states = []
    s_states = []
    for l in range(DEPTH):
        tp, sp, rp = _layer_params(l, a)
        final = l == DEPTH - 1
        zero = {
            "s5": jnp.zeros((2, BP, S5_LANES), F32),
            "lru": jnp.zeros((BP, LRU_WIDTH), F32),
            "conv": jnp.zeros((hist, BP, LRU_WIDTH), F32),
            "rwkv": jnp.zeros((BP, RWKV_PAIRS, LANE, LANE), F32),
            "shift": jnp.zeros((BP, 1, RWKV_PROJ), F32),
        }
        xm, st_meta = _seq_layer(xm, zero, tp, sp, rp, fnorm, N_META, N_META, False)
        xp, st_p = _seq_layer(xp, st_meta, tp, sp, rp, fnorm, 256, 64, final)
        p_states.append(st_p)

        st_in = {
            "s5": jnp.stack([state_s5_re[l].reshape(BS, S5_LANES), state_s5_im[l].reshape(BS, S5_LANES)]),
            "lru": state_lru[l],
            "conv": jnp.swapaxes(state_lru_conv[l], 0, 1),
            "rwkv": state_rwkv[l],
            "shift": state_rwkv_shift[l],
        }
        xs, st_s = _step_layer(xs, st_in, tp, sp, rp, fnorm, final)
        s_states.append(st_s)

    def stack(states, B, rwkv_fn, shift_fn):
        return (
            jnp.stack([s["s5"][0].reshape(B, S5_GROUPS, S5_STATE) for s in states]),
            jnp.stack([s["s5"][1].reshape(B, S5_GROUPS, S5_STATE) for s in states]),
            jnp.stack([rwkv_fn(s["rwkv"]) for s in states]),
            jnp.stack([shift_fn(s["shift"]) for s in states]),
            jnp.stack([s["lru"] for s in states]),
            jnp.stack([jnp.swapaxes(s["conv"], 0, 1) for s in states]),
        )

    p_out = stack(p_states, BP, _pairs_to_heads, lambda s: s[:, 0, :])
    s_out = stack(s_states, BS, lambda s: s, lambda s: s)
    return (xp, xs.reshape(BS, 1, D_MODEL)) + p_out + s_out
```

```python
import functools
import math

import jax
import jax.numpy as jnp
from jax import lax
from jax.experimental import pallas as pl
from jax.experimental.pallas import tpu as pltpu

F32 = jnp.float32
BF16 = jnp.bfloat16

D_MODEL = 1024
DEPTH = 2
N_META = 16
D_FF = 2816
NORM_EPS = 1e-6
FFN_RES_SCALE = 0.5
S5_WIDTH = 384
S5_GROUP = 16
S5_GROUPS = 24
S5_STATE = 64
S5_LANES = S5_GROUPS * S5_STATE
RWKV_HEAD = 64
RWKV_WIDTH = 384
RWKV_HEADS = 6
RWKV_PAIRS = 3
DECAY_RANK = 64
ICL_RANK = 64
GATE_RANK = 128
RWKV_PROJ = 1408
RWKV_LN_EPS = 64e-5
LRU_WIDTH = 256
LRU_BLOCKS = 4
LRU_BLOCK = 64
CONV_WIDTH = 4
LRU_C = 8.0
D_IN = 2304
SL_IN = S5_WIDTH + 2 * LRU_WIDTH
SL_OUT = S5_WIDTH + LRU_WIDTH
LANE = 128

V7X_VMEM_BYTES = 64 * 2 ** 20
VMEM_LIMIT = (V7X_VMEM_BYTES * 7) // 8


def _dot(a, b):
    return jnp.dot(a, b, preferred_element_type=F32)


def _dot_nt(a, b):
    return lax.dot_general(a, b, (((1,), (1,)), ((), ())), preferred_element_type=F32)


def _dot_tn(a, b):
    return lax.dot_general(a, b, (((0,), (0,)), ((), ())), preferred_element_type=F32)


def _split2(x):
    hi = x.astype(BF16)
    lo = (x - hi.astype(F32)).astype(BF16)
    return hi, lo


def _split3(x):
    hi = x.astype(BF16)
    r = x - hi.astype(F32)
    mid = r.astype(BF16)
    lo = (r - mid.astype(F32)).astype(BF16)
    return hi, mid, lo


def _sigmoid(x):
    return 1.0 / (1.0 + jnp.exp(-x))


def _gelu(x):
    c = math.sqrt(2.0 / math.pi)
    return 0.5 * x * (1.0 + jnp.tanh(c * (x + 0.044715 * (x * x * x))))


def _neg_expm1(z):
    t = jnp.tanh(0.5 * z)
    return -2.0 * t / (1.0 - t)


def _softplus(x):
    return jnp.maximum(x, 0.0) + jnp.log1p(jnp.exp(-jnp.abs(x)))


def _rms(x, g):
    return x * lax.rsqrt(jnp.mean(x * x, axis=-1, keepdims=True) + NORM_EPS) * g


def _swiglu_res(x, g_norm, wg_ref, wu_ref, wd_ref):
    xn = _rms(x, g_norm).astype(BF16)
    g = _dot(xn, wg_ref[...])
    u = _dot(xn, wu_ref[...])
    a = (g * _sigmoid(g) * u).astype(BF16)
    return x + FFN_RES_SCALE * _dot(a, wd_ref[...])


def _const_spec(shape):
    nd = len(shape)
    return pl.BlockSpec(shape, lambda *_: (0,) * nd, pipeline_mode=pl.Buffered(1))


def _k1_kernel(x_ref, n1_ref, wg_ref, wu_ref, wd_ref, nm_ref, win_ref, h_ref, prw_ref, psl_ref):
    x = x_ref[...]
    h = _swiglu_res(x, n1_ref[...], wg_ref, wu_ref, wd_ref)
    h_ref[...] = h
    hn = _rms(h, nm_ref[...]).astype(BF16)
    proj = _dot(hn, win_ref[...])
    psl_ref[:, 0:S5_WIDTH] = proj[:, 0:S5_WIDTH]
    prw_ref[...] = proj[:, S5_WIDTH:S5_WIDTH + RWKV_PROJ]
    psl_ref[:, S5_WIDTH:SL_IN] = proj[:, S5_WIDTH + RWKV_PROJ:D_IN]


def _k1(x, n1, wg, wu, wd, nm, win, tl):
    B, L, _ = x.shape
    grid = (B, L // tl)
    return pl.pallas_call(
        _k1_kernel,
        grid=grid,
        in_specs=[
            pl.BlockSpec((None, tl, D_MODEL), lambda b, i: (b, i, 0)),
            _const_spec((1, D_MODEL)),
            _const_spec((D_MODEL, D_FF)),
            _const_spec((D_MODEL, D_FF)),
            _const_spec((D_FF, D_MODEL)),
            _const_spec((1, D_MODEL)),
            _const_spec((D_MODEL, D_IN)),
        ],
        out_specs=[
            pl.BlockSpec((None, tl, D_MODEL), lambda b, i: (b, i, 0)),
            pl.BlockSpec((None, tl, RWKV_PROJ), lambda b, i: (b, i, 0)),
            pl.BlockSpec((tl, SL_IN), lambda b, i: (i, b)),
        ],
        out_shape=[
            jax.ShapeDtypeStruct((B, L, D_MODEL), F32),
            jax.ShapeDtypeStruct((B, L, RWKV_PROJ), F32),
            jax.ShapeDtypeStruct((L, B * SL_IN), F32),
        ],
        compiler_params=pltpu.CompilerParams(
            dimension_semantics=("parallel", "parallel"), vmem_limit_bytes=VMEM_LIMIT),
        name="ffn1_inproj",
    )(x, n1, wg, wu, wd, nm, win)


def _k3_kernel(h_ref, osl_ref, orw_ref, wout_ref, n2_ref, wg_ref, wu_ref, wd_ref, fn_ref, o_ref, *, final):
    osl = osl_ref[...]
    mix = (_dot(osl[:, 0:S5_WIDTH].astype(BF16), wout_ref[0:S5_WIDTH, :])
           + _dot(orw_ref[...].astype(BF16), wout_ref[S5_WIDTH:S5_WIDTH + RWKV_WIDTH, :])
           + _dot(osl[:, S5_WIDTH:SL_OUT].astype(BF16), wout_ref[S5_WIDTH + RWKV_WIDTH:D_MODEL, :]))
    h2 = h_ref[...] + mix
    h3 = _swiglu_res(h2, n2_ref[...], wg_ref, wu_ref, wd_ref)
    if final:
        h3 = _rms(h3, fn_ref[...])
    o_ref[...] = h3


def _k3(h, osl, orw, wout, n2, wg, wu, wd, fnorm, tl, final):
    B, L, _ = h.shape
    grid = (B, L // tl)
    return pl.pallas_call(
        functools.partial(_k3_kernel, final=final),
        grid=grid,
        in_specs=[
            pl.BlockSpec((None, tl, D_MODEL), lambda b, i: (b, i, 0)),
            pl.BlockSpec((tl, SL_OUT), lambda b, i: (i, b)),
            pl.BlockSpec((None, tl, RWKV_WIDTH), lambda b, i: (b, i, 0)),
            _const_spec((D_MODEL, D_MODEL)),
            _const_spec((1, D_MODEL)),
            _const_spec((D_MODEL, D_FF)),
            _const_spec((D_MODEL, D_FF)),
            _const_spec((D_FF, D_MODEL)),
            _const_spec((1, D_MODEL)),
        ],
        out_specs=pl.BlockSpec((None, tl, D_MODEL), lambda b, i: (b, i, 0)),
        out_shape=jax.ShapeDtypeStruct((B, L, D_MODEL), F32),
        compiler_params=pltpu.CompilerParams(
            dimension_semantics=("parallel", "parallel"), vmem_limit_bytes=VMEM_LIMIT),
        name="outproj_ffn2",
    )(h, osl, orw, wout, n2, wg, wu, wd, fnorm)


def _s5_prep_kernel(lr_ref, li_ref, ldt_ref, br_ref, bi_ref, ab_ref, bbr_ref, bbi_ref):
    lr = lr_ref[...]
    li = li_ref[...]
    dt = jnp.exp(ldt_ref[...])
    mag = jnp.exp(lr * dt)
    ab_re = mag * jnp.cos(li * dt)
    ab_im = mag * jnp.sin(li * dt)
    den = lr * lr + li * li
    f_re = ((ab_re - 1.0) * lr + ab_im * li) / den
    f_im = (ab_im * lr - (ab_re - 1.0) * li) / den
    ab_ref[0] = ab_re
    ab_ref[1] = ab_im
    br = br_ref[...]
    bi = bi_ref[...]
    bbr_ref[...] = f_re[:, None, :] * br - f_im[:, None, :] * bi
    bbi_ref[...] = f_re[:, None, :] * bi + f_im[:, None, :] * br


def _s5_prep(lam_re, lam_im, log_dt, b_re, b_im):
    G, N, C = S5_GROUPS, S5_STATE, S5_GROUP
    ldt = jnp.broadcast_to(log_dt[:, None], (G, N))
    br = jnp.swapaxes(b_re, 1, 2)
    bi = jnp.swapaxes(b_im, 1, 2)
    return pl.pallas_call(
        _s5_prep_kernel,
        out_shape=[
            jax.ShapeDtypeStruct((2, G, N), F32),
            jax.ShapeDtypeStruct((G, C, N), F32),
            jax.ShapeDtypeStruct((G, C, N), F32),
        ],
        name="s5_discretise",
    )(lam_re, lam_im, ldt, br, bi)


def _s5lru_kernel(psl_ref, ab_ref, win_ref, cre_ref, cim_ref, d_ref, gw_ref, gb_ref,
                  cw_ref, cb_ref, wa_ref, wx_ref, ba_ref, bx_ref, lam_ref,
                  h0_ref, l0_ref, c0_ref,
                  o_ref, hT_ref, lT_ref, cT_ref,
                  hs_scr, ls_scr, xc_scr, xr_scr, xi_scr, la_scr, lb_scr, *, B, Tc, unroll):
    c = pl.program_id(0)
    R = Tc * B
    hist = CONV_WIDTH - 1

    @pl.when(c == 0)
    def _():
        hs_scr[...] = h0_ref[...]
        ls_scr[...] = l0_ref[...]
        xc_scr[0:hist] = c0_ref[...]

    psl = psl_ref[...]
    u = psl[:, :, 0:S5_WIDTH].reshape(R, S5_WIDTH)
    gate_in = psl[:, :, S5_WIDTH + LRU_WIDTH:SL_IN].reshape(R, LRU_WIDTH)

    ub = u.astype(BF16)
    half = 4 * LANE
    for j in range(3):
        x = _dot(ub[:, LANE * j:LANE * (j + 1)], win_ref[j])
        xr_scr[:, :, half * j:half * (j + 1)] = x[:, 0:half].reshape(Tc, B, half)
        xi_scr[:, :, half * j:half * (j + 1)] = x[:, half:2 * half].reshape(Tc, B, half)

    xc_scr[hist:hist + Tc] = psl[:, :, S5_WIDTH:S5_WIDTH + LRU_WIDTH]
    cw = cw_ref[...]
    xc = cb_ref[...] + xc_scr[0:Tc] * cw[0:1]
    for j in range(1, CONV_WIDTH):
        xc = xc + xc_scr[j:j + Tc] * cw[j:j + 1]
    new_hist = xc_scr[Tc:Tc + hist]
    xc_scr[0:hist] = new_hist
    xc2 = xc.reshape(R, LRU_WIDTH)
    xcb = xc2.astype(BF16)
    gate_a = _sigmoid(_dot(xcb, wa_ref[...]) + ba_ref[...])
    gate_x = _sigmoid(_dot(xcb, wx_ref[...]) + bx_ref[...])
    log_a = LRU_C * gate_a * (-_softplus(-lam_ref[...]))
    la_scr[...] = jnp.exp(log_a).reshape(Tc, B, LRU_WIDTH)
    lb_scr[...] = (jnp.sqrt(_neg_expm1(2.0 * log_a)) * (gate_x * xc2)).reshape(Tc, B, LRU_WIDTH)

    ar = jnp.broadcast_to(ab_ref[0:1, :], (B, S5_LANES))
    ai = jnp.broadcast_to(ab_ref[1:2, :], (B, S5_LANES))

    def step(t, carry):
        hr, hi, hl = carry
        nr = ar * hr - ai * hi + xr_scr[t]
        ni = ar * hi + ai * hr + xi_scr[t]
        xr_scr[t] = nr
        xi_scr[t] = ni
        nl = la_scr[t] * hl + lb_scr[t]
        lb_scr[t] = nl
        return nr, ni, nl

    hr, hi, hl = lax.fori_loop(0, Tc, step, (hs_scr[0], hs_scr[1], ls_scr[...]), unroll=unroll)
    hs_scr[0] = hr
    hs_scr[1] = hi
    ls_scr[...] = hl

    hrb = xr_scr[...].reshape(R, S5_LANES).astype(BF16)
    hib = xi_scr[...].reshape(R, S5_LANES).astype(BF16)
    ys = []
    for j in range(3):
        sl = slice(half * j, half * (j + 1))
        ys.append(_dot(hrb[:, sl], cre_ref[j]) - _dot(hib[:, sl], cim_ref[j]))
    y = jnp.concatenate(ys, axis=-1) + d_ref[...] * u
    z = _gelu(y)
    o_s5 = z * _sigmoid(_dot(z.astype(BF16), gw_ref[...]) + gb_ref[...])
    o_lru = lb_scr[...].reshape(R, LRU_WIDTH) * _gelu(gate_in)
    o_ref[:, :, 0:S5_WIDTH] = o_s5.reshape(Tc, B, S5_WIDTH)
    o_ref[:, :, S5_WIDTH:SL_OUT] = o_lru.reshape(Tc, B, LRU_WIDTH)

    @pl.when(c == pl.num_programs(0) - 1)
    def _():
        hT_ref[...] = hs_scr[...]
        lT_ref[...] = ls_scr[...]
        cT_ref[...] = xc_scr[0:hist]


def _s5lru(psl3, sp, h0, l0, c0, Tc):
    L, B, _ = psl3.shape
    hist = CONV_WIDTH - 1
    unroll = True if Tc <= 2 else 2
    return pl.pallas_call(
        functools.partial(_s5lru_kernel, B=B, Tc=Tc, unroll=unroll),
        grid=(L // Tc,),
        in_specs=[
            pl.BlockSpec((Tc, B, SL_IN), lambda c: (c, 0, 0)),
            _const_spec((2, S5_LANES)),
            _const_spec((3, LANE, 8 * LANE)),
            _const_spec((3, 4 * LANE, LANE)),
            _const_spec((3, 4 * LANE, LANE)),
            _const_spec((1, S5_WIDTH)),
            _const_spec((S5_WIDTH, S5_WIDTH)),
            _const_spec((1, S5_WIDTH)),
            _const_spec((CONV_WIDTH, LRU_WIDTH)),
            _const_spec((1, LRU_WIDTH)),
            _const_spec((LRU_WIDTH, LRU_WIDTH)),
            _const_spec((LRU_WIDTH, LRU_WIDTH)),
            _const_spec((1, LRU_WIDTH)),
            _const_spec((1, LRU_WIDTH)),
            _const_spec((1, LRU_WIDTH)),
            _const_spec((2, B, S5_LANES)),
            _const_spec((B, LRU_WIDTH)),
            _const_spec((hist, B, LRU_WIDTH)),
        ],
        out_specs=[
            pl.BlockSpec((Tc, B, SL_OUT), lambda c: (c, 0, 0)),
            pl.BlockSpec((2, B, S5_LANES), lambda c: (0, 0, 0)),
            pl.BlockSpec((B, LRU_WIDTH), lambda c: (0, 0)),
            pl.BlockSpec((hist, B, LRU_WIDTH), lambda c: (0, 0, 0)),
        ],
        out_shape=[
            jax.ShapeDtypeStruct((L, B, SL_OUT), F32),
            jax.ShapeDtypeStruct((2, B, S5_LANES), F32),
            jax.ShapeDtypeStruct((B, LRU_WIDTH), F32),
            jax.ShapeDtypeStruct((hist, B, LRU_WIDTH), F32),
        ],
        scratch_shapes=[
            pltpu.VMEM((2, B, S5_LANES), F32),
            pltpu.VMEM((B, LRU_WIDTH), F32),
            pltpu.VMEM((Tc + hist, B, LRU_WIDTH), F32),
            pltpu.VMEM((Tc, B, S5_LANES), F32),
            pltpu.VMEM((Tc, B, S5_LANES), F32),
            pltpu.VMEM((Tc, B, LRU_WIDTH), F32),
            pltpu.VMEM((Tc, B, LRU_WIDTH), F32),
        ],
        compiler_params=pltpu.CompilerParams(
            dimension_semantics=("arbitrary",), vmem_limit_bytes=VMEM_LIMIT),
        name="s5_rglru_scan",
    )(psl3, sp["ab"], sp["win"], sp["cre"], sp["cim"], sp["d"], sp["glu_w"], sp["glu_b"],
      sp["conv_w"], sp["conv_b"], sp["wa"], sp["wx"], sp["ba"], sp["bx"], sp["lam"], h0, l0, c0)


def _segsum(x, ones_ref):
    outs = []
    for p in range(RWKV_PAIRS):
        hi, lo = _split2(x[:, LANE * p:LANE * (p + 1)])
        outs.append(_dot(hi, ones_ref[...]) + _dot(lo, ones_ref[...]))
    return jnp.concatenate(outs, axis=-1)


def _rwkv_token_math(xm, w0, wup_ref, a0, aup_ref, gup_ref, k_k, k_a, r_k, ones_ref):
    W = RWKV_WIDTH
    r = xm[:, 0:W]
    k = xm[:, W:2 * W]
    v = xm[:, 2 * W:3 * W]
    xwa = xm[:, 3 * W:3 * W + DECAY_RANK + ICL_RANK]
    xg = xm[:, 3 * W + DECAY_RANK + ICL_RANK:RWKV_PROJ]
    lw = w0 + _dot(jnp.tanh(xwa).astype(BF16), wup_ref[...])
    logw = -_softplus(-lw) - 0.5
    ld = -jnp.exp(logw)
    a = _sigmoid(a0 + _dot(xwa.astype(BF16), aup_ref[...]))
    g = _dot(_sigmoid(xg).astype(BF16), gup_ref[...])
    kkr = k * k_k
    kk = kkr / jnp.maximum(jnp.sqrt(_segsum(kkr * kkr, ones_ref)), 1e-12)
    k2 = k * (1.0 + (a - 1.0) * k_a)
    bonus = _segsum(r * k2 * r_k, ones_ref) * v
    return r, k2, v, kk, a, ld, g, bonus


def _rwkv_out(y, bonus, g, ln_w, ln_b, ones_ref):
    inv = 1.0 / RWKV_HEAD
    mean = _segsum(y, ones_ref) * inv
    d = y - mean
    var = _segsum(d * d, ones_ref) * inv
    yn = d * lax.rsqrt(var + RWKV_LN_EPS) * ln_w + ln_b
    return (yn + bonus) * g


def _rwkv_chunk_kernel(p_ref, s0_ref, sh0_ref, mu_ref, w0_ref, wup_ref, a0_ref, aup_ref, gup_ref,
                       kk_ref, ka_ref, rk_ref, lnw_ref, lnb_ref, ones_ref, tri_ref,
                       o_ref, sT_ref,
                       st_scr, xs_scr, a_scr, b_scr, k_scr, r_scr, v_scr, bp_scr, kp_scr, pt_scr, y_scr,
                       *, B, Tc):
    c = pl.program_id(0)
    R = B * Tc
    W = RWKV_WIDTH

    @pl.when(c == 0)
    def _():
        st_scr[...] = s0_ref[...]
        xs_scr[:, 7:8, :] = sh0_ref[...]

    p = p_ref[...]
    xs_scr[:, 8:8 + Tc, :] = p
    prev = xs_scr[:, 7:7 + Tc, :]
    xs_scr[:, 7:8, :] = p[:, Tc - 1:Tc, :]
    pf = p.reshape(R, RWKV_PROJ)
    xm = pf + (prev.reshape(R, RWKV_PROJ) - pf) * mu_ref[...]
    r, k2, v, kk, a, ld, g, bonus = _rwkv_token_math(
        xm, w0_ref[...], wup_ref, a0_ref[...], aup_ref, gup_ref, kk_ref[...], ka_ref[...], rk_ref[...], ones_ref)

    h1, h2, h3 = _split3(ld)
    tri = tri_ref[...]
    cls = []
    for b in range(B):
        rows = slice(b * Tc, (b + 1) * Tc)
        cls.append(_dot(tri, h1[rows]) + _dot(tri, h2[rows]) + _dot(tri, h3[rows]))
    cl = jnp.concatenate(cls, axis=0)
    cl3 = cl.reshape(B, Tc, W)
    clT = cl3[:, Tc - 1:Tc, :]
    to_end = jnp.exp(clT - cl3).reshape(R, W)
    inv_p = jnp.exp(-cl)
    beta = kk * a
    a_scr[...] = (-kk * jnp.exp(cl - ld)).astype(BF16)
    b_scr[...] = (beta * inv_p).astype(BF16)
    k_scr[...] = (k2 * inv_p).astype(BF16)
    r_scr[...] = (r * jnp.exp(cl)).astype(BF16)
    v_scr[...] = v.astype(BF16)
    bp_scr[...] = (beta * to_end).astype(BF16)
    kp_scr[...] = (k2 * to_end).astype(BF16)
    pt_scr[...] = jnp.exp(clT)

    lane = lax.broadcasted_iota(jnp.int32, (1, LANE), 1)
    m0 = (lane < RWKV_HEAD).astype(BF16)
    m1 = (lane >= RWKV_HEAD).astype(BF16)
    assert Tc & (Tc - 1) == 0
    ti = lax.broadcasted_iota(jnp.int32, (2 * Tc, 2 * Tc), 0) & (Tc - 1)
    si = lax.broadcasted_iota(jnp.int32, (2 * Tc, 2 * Tc), 1) & (Tc - 1)
    strict = (si < ti).astype(F32)
    incl = (si <= ti).astype(F32)
    n_doubling = max(1, (Tc - 1).bit_length())

    def stack2(x):
        return jnp.concatenate([x * m0, x * m1], axis=0)

    probs = [(b, pr) for b in range(B) for pr in range(RWKV_PAIRS)]

    def operand(scr, b, pr):
        return stack2(scr[b * Tc:(b + 1) * Tc, LANE * pr:LANE * (pr + 1)])

    v2s, gss, ns, wrbs, wrks, xs = [], [], [], [], [], []
    for b, pr in probs:
        lhs = jnp.concatenate([operand(a_scr, b, pr), operand(r_scr, b, pr)], axis=0)
        v2 = operand(v_scr, b, pr)
        gb = _dot_nt(lhs, operand(b_scr, b, pr))
        gk = _dot_nt(lhs, operand(k_scr, b, pr))
        gs = _dot_nt(lhs, st_scr[b, pr].astype(BF16))
        m = (gk[0:2 * Tc] * strict).astype(BF16)
        v2s.append(v2)
        gss.append(gs[2 * Tc:4 * Tc])
        ns.append((gb[0:2 * Tc] * strict).astype(BF16))
        wrbs.append((gb[2 * Tc:4 * Tc] * incl).astype(BF16))
        wrks.append((gk[2 * Tc:4 * Tc] * incl).astype(BF16))
        xs.append(gs[0:2 * Tc] + _dot(m, v2))
    for j in range(n_doubling):
        for i in range(len(probs)):
            xs[i] = xs[i] + _dot(ns[i], xs[i].astype(BF16))
        if j + 1 < n_doubling:
            for i in range(len(probs)):
                ns[i] = _dot(ns[i], ns[i]).astype(BF16)
    for i, (b, pr) in enumerate(probs):
        sl = slice(LANE * pr, LANE * (pr + 1))
        sab = xs[i].astype(BF16)
        y2 = gss[i] + _dot(wrbs[i], sab) + _dot(wrks[i], v2s[i])
        y_scr[b * Tc:(b + 1) * Tc, sl] = y2[0:Tc] + y2[Tc:2 * Tc]
        upd = _dot_tn(jnp.concatenate([sab, v2s[i]], axis=0),
                      jnp.concatenate([operand(bp_scr, b, pr), operand(kp_scr, b, pr)], axis=0))
        st_scr[b, pr] = st_scr[b, pr] * pt_scr[b, :, sl] + upd

    out = _rwkv_out(y_scr[...], bonus, g, lnw_ref[...], lnb_ref[...], ones_ref)
    o_ref[...] = out.reshape(B, Tc, W)

    @pl.when(c == pl.num_programs(0) - 1)
    def _():
        sT_ref[...] = st_scr[...]


def _rwkv_chunk(prw, rp, s0, sh0, Tc):
    B, L, _ = prw.shape
    R = B * Tc
    W = RWKV_WIDTH
    tri = jnp.tril(jnp.ones((Tc, Tc), F32)).astype(BF16)
    return pl.pallas_call(
        functools.partial(_rwkv_chunk_kernel, B=B, Tc=Tc),
        grid=(L // Tc,),
        in_specs=[
            pl.BlockSpec((B, Tc, RWKV_PROJ), lambda c: (0, c, 0)),
            _const_spec((B, RWKV_PAIRS, LANE, LANE)),
            _const_spec((B, 1, RWKV_PROJ)),
            _const_spec((1, RWKV_PROJ)),
            _const_spec((1, W)),
            _const_spec((LANE, W)),
            _const_spec((1, W)),
            _const_spec((LANE, W)),
            _const_spec((GATE_RANK, W)),
            _const_spec((1, W)),
            _const_spec((1, W)),
            _const_spec((1, W)),
            _const_spec((1, W)),
            _const_spec((1, W)),
            _const_spec((LANE, LANE)),
            _const_spec((Tc, Tc)),
        ],
        out_specs=[
            pl.BlockSpec((B, Tc, W), lambda c: (0, c, 0)),
            pl.BlockSpec((B, RWKV_PAIRS, LANE, LANE), lambda c: (0, 0, 0, 0)),
        ],
        out_shape=[
            jax.ShapeDtypeStruct((B, L, W), F32),
            jax.ShapeDtypeStruct((B, RWKV_PAIRS, LANE, LANE), F32),
        ],
        scratch_shapes=[
            pltpu.VMEM((B, RWKV_PAIRS, LANE, LANE), F32),
            pltpu.VMEM((B, Tc + 8, RWKV_PROJ), F32),
        ] + [pltpu.VMEM((R, W), BF16)] * 7 + [
            pltpu.VMEM((B, 1, W), F32),
            pltpu.VMEM((R, W), F32),
        ],
        compiler_params=pltpu.CompilerParams(
            dimension_semantics=("arbitrary",), vmem_limit_bytes=VMEM_LIMIT),
        name="rwkv7_chunked",
    )(prw, s0, sh0, rp["mu"], rp["w0"], rp["wup"], rp["a0"], rp["aup"], rp["gup"],
      rp["k_k"], rp["k_a"], rp["r_k"], rp["ln_w"], rp["ln_b"], rp["ones"], tri)


def _rwkv_step_kernel(p_ref, sh_ref, s_ref, mu_ref, w0_ref, wup_ref, a0_ref, aup_ref, gup_ref,
                      kk_ref, ka_ref, rk_ref, lnw_ref, lnb_ref, ones_ref, o_ref, sn_ref):
    pf = p_ref[...]
    xm = pf + (sh_ref[...] - pf) * mu_ref[...]
    r, k2, v, kk, a, ld, g, bonus = _rwkv_token_math(
        xm, w0_ref[...], wup_ref, a0_ref[...], aup_ref, gup_ref, kk_ref[...], ka_ref[...], rk_ref[...], ones_ref)
    w = jnp.exp(ld)
    beta = kk * a
    N = RWKV_HEAD
    eye = (lax.broadcasted_iota(jnp.int32, (N, N), 0) == lax.broadcasted_iota(jnp.int32, (N, N), 1)).astype(F32)
    ys = []
    for h in range(RWKV_HEADS):
        sl = slice(N * h, N * (h + 1))
        s = s_ref[:, h]
        sa = jnp.sum(s * (-kk[:, sl])[:, None, :], axis=-1, keepdims=True)
        vcol = jnp.sum(eye[None] * v[:, sl][:, None, :], axis=-1, keepdims=True)
        sn = s * w[:, sl][:, None, :] + sa * beta[:, sl][:, None, :] + vcol * k2[:, sl][:, None, :]
        sn_ref[:, h] = sn
        ycol = jnp.sum(sn * r[:, sl][:, None, :], axis=-1, keepdims=True)
        ys.append(jnp.sum(eye[None] * ycol, axis=1))
    y = jnp.concatenate(ys, axis=-1)
    o_ref[...] = _rwkv_out(y, bonus, g, lnw_ref[...], lnb_ref[...], ones_ref)


def _rwkv_step(prw, rp, s0, sh0, bb):
    B = prw.shape[0]
    W = RWKV_WIDTH
    N = RWKV_HEAD
    return pl.pallas_call(
        _rwkv_step_kernel,
        grid=(B // bb,),
        in_specs=[
            pl.BlockSpec((bb, RWKV_PROJ), lambda i: (i, 0)),
            pl.BlockSpec((bb, RWKV_PROJ), lambda i: (i, 0)),
            pl.BlockSpec((bb, RWKV_HEADS, N, N), lambda i: (i, 0, 0, 0)),
            _const_spec((1, RWKV_PROJ)),
            _const_spec((1, W)),
            _const_spec((LANE, W)),
            _const_spec((1, W)),
            _const_spec((LANE, W)),
            _const_spec((GATE_RANK, W)),
            _const_spec((1, W)),
            _const_spec((1, W)),
            _const_spec((1, W)),
            _const_spec((1, W)),
            _const_spec((1, W)),
            _const_spec((LANE, LANE)),
        ],
        out_specs=[
            pl.BlockSpec((bb, W), lambda i: (i, 0)),
            pl.BlockSpec((bb, RWKV_HEADS, N, N), lambda i: (i, 0, 0, 0)),
        ],
        out_shape=[
            jax.ShapeDtypeStruct((B, W), F32),
            jax.ShapeDtypeStruct((B, RWKV_HEADS, N, N), F32),
        ],
        compiler_params=pltpu.CompilerParams(
            dimension_semantics=("parallel",), vmem_limit_bytes=VMEM_LIMIT),
        name="rwkv7_step",
    )(prw, sh0, s0, rp["mu"], rp["w0"], rp["wup"], rp["a0"], rp["aup"], rp["gup"],
      rp["k_k"], rp["k_a"], rp["r_k"], rp["ln_w"], rp["ln_b"], rp["ones"])


def _block_diag(blocks):
    n, r, c = blocks.shape
    eye = jnp.eye(n, dtype=blocks.dtype)
    return (blocks[:, :, None, :] * eye[:, None, :, None]).reshape(n * r, n * c)


def _s5_layout(ab, bbr, bbi, c_re, c_im):
    G, C, N = S5_GROUPS, S5_GROUP, S5_STATE
    gl = 8

    def in_side(bb):
        return jnp.stack([_block_diag(bb[gl * j:gl * (j + 1)]) for j in range(G // gl)])

    def out_side(cc):
        ct = jnp.swapaxes(cc, 1, 2)
        return jnp.stack([_block_diag(ct[gl * j:gl * (j + 1)]) for j in range(G // gl)])

    win = jnp.concatenate([in_side(bbr), in_side(bbi)], axis=-1).astype(BF16)
    return {
        "ab": ab.reshape(2, S5_LANES),
        "win": win,
        "cre": out_side(c_re).astype(BF16),
        "cim": out_side(c_im).astype(BF16),
    }


def _row(x):
    return x.reshape(1, -1).astype(F32)


def _layer_params(l, a):
    ab, bbr, bbi = _s5_prep(a["s5_lambda_re"][l], a["s5_lambda_im"][l], a["s5_log_dt"][l],
                            a["s5_b_re"][l], a["s5_b_im"][l])
    sp = _s5_layout(ab, bbr, bbi, a["s5_c_re"][l], a["s5_c_im"][l])
    sp.update({
        "d": _row(a["s5_d"][l]),
        "glu_w": a["s5_glu_w"][l].astype(BF16),
        "glu_b": _row(a["s5_glu_b"][l]),
        "conv_w": a["lru_conv_w"][l].astype(F32),
        "conv_b": _row(a["lru_conv_b"][l]),
        "wa": _block_diag(a["lru_w_a"][l]).astype(BF16),
        "wx": _block_diag(a["lru_w_x"][l]).astype(BF16),
        "ba": _row(a["lru_b_a"][l]),
        "bx": _row(a["lru_b_x"][l]),
        "lam": _row(a["lru_lambda"][l]),
    })
    zpad = jnp.zeros((DECAY_RANK, RWKV_WIDTH), F32)
    rp = {
        "mu": _row(a["rwkv_mu"][l]),
        "w0": _row(a["rwkv_w0"][l]),
        "wup": jnp.concatenate([a["rwkv_w_up"][l], zpad], axis=0).astype(BF16),
        "a0": _row(a["rwkv_a0"][l]),
        "aup": jnp.concatenate([zpad, a["rwkv_a_up"][l]], axis=0).astype(BF16),
        "gup": a["rwkv_g_up"][l].astype(BF16),
        "k_k": _row(a["rwkv_k_k"][l]),
        "k_a": _row(a["rwkv_k_a"][l]),
        "r_k": _row(a["rwkv_r_k"][l]),
        "ln_w": _row(a["rwkv_ln_w"][l]),
        "ln_b": _row(a["rwkv_ln_b"][l]),
        "ones": _block_diag(jnp.ones((2, RWKV_HEAD, RWKV_HEAD), F32)).astype(BF16),
    }
    tp = {
        "n1": _row(a["ffn1_norm"][l]),
        "wg1": a["ffn1_w_gate"][l].astype(BF16),
        "wu1": a["ffn1_w_up"][l].astype(BF16),
        "wd1": a["ffn1_w_down"][l].astype(BF16),
        "nm": _row(a["mix_norm"][l]),
        "win": a["w_in"][l].astype(BF16),
        "wout": a["w_out"][l].astype(BF16),
        "n2": _row(a["ffn2_norm"][l]),
        "wg2": a["ffn2_w_gate"][l].astype(BF16),
        "wu2": a["ffn2_w_up"][l].astype(BF16),
        "wd2": a["ffn2_w_down"][l].astype(BF16),
    }
    return tp, sp, rp


def _pairs_to_heads(sp):
    N = RWKV_HEAD
    blocks = [sp[:, :, N * i:N * (i + 1), N * i:N * (i + 1)] for i in range(2)]
    return jnp.stack(blocks, axis=2).reshape(sp.shape[0], RWKV_HEADS, N, N)


def _heads_to_pairs(sh):
    B = sh.shape[0]
    N = RWKV_HEAD
    s = sh.reshape(B, RWKV_PAIRS, 2, N, N)
    eye = jnp.eye(2, dtype=sh.dtype)
    return (s[:, :, :, :, None, :] * eye[None, None, :, None, :, None]).reshape(B, RWKV_PAIRS, 2 * N, 2 * N)


def _seq_layer(x, st, tp, sp, rp, fnorm, tl, Tc, final):
    B, L, _ = x.shape
    h, prw, psl = _k1(x, tp["n1"], tp["wg1"], tp["wu1"], tp["wd1"], tp["nm"], tp["win"], tl)
    osl, hT, lT, cT = _s5lru(psl.reshape(L, B, SL_IN), sp, st["s5"], st["lru"], st["conv"], Tc)
    orw, sT = _rwkv_chunk(prw, rp, st["rwkv"], st["shift"], Tc)
    y = _k3(h, osl.reshape(L, B * SL_OUT), orw, tp["wout"], tp["n2"], tp["wg2"], tp["wu2"], tp["wd2"],
            fnorm, tl, final)
    new = {"s5": hT, "lru": lT, "conv": cT, "rwkv": sT, "shift": prw[:, L - 1:L, :]}
    return y, new


def _step_layer(x, st, tp, sp, rp, fnorm, final):
    B = x.shape[0]
    h, prw, psl = _k1(x[None], tp["n1"], tp["wg1"], tp["wu1"], tp["wd1"], tp["nm"], tp["win"], B)
    osl, hT, lT, cT = _s5lru(psl.reshape(1, B, SL_IN), sp, st["s5"], st["lru"], st["conv"], 1)
    orw, sT = _rwkv_step(prw[0], rp, st["rwkv"], st["shift"], 16)
    y = _k3(h, osl.reshape(B, SL_OUT), orw[None], tp["wout"], tp["n2"], tp["wg2"], tp["wu2"], tp["wd2"],
            fnorm, B, final)
    new = {"s5": hT, "lru": lT, "conv": cT, "rwkv": sT, "shift": prw[0]}
    return y[0], new


def kernel(x_prompt, x_sample, state_s5_re, state_s5_im, state_rwkv, state_rwkv_shift, state_lru, state_lru_conv, meta_tokens, ffn1_norm, ffn1_w_gate, ffn1_w_up, ffn1_w_down, mix_norm, w_in, s5_lambda_re, s5_lambda_im, s5_log_dt, s5_b_re, s5_b_im, s5_c_re, s5_c_im, s5_d, s5_glu_w, s5_glu_b, rwkv_mu, rwkv_w0, rwkv_w_up, rwkv_a0, rwkv_a_up, rwkv_g_up, rwkv_k_k, rwkv_k_a, rwkv_r_k, rwkv_ln_w, rwkv_ln_b, lru_conv_w, lru_conv_b, lru_w_a, lru_b_a, lru_w_x, lru_b_x, lru_lambda, w_out, ffn2_norm, ffn2_w_gate, ffn2_w_up, ffn2_w_down, final_norm):
    a = dict(locals())
    BP, SEQ, _ = x_prompt.shape
    BS = x_sample.shape[0]
    hist = CONV_WIDTH - 1
    fnorm = _row(final_norm)

    xm = jnp.broadcast_to(meta_tokens.astype(F32)[None], (BP, N_META, D_MODEL))
    xp = x_prompt
    xs = x_sample.reshape(BS, D_MODEL)
    p_states = []
    s_states = []
    for l in range(DEPTH):
        tp, sp, rp = _layer_params(l, a)
        final = l == DEPTH - 1
        zero = {
            "s5": jnp.zeros((2, BP, S5_LANES), F32),
            "lru": jnp.zeros((BP, LRU_WIDTH), F32),
            "conv": jnp.zeros((hist, BP, LRU_WIDTH), F32),
            "rwkv": jnp.zeros((BP, RWKV_PAIRS, LANE, LANE), F32),
            "shift": jnp.zeros((BP, 1, RWKV_PROJ), F32),
        }
        xm, st_meta = _seq_layer(xm, zero, tp, sp, rp, fnorm, N_META, N_META, False)
        xp, st_p = _seq_layer(xp, st_meta, tp, sp, rp, fnorm, 256, 64, final)
        p_states.append(st_p)

        st_in = {
            "s5": jnp.stack([state_s5_re[l].reshape(BS, S5_LANES), state_s5_im[l].reshape(BS, S5_LANES)]),
            "lru": state_lru[l],
            "conv": jnp.swapaxes(state_lru_conv[l], 0, 1),
            "rwkv": state_rwkv[l],
            "shift": state_rwkv_shift[l],
        }
        xs, st_s = _step_layer(xs, st_in, tp, sp, rp, fnorm, final)
        s_states.append(st_s)

    def stack(states, B, rwkv_fn, shift_fn):
        return (
            jnp.stack([s["s5"][0].reshape(B, S5_GROUPS, S5_STATE) for s in states]),
            jnp.stack([s["s5"][1].reshape(B, S5_GROUPS, S5_STATE) for s in states]),
            jnp.stack([rwkv_fn(s["rwkv"]) for s in states]),
            jnp.stack([shift_fn(s["shift"]) for s in states]),
            jnp.stack([s["lru"] for s in states]),
            jnp.stack([jnp.swapaxes(s["conv"], 0, 1) for s in states]),
        )

    p_out = stack(p_states, BP, _pairs_to_heads, lambda s: s[:, 0, :])
    s_out = stack(s_states, BS, lambda s: s, lambda s: s)
    return (xp, xs.reshape(BS, 1, D_MODEL)) + p_out + s_out
```

```python
import functools
import math

import jax
import jax.numpy as jnp
from jax import lax
from jax.experimental import pallas as pl
from jax.experimental.pallas import tpu as pltpu

F32 = jnp.float32
BF16 = jnp.bfloat16

D_MODEL = 1024
DEPTH = 2
N_META = 16
D_FF = 2816
NORM_EPS = 1e-6
FFN_RES_SCALE = 0.5
S5_WIDTH = 384
S5_GROUP = 16
S5_GROUPS = 24
S5_STATE = 64
S5_LANES = S5_GROUPS * S5_STATE
RWKV_HEAD = 64
RWKV_WIDTH = 384
RWKV_HEADS = 6
RWKV_PAIRS = 3
DECAY_RANK = 64
ICL_RANK = 64
GATE_RANK = 128
RWKV_PROJ = 1408
RWKV_LN_EPS = 64e-5
LRU_WIDTH = 256
LRU_BLOCKS = 4
CONV_WIDTH = 4
CONV_HIST = CONV_WIDTH - 1
LRU_C = 8.0
D_IN = 2304
SL_IN = S5_WIDTH + 2 * LRU_WIDTH
SL_OUT = S5_WIDTH + LRU_WIDTH
LANE = 128
SUBLANE = 8
S5_GROUPS_PER_BLOCK = LANE // S5_GROUP
S5_BLOCKS = S5_GROUPS // S5_GROUPS_PER_BLOCK
S5_BLOCK_STATES = S5_GROUPS_PER_BLOCK * S5_STATE

V7X_VMEM_BYTES = 64 * 2 ** 20
VMEM_LIMIT = (V7X_VMEM_BYTES * 7) // 8

PROMPT_TOKEN_TILE = 256
PROMPT_CHUNK = 64
STEP_SEQ_TILE = 16


def _dot(a, b):
    return jnp.dot(a, b, preferred_element_type=F32)


def _dot_nt(a, b):
    return lax.dot_general(a, b, (((1,), (1,)), ((), ())), preferred_element_type=F32)


def _dot_tn(a, b):
    return lax.dot_general(a, b, (((0,), (0,)), ((), ())), preferred_element_type=F32)


def _split2(x):
    hi = x.astype(BF16)
    lo = (x - hi.astype(F32)).astype(BF16)
    return hi, lo


def _split3(x):
    hi = x.astype(BF16)
    r = x - hi.astype(F32)
    mid = r.astype(BF16)
    lo = (r - mid.astype(F32)).astype(BF16)
    return hi, mid, lo


def _sigmoid(x):
    return 1.0 / (1.0 + jnp.exp(-x))


def _gelu(x):
    c = math.sqrt(2.0 / math.pi)
    return 0.5 * x * (1.0 + jnp.tanh(c * (x + 0.044715 * (x * x * x))))


def _neg_expm1(z):
    t = jnp.tanh(0.5 * z)
    return -2.0 * t / (1.0 - t)


def _softplus(x):
    return jnp.maximum(x, 0.0) + jnp.log1p(jnp.exp(-jnp.abs(x)))


def _rms(x, g):
    return x * lax.rsqrt(jnp.mean(x * x, axis=-1, keepdims=True) + NORM_EPS) * g


def _swiglu_res(x, g_norm, wg_ref, wu_ref, wd_ref):
    xn = _rms(x, g_norm).astype(BF16)
    g = _dot(xn, wg_ref[...])
    u = _dot(xn, wu_ref[...])
    a = (g * _sigmoid(g) * u).astype(BF16)
    return x + FFN_RES_SCALE * _dot(a, wd_ref[...])


def _full_spec(arr):
    nd = arr.ndim
    return pl.BlockSpec(arr.shape, lambda *_: (0,) * nd, pipeline_mode=pl.Buffered(1))


def _layer_spec(arr, l):
    nd = arr.ndim - 1
    return pl.BlockSpec((None,) + arr.shape[1:], lambda *_: (l,) + (0,) * nd, pipeline_mode=pl.Buffered(1))


def _row(ref, l):
    return ref[l:l + 1, :]


def _k1_kernel(x_ref, n1_ref, wg_ref, wu_ref, wd_ref, nm_ref, win_ref, h_ref, prw_ref, psl_ref, *, l):
    x = x_ref[...]
    h = _swiglu_res(x, _row(n1_ref, l), wg_ref, wu_ref, wd_ref)
    h_ref[...] = h
    hn = _rms(h, _row(nm_ref, l)).astype(BF16)
    proj = _dot(hn, win_ref[...])
    psl_ref[:, 0:S5_WIDTH] = proj[:, 0:S5_WIDTH]
    prw_ref[...] = proj[:, S5_WIDTH:S5_WIDTH + RWKV_PROJ]
    psl_ref[:, S5_WIDTH:SL_IN] = proj[:, S5_WIDTH + RWKV_PROJ:D_IN]


def _k1(x, w, l, tl):
    B, L, _ = x.shape
    return pl.pallas_call(
        functools.partial(_k1_kernel, l=l),
        grid=(B, L // tl),
        in_specs=[
            pl.BlockSpec((None, tl, D_MODEL), lambda b, i: (b, i, 0)),
            _full_spec(w["n1"]),
            _layer_spec(w["wg1"], l),
            _layer_spec(w["wu1"], l),
            _layer_spec(w["wd1"], l),
            _full_spec(w["nm"]),
            _layer_spec(w["win"], l),
        ],
        out_specs=[
            pl.BlockSpec((None, tl, D_MODEL), lambda b, i: (b, i, 0)),
            pl.BlockSpec((None, tl, RWKV_PROJ), lambda b, i: (b, i, 0)),
            pl.BlockSpec((tl, SL_IN), lambda b, i: (i, b)),
        ],
        out_shape=[
            jax.ShapeDtypeStruct((B, L, D_MODEL), F32),
            jax.ShapeDtypeStruct((B, L, RWKV_PROJ), F32),
            jax.ShapeDtypeStruct((L, B * SL_IN), F32),
        ],
        compiler_params=pltpu.CompilerParams(
            dimension_semantics=("parallel", "parallel"), vmem_limit_bytes=VMEM_LIMIT),
        name="ffn1_inproj",
    )(x, w["n1"], w["wg1"], w["wu1"], w["wd1"], w["nm"], w["win"])


def _k3_kernel(h_ref, osl_ref, orw_ref, wout_ref, n2_ref, wg_ref, wu_ref, wd_ref, fn_ref, o_ref, *, l, final):
    osl = osl_ref[...]
    mix = (_dot(osl[:, 0:S5_WIDTH].astype(BF16), wout_ref[0:S5_WIDTH, :])
           + _dot(orw_ref[...].astype(BF16), wout_ref[S5_WIDTH:S5_WIDTH + RWKV_WIDTH, :])
           + _dot(osl[:, S5_WIDTH:SL_OUT].astype(BF16), wout_ref[S5_WIDTH + RWKV_WIDTH:D_MODEL, :]))
    h2 = h_ref[...] + mix
    h3 = _swiglu_res(h2, _row(n2_ref, l), wg_ref, wu_ref, wd_ref)
    if final:
        h3 = _rms(h3, fn_ref[...])
    o_ref[...] = h3


def _k3(h, osl, orw, w, l, tl, final):
    B, L, _ = h.shape
    return pl.pallas_call(
        functools.partial(_k3_kernel, l=l, final=final),
        grid=(B, L // tl),
        in_specs=[
            pl.BlockSpec((None, tl, D_MODEL), lambda b, i: (b, i, 0)),
            pl.BlockSpec((tl, SL_OUT), lambda b, i: (i, b)),
            pl.BlockSpec((None, tl, RWKV_WIDTH), lambda b, i: (b, i, 0)),
            _layer_spec(w["wout"], l),
            _full_spec(w["n2"]),
            _layer_spec(w["wg2"], l),
            _layer_spec(w["wu2"], l),
            _layer_spec(w["wd2"], l),
            _full_spec(w["fnorm"]),
        ],
        out_specs=pl.BlockSpec((None, tl, D_MODEL), lambda b, i: (b, i, 0)),
        out_shape=jax.ShapeDtypeStruct((B, L, D_MODEL), F32),
        compiler_params=pltpu.CompilerParams(
            dimension_semantics=("parallel", "parallel"), vmem_limit_bytes=VMEM_LIMIT),
        name="outproj_ffn2",
    )(h, osl, orw, w["wout"], w["n2"], w["wg2"], w["wu2"], w["wd2"], w["fnorm"])


def _s5_prep_kernel(lr_ref, li_ref, ldt_ref, br_ref, bi_ref, abr_ref, abi_ref, bbr_ref, bbi_ref):
    lr = lr_ref[...]
    li = li_ref[...]
    dt = jnp.exp(ldt_ref[...])
    mag = jnp.exp(lr * dt)
    ab_re = mag * jnp.cos(li * dt)
    ab_im = mag * jnp.sin(li * dt)
    den = lr * lr + li * li
    f_re = ((ab_re - 1.0) * lr + ab_im * li) / den
    f_im = (ab_im * lr - (ab_re - 1.0) * li) / den
    abr_ref[...] = ab_re
    abi_ref[...] = ab_im
    br = br_ref[...]
    bi = bi_ref[...]
    bbr_ref[...] = f_re[:, :, None, :] * br - f_im[:, :, None, :] * bi
    bbi_ref[...] = f_re[:, :, None, :] * bi + f_im[:, :, None, :] * br


def _s5_prep(lam_re, lam_im, log_dt, b_re, b_im):
    G, N, C = S5_GROUPS, S5_STATE, S5_GROUP
    ldt = jnp.broadcast_to(log_dt[:, :, None], (DEPTH, G, N))
    br = jnp.swapaxes(b_re, 2, 3)
    bi = jnp.swapaxes(b_im, 2, 3)
    return pl.pallas_call(
        _s5_prep_kernel,
        out_shape=[
            jax.ShapeDtypeStruct((DEPTH, G, N), F32),
            jax.ShapeDtypeStruct((DEPTH, G, N), F32),
            jax.ShapeDtypeStruct((DEPTH, G, C, N), F32),
            jax.ShapeDtypeStruct((DEPTH, G, C, N), F32),
        ],
        name="s5_discretise",
    )(lam_re, lam_im, ldt, br, bi)


def _s5lru_kernel(psl_ref, ab_ref, win_ref, cre_ref, cim_ref, d_ref, gw_ref, gb_ref,
                  cw_ref, cb_ref, wa_ref, wx_ref, ba_ref, bx_ref, lam_ref,
                  h0_ref, l0_ref, c0_ref,
                  o_ref, hT_ref, lT_ref, cT_ref,
                  hs_scr, ls_scr, xc_scr, xr_scr, xi_scr, la_scr, lb_scr, *, l, B, Tc, unroll):
    c = pl.program_id(0)
    R = Tc * B

    @pl.when(c == 0)
    def _():
        hs_scr[...] = h0_ref[...]
        ls_scr[...] = l0_ref[...]
        xc_scr[0:CONV_HIST] = c0_ref[...]

    psl = psl_ref[...]
    u = psl[:, :, 0:S5_WIDTH].reshape(R, S5_WIDTH)
    gate_in = psl[:, :, S5_WIDTH + LRU_WIDTH:SL_IN].reshape(R, LRU_WIDTH)

    ub = u.astype(BF16)
    half = S5_BLOCK_STATES
    for j in range(S5_BLOCKS):
        x = _dot(ub[:, LANE * j:LANE * (j + 1)], win_ref[j])
        xr_scr[:, :, half * j:half * (j + 1)] = x[:, 0:half].reshape(Tc, B, half)
        xi_scr[:, :, half * j:half * (j + 1)] = x[:, half:2 * half].reshape(Tc, B, half)

    xc_scr[CONV_HIST:CONV_HIST + Tc] = psl[:, :, S5_WIDTH:S5_WIDTH + LRU_WIDTH]
    cw = cw_ref[...]
    xc = _row(cb_ref, l) + xc_scr[0:Tc] * cw[0:1]
    for j in range(1, CONV_WIDTH):
        xc = xc + xc_scr[j:j + Tc] * cw[j:j + 1]
    new_hist = xc_scr[Tc:Tc + CONV_HIST]
    xc_scr[0:CONV_HIST] = new_hist
    xc2 = xc.reshape(R, LRU_WIDTH)
    xcb = xc2.astype(BF16)
    gate_a = _sigmoid(_dot(xcb, wa_ref[...]) + _row(ba_ref, l))
    gate_x = _sigmoid(_dot(xcb, wx_ref[...]) + _row(bx_ref, l))
    log_a = LRU_C * gate_a * (-_softplus(-_row(lam_ref, l)))
    la_scr[...] = jnp.exp(log_a).reshape(Tc, B, LRU_WIDTH)
    lb_scr[...] = (jnp.sqrt(_neg_expm1(2.0 * log_a)) * (gate_x * xc2)).reshape(Tc, B, LRU_WIDTH)

    ar = jnp.broadcast_to(ab_ref[0:1, :], (B, S5_LANES))
    ai = jnp.broadcast_to(ab_ref[1:2, :], (B, S5_LANES))

    def step(t, carry):
        hr, hi, hl = carry
        nr = ar * hr - ai * hi + xr_scr[t]
        ni = ar * hi + ai * hr + xi_scr[t]
        xr_scr[t] = nr
        xi_scr[t] = ni
        nl = la_scr[t] * hl + lb_scr[t]
        lb_scr[t] = nl
        return nr, ni, nl

    hr, hi, hl = lax.fori_loop(0, Tc, step, (hs_scr[0], hs_scr[1], ls_scr[...]), unroll=unroll)
    hs_scr[0] = hr
    hs_scr[1] = hi
    ls_scr[...] = hl

    hrb = xr_scr[...].reshape(R, S5_LANES).astype(BF16)
    hib = xi_scr[...].reshape(R, S5_LANES).astype(BF16)
    ys = []
    for j in range(S5_BLOCKS):
        sl = slice(half * j, half * (j + 1))
        ys.append(_dot(hrb[:, sl], cre_ref[j]) - _dot(hib[:, sl], cim_ref[j]))
    y = jnp.concatenate(ys, axis=-1) + _row(d_ref, l) * u
    z = _gelu(y)
    o_s5 = z * _sigmoid(_dot(z.astype(BF16), gw_ref[...].astype(BF16)) + _row(gb_ref, l))
    o_lru = lb_scr[...].reshape(R, LRU_WIDTH) * _gelu(gate_in)
    o_ref[:, :, 0:S5_WIDTH] = o_s5.reshape(Tc, B, S5_WIDTH)
    o_ref[:, :, S5_WIDTH:SL_OUT] = o_lru.reshape(Tc, B, LRU_WIDTH)

    @pl.when(c == pl.num_programs(0) - 1)
    def _():
        hT_ref[...] = hs_scr[...]
        lT_ref[...] = ls_scr[...]
        cT_ref[...] = xc_scr[0:CONV_HIST]


def _s5lru(psl3, w, l, st, Tc):
    L, B, _ = psl3.shape
    li = st["li"]
    unroll = True if Tc <= 2 else 2
    return pl.pallas_call(
        functools.partial(_s5lru_kernel, l=l, B=B, Tc=Tc, unroll=unroll),
        grid=(L // Tc,),
        in_specs=[
            pl.BlockSpec((Tc, B, SL_IN), lambda c: (c, 0, 0)),
            _layer_spec(w["s5_ab"], l),
            _layer_spec(w["s5_win"], l),
            _layer_spec(w["s5_cre"], l),
            _layer_spec(w["s5_cim"], l),
            _full_spec(w["s5_d"]),
            _layer_spec(w["s5_glu_w"], l),
            _full_spec(w["s5_glu_b"]),
            _layer_spec(w["lru_conv_w"], l),
            _full_spec(w["lru_conv_b"]),
            _layer_spec(w["lru_wa"], l),
            _layer_spec(w["lru_wx"], l),
            _full_spec(w["lru_b_a"]),
            _full_spec(w["lru_b_x"]),
            _full_spec(w["lru_lambda"]),
            _layer_spec(st["s5"], li),
            _layer_spec(st["lru"], li),
            _layer_spec(st["conv"], li),
        ],
        out_specs=[
            pl.BlockSpec((Tc, B, SL_OUT), lambda c: (c, 0, 0)),
            pl.BlockSpec((2, B, S5_LANES), lambda c: (0, 0, 0)),
            pl.BlockSpec((B, LRU_WIDTH), lambda c: (0, 0)),
            pl.BlockSpec((CONV_HIST, B, LRU_WIDTH), lambda c: (0, 0, 0)),
        ],
        out_shape=[
            jax.ShapeDtypeStruct((L, B, SL_OUT), F32),
            jax.ShapeDtypeStruct((2, B, S5_LANES), F32),
            jax.ShapeDtypeStruct((B, LRU_WIDTH), F32),
            jax.ShapeDtypeStruct((CONV_HIST, B, LRU_WIDTH), F32),
        ],
        scratch_shapes=[
            pltpu.VMEM((2, B, S5_LANES), F32),
            pltpu.VMEM((B, LRU_WIDTH), F32),
            pltpu.VMEM((Tc + CONV_HIST, B, LRU_WIDTH), F32),
            pltpu.VMEM((Tc, B, S5_LANES), F32),
            pltpu.VMEM((Tc, B, S5_LANES), F32),
            pltpu.VMEM((Tc, B, LRU_WIDTH), F32),
            pltpu.VMEM((Tc, B, LRU_WIDTH), F32),
        ],
        compiler_params=pltpu.CompilerParams(
            dimension_semantics=("arbitrary",), vmem_limit_bytes=VMEM_LIMIT),
        name="s5_rglru_scan",
    )(psl3, w["s5_ab"], w["s5_win"], w["s5_cre"], w["s5_cim"], w["s5_d"], w["s5_glu_w"], w["s5_glu_b"],
      w["lru_conv_w"], w["lru_conv_b"], w["lru_wa"], w["lru_wx"], w["lru_b_a"], w["lru_b_x"], w["lru_lambda"],
      st["s5"], st["lru"], st["conv"])


_RWKV_PARAMS = ("rwkv_mu", "rwkv_w0", "rwkv_w_up", "rwkv_a0", "rwkv_a_up", "rwkv_g_up", "rwkv_k_k", "rwkv_k_a",
                "rwkv_r_k", "rwkv_ln_w", "rwkv_ln_b", "rwkv_ones")


def _rwkv_param_specs(w, l):
    layered = ("rwkv_w_up", "rwkv_a_up", "rwkv_g_up")
    return [_layer_spec(w[k], l) if k in layered else _full_spec(w[k]) for k in _RWKV_PARAMS]


def _segsum(x, ones):
    outs = []
    for p in range(RWKV_PAIRS):
        hi, lo = _split2(x[:, LANE * p:LANE * (p + 1)])
        outs.append(_dot(hi, ones) + _dot(lo, ones))
    return jnp.concatenate(outs, axis=-1)


def _rwkv_token_math(p, prev, prm, l):
    (mu_ref, w0_ref, wup_ref, a0_ref, aup_ref, gup_ref, kk_ref, ka_ref, rk_ref, _, _, ones_ref) = prm
    W = RWKV_WIDTH
    ones = ones_ref[...]
    xm = p + (prev - p) * _row(mu_ref, l)
    r = xm[:, 0:W]
    k = xm[:, W:2 * W]
    v = xm[:, 2 * W:3 * W]
    xwa = xm[:, 3 * W:3 * W + DECAY_RANK + ICL_RANK]
    xg = xm[:, 3 * W + DECAY_RANK + ICL_RANK:RWKV_PROJ]
    zpad = jnp.zeros((DECAY_RANK, W), BF16)
    wup = jnp.concatenate([wup_ref[...].astype(BF16), zpad], axis=0)
    aup = jnp.concatenate([zpad, aup_ref[...].astype(BF16)], axis=0)
    lw = _row(w0_ref, l) + _dot(jnp.tanh(xwa).astype(BF16), wup)
    logw = -_softplus(-lw) - 0.5
    ld = -jnp.exp(logw)
    a = _sigmoid(_row(a0_ref, l) + _dot(xwa.astype(BF16), aup))
    g = _dot(_sigmoid(xg).astype(BF16), gup_ref[...].astype(BF16))
    kkr = k * _row(kk_ref, l)
    kk = kkr / jnp.maximum(jnp.sqrt(_segsum(kkr * kkr, ones)), 1e-12)
    k2 = k * (1.0 + (a - 1.0) * _row(ka_ref, l))
    bonus = _segsum(r * k2 * _row(rk_ref, l), ones) * v
    return r, k2, v, kk, a, ld, g, bonus


def _rwkv_out(y, bonus, g, prm, l):
    lnw_ref, lnb_ref, ones_ref = prm[9], prm[10], prm[11]
    ones = ones_ref[...]
    inv = 1.0 / RWKV_HEAD
    mean = _segsum(y, ones) * inv
    d = y - mean
    var = _segsum(d * d, ones) * inv
    yn = d * lax.rsqrt(var + RWKV_LN_EPS) * _row(lnw_ref, l) + _row(lnb_ref, l)
    return (yn + bonus) * g


def _rwkv_chunk_kernel(*refs, l, B, Tc):
    p_ref, s0_ref, sh0_ref = refs[0:3]
    prm = refs[3:3 + len(_RWKV_PARAMS)]
    tri_ref, o_ref, sT_ref = refs[3 + len(_RWKV_PARAMS):6 + len(_RWKV_PARAMS)]
    (st_scr, xs_scr, a_scr, b_scr, k_scr, r_scr, v_scr, bp_scr, kp_scr, pt_scr, y_scr) = refs[6 + len(_RWKV_PARAMS):]
    c = pl.program_id(0)
    R = B * Tc
    W = RWKV_WIDTH
    carry_row = SUBLANE - 1

    @pl.when(c == 0)
    def _():
        st_scr[...] = s0_ref[...]
        xs_scr[:, carry_row:SUBLANE, :] = sh0_ref[...]

    p = p_ref[...]
    xs_scr[:, SUBLANE:SUBLANE + Tc, :] = p
    prev = xs_scr[:, carry_row:carry_row + Tc, :]
    xs_scr[:, carry_row:SUBLANE, :] = p[:, Tc - 1:Tc, :]
    r, k2, v, kk, a, ld, g, bonus = _rwkv_token_math(
        p.reshape(R, RWKV_PROJ), prev.reshape(R, RWKV_PROJ), prm, l)

    h1, h2, h3 = _split3(ld)
    tri = tri_ref[...]
    cls = []
    for b in range(B):
        rows = slice(b * Tc, (b + 1) * Tc)
        cls.append(_dot(tri, h1[rows]) + _dot(tri, h2[rows]) + _dot(tri, h3[rows]))
    cl = jnp.concatenate(cls, axis=0)
    pt = jnp.exp(cl.reshape(B, Tc, W)[:, Tc - 1:Tc, :])
    inv_p = jnp.exp(-cl)
    to_end = (pt * inv_p.reshape(B, Tc, W)).reshape(R, W)
    beta = kk * a
    a_scr[...] = (-kk * jnp.exp(cl - ld)).astype(BF16)
    b_scr[...] = (beta * inv_p).astype(BF16)
    k_scr[...] = (k2 * inv_p).astype(BF16)
    r_scr[...] = (r * jnp.exp(cl)).astype(BF16)
    v_scr[...] = v.astype(BF16)
    bp_scr[...] = (beta * to_end).astype(BF16)
    kp_scr[...] = (k2 * to_end).astype(BF16)
    pt_scr[...] = pt

    lane = lax.broadcasted_iota(jnp.int32, (1, LANE), 1)
    m0 = (lane < RWKV_HEAD).astype(BF16)
    m1 = (lane >= RWKV_HEAD).astype(BF16)
    assert Tc & (Tc - 1) == 0
    ti = lax.broadcasted_iota(jnp.int32, (2 * Tc, 2 * Tc), 0) & (Tc - 1)
    si = lax.broadcasted_iota(jnp.int32, (2 * Tc, 2 * Tc), 1) & (Tc - 1)
    strict = (si < ti).astype(F32)
    incl = (si <= ti).astype(F32)
    n_doubling = max(1, (Tc - 1).bit_length())

    def stack2(x):
        return jnp.concatenate([x * m0, x * m1], axis=0)

    probs = [(b, pr) for b in range(B) for pr in range(RWKV_PAIRS)]

    def operand(scr, b, pr):
        return stack2(scr[b * Tc:(b + 1) * Tc, LANE * pr:LANE * (pr + 1)])

    v2s, gss, ns, wrbs, wrks, xs = [], [], [], [], [], []
    for b, pr in probs:
        lhs = jnp.concatenate([operand(a_scr, b, pr), operand(r_scr, b, pr)], axis=0)
        v2 = operand(v_scr, b, pr)
        gb = _dot_nt(lhs, operand(b_scr, b, pr))
        gk = _dot_nt(lhs, operand(k_scr, b, pr))
        gs = _dot_nt(lhs, st_scr[b, pr].astype(BF16))
        m = (gk[0:2 * Tc] * strict).astype(BF16)
        v2s.append(v2)
        gss.append(gs[2 * Tc:4 * Tc])
        ns.append((gb[0:2 * Tc] * strict).astype(BF16))
        wrbs.append((gb[2 * Tc:4 * Tc] * incl).astype(BF16))
        wrks.append((gk[2 * Tc:4 * Tc] * incl).astype(BF16))
        xs.append(gs[0:2 * Tc] + _dot(m, v2))
    for j in range(n_doubling):
        for i in range(len(probs)):
            xs[i] = xs[i] + _dot(ns[i], xs[i].astype(BF16))
        if j + 1 < n_doubling:
            for i in range(len(probs)):
                ns[i] = _dot(ns[i], ns[i]).astype(BF16)
    for i, (b, pr) in enumerate(probs):
        sl = slice(LANE * pr, LANE * (pr + 1))
        sab = xs[i].astype(BF16)
        y2 = gss[i] + _dot(wrbs[i], sab) + _dot(wrks[i], v2s[i])
        y_scr[b * Tc:(b + 1) * Tc, sl] = y2[0:Tc] + y2[Tc:2 * Tc]
        upd = _dot_tn(jnp.concatenate([sab, v2s[i]], axis=0),
                      jnp.concatenate([operand(bp_scr, b, pr), operand(kp_scr, b, pr)], axis=0))
        st_scr[b, pr] = st_scr[b, pr] * pt_scr[b, :, sl] + upd

    out = _rwkv_out(y_scr[...], bonus, g, prm, l)
    o_ref[...] = out.reshape(B, Tc, W)

    @pl.when(c == pl.num_programs(0) - 1)
    def _():
        sT_ref[...] = st_scr[...]


def _rwkv_chunk(prw, w, l, s0, sh0, Tc):
    B, L, _ = prw.shape
    R = B * Tc
    W = RWKV_WIDTH
    tri = jnp.tril(jnp.ones((Tc, Tc), F32)).astype(BF16)
    return pl.pallas_call(
        functools.partial(_rwkv_chunk_kernel, l=l, B=B, Tc=Tc),
        grid=(L // Tc,),
        in_specs=[
            pl.BlockSpec((B, Tc, RWKV_PROJ), lambda c: (0, c, 0)),
            _full_spec(s0),
            _full_spec(sh0),
        ] + _rwkv_param_specs(w, l) + [_full_spec(tri)],
        out_specs=[
            pl.BlockSpec((B, Tc, W), lambda c: (0, c, 0)),
            pl.BlockSpec((B, RWKV_PAIRS, LANE, LANE), lambda c: (0, 0, 0, 0)),
        ],
        out_shape=[
            jax.ShapeDtypeStruct((B, L, W), F32),
            jax.ShapeDtypeStruct((B, RWKV_PAIRS, LANE, LANE), F32),
        ],
        scratch_shapes=[
            pltpu.VMEM((B, RWKV_PAIRS, LANE, LANE), F32),
            pltpu.VMEM((B, Tc + SUBLANE, RWKV_PROJ), F32),
        ] + [pltpu.VMEM((R, W), BF16)] * 7 + [
            pltpu.VMEM((B, 1, W), F32),
            pltpu.VMEM((R, W), F32),
        ],
        compiler_params=pltpu.CompilerParams(
            dimension_semantics=("arbitrary",), vmem_limit_bytes=VMEM_LIMIT),
        name="rwkv7_chunked",
    )(prw, s0, sh0, *[w[k] for k in _RWKV_PARAMS], tri)


def _rwkv_step_kernel(*refs, l):
    p_ref, sh_ref, s_ref = refs[0:3]
    prm = refs[3:3 + len(_RWKV_PARAMS)]
    o_ref, sn_ref = refs[3 + len(_RWKV_PARAMS):]
    r, k2, v, kk, a, ld, g, bonus = _rwkv_token_math(p_ref[...], sh_ref[...], prm, l)
    w = jnp.exp(ld)
    beta = kk * a
    N = RWKV_HEAD
    eye = (lax.broadcasted_iota(jnp.int32, (N, N), 0) == lax.broadcasted_iota(jnp.int32, (N, N), 1)).astype(F32)
    ys = []
    for h in range(RWKV_HEADS):
        sl = slice(N * h, N * (h + 1))
        s = s_ref[:, h]
        sa = jnp.sum(s * (-kk[:, sl])[:, None, :], axis=-1, keepdims=True)
        vcol = jnp.sum(eye[None] * v[:, sl][:, None, :], axis=-1, keepdims=True)
        sn = s * w[:, sl][:, None, :] + sa * beta[:, sl][:, None, :] + vcol * k2[:, sl][:, None, :]
        sn_ref[:, h] = sn
        ycol = jnp.sum(sn * r[:, sl][:, None, :], axis=-1, keepdims=True)
        ys.append(jnp.sum(eye[None] * ycol, axis=1))
    y = jnp.concatenate(ys, axis=-1)
    o_ref[...] = _rwkv_out(y, bonus, g, prm, l)


def _rwkv_step(prw, w, l, s_all, sh_all, bb):
    B = prw.shape[0]
    W = RWKV_WIDTH
    N = RWKV_HEAD
    return pl.pallas_call(
        functools.partial(_rwkv_step_kernel, l=l),
        grid=(B // bb,),
        in_specs=[
            pl.BlockSpec((bb, RWKV_PROJ), lambda i: (i, 0)),
            pl.BlockSpec((None, bb, RWKV_PROJ), lambda i: (l, i, 0)),
            pl.BlockSpec((None, bb, RWKV_HEADS, N, N), lambda i: (l, i, 0, 0, 0)),
        ] + _rwkv_param_specs(w, l),
        out_specs=[
            pl.BlockSpec((bb, W), lambda i: (i, 0)),
            pl.BlockSpec((bb, RWKV_HEADS, N, N), lambda i: (i, 0, 0, 0)),
        ],
        out_shape=[
            jax.ShapeDtypeStruct((B, W), F32),
            jax.ShapeDtypeStruct((B, RWKV_HEADS, N, N), F32),
        ],
        compiler_params=pltpu.CompilerParams(
            dimension_semantics=("parallel",), vmem_limit_bytes=VMEM_LIMIT),
        name="rwkv7_step",
    )(prw, sh_all, s_all, *[w[k] for k in _RWKV_PARAMS])


def _block_diag(blocks):
    *lead, n, r, c = blocks.shape
    eye = jnp.eye(n, dtype=blocks.dtype)
    out = blocks[..., :, :, None, :] * eye[:, None, :, None]
    return out.reshape(*lead, n * r, n * c)


def _prepare_weights(a):
    G, C, N = S5_GROUPS, S5_GROUP, S5_STATE
    abr, abi, bbr, bbi = _s5_prep(a["s5_lambda_re"], a["s5_lambda_im"], a["s5_log_dt"], a["s5_b_re"], a["s5_b_im"])

    def in_side(bb):
        return _block_diag(bb.reshape(DEPTH, S5_BLOCKS, S5_GROUPS_PER_BLOCK, C, N))

    def out_side(cc):
        ct = jnp.swapaxes(cc, 2, 3)
        return _block_diag(ct.reshape(DEPTH, S5_BLOCKS, S5_GROUPS_PER_BLOCK, N, C))

    w = {k: a[k] for k in (
        "s5_d", "s5_glu_w", "s5_glu_b", "lru_conv_w", "lru_conv_b", "lru_b_a", "lru_b_x", "lru_lambda",
        "rwkv_mu", "rwkv_w0", "rwkv_w_up", "rwkv_a0", "rwkv_a_up", "rwkv_g_up", "rwkv_k_k", "rwkv_k_a",
        "rwkv_ln_w", "rwkv_ln_b")}
    w.update({
        "n1": a["ffn1_norm"], "nm": a["mix_norm"], "n2": a["ffn2_norm"],
        "fnorm": a["final_norm"].reshape(1, D_MODEL),
        "wg1": a["ffn1_w_gate"].astype(BF16), "wu1": a["ffn1_w_up"].astype(BF16),
        "wd1": a["ffn1_w_down"].astype(BF16), "win": a["w_in"].astype(BF16), "wout": a["w_out"].astype(BF16),
        "wg2": a["ffn2_w_gate"].astype(BF16), "wu2": a["ffn2_w_up"].astype(BF16),
        "wd2": a["ffn2_w_down"].astype(BF16),
        "s5_ab": jnp.stack([abr.reshape(DEPTH, S5_LANES), abi.reshape(DEPTH, S5_LANES)], axis=1),
        "s5_win": jnp.concatenate([in_side(bbr), in_side(bbi)], axis=-1).astype(BF16),
        "s5_cre": out_side(a["s5_c_re"]).astype(BF16),
        "s5_cim": out_side(a["s5_c_im"]).astype(BF16),
        "lru_wa": _block_diag(a["lru_w_a"]).astype(BF16),
        "lru_wx": _block_diag(a["lru_w_x"]).astype(BF16),
        "rwkv_r_k": a["rwkv_r_k"].reshape(DEPTH, RWKV_WIDTH),
        "rwkv_ones": _block_diag(jnp.ones((2, RWKV_HEAD, RWKV_HEAD), F32)).astype(BF16),
    })
    return w


def _pairs_to_heads(sp):
    N = RWKV_HEAD
    blocks = [sp[:, :, N * i:N * (i + 1), N * i:N * (i + 1)] for i in range(2)]
    return jnp.stack(blocks, axis=2).reshape(sp.shape[0], RWKV_HEADS, N, N)


def _seq_mixers(prw, psl3, st, w, l, Tc):
    L = prw.shape[1]
    osl, hT, lT, cT = _s5lru(psl3, w, l, st, Tc)
    orw, sT = _rwkv_chunk(prw, w, l, st["rwkv"], st["shift"], Tc)
    new = {"s5": hT[None], "lru": lT[None], "conv": cT[None], "li": 0,
           "rwkv": sT, "shift": prw[:, L - 1:L, :]}
    return osl, orw, new


def kernel(x_prompt, x_sample, state_s5_re, state_s5_im, state_rwkv, state_rwkv_shift, state_lru, state_lru_conv, meta_tokens, ffn1_norm, ffn1_w_gate, ffn1_w_up, ffn1_w_down, mix_norm, w_in, s5_lambda_re, s5_lambda_im, s5_log_dt, s5_b_re, s5_b_im, s5_c_re, s5_c_im, s5_d, s5_glu_w, s5_glu_b, rwkv_mu, rwkv_w0, rwkv_w_up, rwkv_a0, rwkv_a_up, rwkv_g_up, rwkv_k_k, rwkv_k_a, rwkv_r_k, rwkv_ln_w, rwkv_ln_b, lru_conv_w, lru_conv_b, lru_w_a, lru_b_a, lru_w_x, lru_b_x, lru_lambda, w_out, ffn2_norm, ffn2_w_gate, ffn2_w_up, ffn2_w_down, final_norm):
    w = _prepare_weights(dict(locals()))
    BP, SEQ, _ = x_prompt.shape
    BS = x_sample.shape[0]
    NS = N_META + BS

    zero = {
        "s5": jnp.zeros((1, 2, BP, S5_LANES), F32),
        "lru": jnp.zeros((1, BP, LRU_WIDTH), F32),
        "conv": jnp.zeros((1, CONV_HIST, BP, LRU_WIDTH), F32),
        "li": 0,
        "rwkv": jnp.zeros((BP, RWKV_PAIRS, LANE, LANE), F32),
        "shift": jnp.zeros((BP, 1, RWKV_PROJ), F32),
    }
    sample_in = {
        "s5": jnp.stack([state_s5_re.reshape(DEPTH, BS, S5_LANES), state_s5_im.reshape(DEPTH, BS, S5_LANES)], axis=1),
        "lru": state_lru,
        "conv": jnp.swapaxes(state_lru_conv, 1, 2),
    }

    xp = x_prompt
    xs = jnp.concatenate([meta_tokens.astype(F32), x_sample.reshape(BS, D_MODEL)], axis=0)[None]
    p_states = []
    s_states = []
    for l in range(DEPTH):
        final = l == DEPTH - 1
        hs, prw_s, psl_s = _k1(xs, w, l, NS)
        prw_m = jnp.broadcast_to(prw_s[:, :N_META], (BP, N_META, RWKV_PROJ))
        psl_m = jnp.broadcast_to(psl_s[:N_META, None, :], (N_META, BP, SL_IN))
        osl_m, orw_m, st_meta = _seq_mixers(prw_m, psl_m, zero, w, l, N_META)

        st_l = dict(sample_in, li=l)
        osl_s, hT, lT, cT = _s5lru(psl_s[N_META:].reshape(1, BS, SL_IN), w, l, st_l, 1)
        orw_s, sT = _rwkv_step(prw_s[0, N_META:], w, l, state_rwkv, state_rwkv_shift, STEP_SEQ_TILE)
        s_states.append({"s5": hT, "lru": lT, "conv": cT, "rwkv": sT, "shift": prw_s[0, N_META:]})

        hp, prw_p, psl_p = _k1(xp, w, l, PROMPT_TOKEN_TILE)
        osl_p, orw_p, st_p = _seq_mixers(prw_p, psl_p.reshape(SEQ, BP, SL_IN), st_meta, w, l, PROMPT_CHUNK)
        p_states.append(st_p)
        xp = _k3(hp, osl_p.reshape(SEQ, BP * SL_OUT), orw_p, w, l, PROMPT_TOKEN_TILE, final)

        if final:
            xs = _k3(hs[:, N_META:], osl_s.reshape(BS, SL_OUT), orw_s[None], w, l, BS, True)
        else:
            osl = jnp.concatenate([osl_m[:, 0, :], osl_s.reshape(BS, SL_OUT)], axis=0)
            orw = jnp.concatenate([orw_m[0], orw_s], axis=0)[None]
            xs = _k3(hs, osl, orw, w, l, NS, False)

    p_out = (
        jnp.stack([s["s5"][0, 0].reshape(BP, S5_GROUPS, S5_STATE) for s in p_states]),
        jnp.stack([s["s5"][0, 1].reshape(BP, S5_GROUPS, S5_STATE) for s in p_states]),
        jnp.stack([_pairs_to_heads(s["rwkv"]) for s in p_states]),
        jnp.stack([s["shift"][:, 0, :] for s in p_states]),
        jnp.stack([s["lru"][0] for s in p_states]),
        jnp.stack([jnp.swapaxes(s["conv"][0], 0, 1) for s in p_states]),
    )
    s_out = (
        jnp.stack([s["s5"][0].reshape(BS, S5_GROUPS, S5_STATE) for s in s_states]),
        jnp.stack([s["s5"][1].reshape(BS, S5_GROUPS, S5_STATE) for s in s_states]),
        jnp.stack([s["rwkv"] for s in s_states]),
        jnp.stack([s["shift"] for s in s_states]),
        jnp.stack([s["lru"] for s in s_states]),
        jnp.stack([jnp.swapaxes(s["conv"], 0, 1) for s in s_states]),
    )
    return (xp, xs.reshape(BS, 1, D_MODEL)) + p_out + s_out
```

```python
import functools
import math

import jax
import jax.numpy as jnp
from jax import lax
from jax.experimental import pallas as pl
from jax.experimental.pallas import tpu as pltpu

F32 = jnp.float32
BF16 = jnp.bfloat16

D_MODEL = 1024
DEPTH = 2
N_META = 16
D_FF = 2816
NORM_EPS = 1e-6
FFN_RES_SCALE = 0.5
S5_WIDTH = 384
S5_GROUP = 16
S5_GROUPS = 24
S5_STATE = 64
S5_LANES = S5_GROUPS * S5_STATE
RWKV_HEAD = 64
RWKV_WIDTH = 384
RWKV_HEADS = 6
RWKV_PAIRS = 3
DECAY_RANK = 64
ICL_RANK = 64
GATE_RANK = 128
RWKV_PROJ = 1408
RWKV_LN_EPS = 64e-5
LRU_WIDTH = 256
LRU_BLOCKS = 4
CONV_WIDTH = 4
CONV_HIST = CONV_WIDTH - 1
LRU_C = 8.0
D_IN = 2304
SL_IN = S5_WIDTH + 2 * LRU_WIDTH
SL_OUT = S5_WIDTH + LRU_WIDTH
LANE = 128
SUBLANE = 8
S5_GROUPS_PER_BLOCK = LANE // S5_GROUP
S5_BLOCKS = S5_GROUPS // S5_GROUPS_PER_BLOCK
S5_BLOCK_STATES = S5_GROUPS_PER_BLOCK * S5_STATE

V7X_VMEM_BYTES = 64 * 2 ** 20
VMEM_LIMIT = (V7X_VMEM_BYTES * 7) // 8

PROMPT_TOKEN_TILE = 256
PROMPT_CHUNK = 64


def _dot(a, b):
    return jnp.dot(a, b, preferred_element_type=F32)


def _dot_nt(a, b):
    return lax.dot_general(a, b, (((1,), (1,)), ((), ())), preferred_element_type=F32)


def _dot_tn(a, b):
    return lax.dot_general(a, b, (((0,), (0,)), ((), ())), preferred_element_type=F32)


def _split3(x):
    hi = x.astype(BF16)
    r = x - hi.astype(F32)
    mid = r.astype(BF16)
    lo = (r - mid.astype(F32)).astype(BF16)
    return hi, mid, lo


def _sigmoid(x):
    return 0.5 * (jnp.tanh(0.5 * x) + 1.0)


def _gelu(x):
    c = math.sqrt(2.0 / math.pi)
    return 0.5 * x * (1.0 + jnp.tanh(c * (x + 0.044715 * (x * x * x))))


def _neg_expm1(z):
    t = jnp.tanh(0.5 * z)
    return -2.0 * t / (1.0 - t)


def _softplus(x):
    return jnp.maximum(x, 0.0) + jnp.log1p(jnp.exp(-jnp.abs(x)))


def _rms(x, g):
    return x * lax.rsqrt(jnp.mean(x * x, axis=-1, keepdims=True) + NORM_EPS) * g


def _swiglu_res(x, g_norm, wg_ref, wu_ref, wd_ref):
    xn = _rms(x, g_norm).astype(BF16)
    g = _dot(xn, wg_ref[...])
    u = _dot(xn, wu_ref[...])
    a = (g * _sigmoid(g) * u).astype(BF16)
    return x + FFN_RES_SCALE * _dot(a, wd_ref[...])


def _full_spec(arr):
    nd = arr.ndim
    return pl.BlockSpec(arr.shape, lambda *_: (0,) * nd, pipeline_mode=pl.Buffered(1))


def _layer_spec(arr, l):
    nd = arr.ndim - 1
    return pl.BlockSpec((None,) + arr.shape[1:], lambda *_: (l,) + (0,) * nd, pipeline_mode=pl.Buffered(1))


def _row(ref, l):
    return ref[l:l + 1, :]


def _k1_kernel(x_ref, n1_ref, wg_ref, wu_ref, wd_ref, nm_ref, win_ref, h_ref, prw_ref, psl_ref, *, l):
    x = x_ref[...]
    h = _swiglu_res(x, _row(n1_ref, l), wg_ref, wu_ref, wd_ref)
    h_ref[...] = h
    hn = _rms(h, _row(nm_ref, l)).astype(BF16)
    proj = _dot(hn, win_ref[...])
    psl_ref[:, 0:S5_WIDTH] = proj[:, 0:S5_WIDTH]
    prw_ref[...] = proj[:, S5_WIDTH:S5_WIDTH + RWKV_PROJ]
    psl_ref[:, S5_WIDTH:SL_IN] = proj[:, S5_WIDTH + RWKV_PROJ:D_IN]


def _k1(x, w, l, tl):
    B, L, _ = x.shape
    return pl.pallas_call(
        functools.partial(_k1_kernel, l=l),
        grid=(B, L // tl),
        in_specs=[
            pl.BlockSpec((None, tl, D_MODEL), lambda b, i: (b, i, 0)),
            _full_spec(w["n1"]),
            _layer_spec(w["wg1"], l),
            _layer_spec(w["wu1"], l),
            _layer_spec(w["wd1"], l),
            _full_spec(w["nm"]),
            _layer_spec(w["win"], l),
        ],
        out_specs=[
            pl.BlockSpec((None, tl, D_MODEL), lambda b, i: (b, i, 0)),
            pl.BlockSpec((None, tl, RWKV_PROJ), lambda b, i: (b, i, 0)),
            pl.BlockSpec((tl, SL_IN), lambda b, i: (i, b)),
        ],
        out_shape=[
            jax.ShapeDtypeStruct((B, L, D_MODEL), F32),
            jax.ShapeDtypeStruct((B, L, RWKV_PROJ), F32),
            jax.ShapeDtypeStruct((L, B * SL_IN), F32),
        ],
        compiler_params=pltpu.CompilerParams(
            dimension_semantics=("parallel", "parallel"), vmem_limit_bytes=VMEM_LIMIT),
        name="ffn1_inproj",
    )(x, w["n1"], w["wg1"], w["wu1"], w["wd1"], w["nm"], w["win"])


def _k3_kernel(h_ref, osl_ref, orw_ref, wout_ref, n2_ref, wg_ref, wu_ref, wd_ref, fn_ref, o_ref, *, l, final):
    osl = osl_ref[...]
    mix = (_dot(osl[:, 0:S5_WIDTH].astype(BF16), wout_ref[0:S5_WIDTH, :])
           + _dot(orw_ref[...].astype(BF16), wout_ref[S5_WIDTH:S5_WIDTH + RWKV_WIDTH, :])
           + _dot(osl[:, S5_WIDTH:SL_OUT].astype(BF16), wout_ref[S5_WIDTH + RWKV_WIDTH:D_MODEL, :]))
    h2 = h_ref[...] + mix
    h3 = _swiglu_res(h2, _row(n2_ref, l), wg_ref, wu_ref, wd_ref)
    if final:
        h3 = _rms(h3, fn_ref[...])
    o_ref[...] = h3


def _k3(h, osl, orw, w, l, tl, final):
    B, L, _ = h.shape
    return pl.pallas_call(
        functools.partial(_k3_kernel, l=l, final=final),
        grid=(B, L // tl),
        in_specs=[
            pl.BlockSpec((None, tl, D_MODEL), lambda b, i: (b, i, 0)),
            pl.BlockSpec((tl, SL_OUT), lambda b, i: (i, b)),
            pl.BlockSpec((None, tl, RWKV_WIDTH), lambda b, i: (b, i, 0)),
            _layer_spec(w["wout"], l),
            _full_spec(w["n2"]),
            _layer_spec(w["wg2"], l),
            _layer_spec(w["wu2"], l),
            _layer_spec(w["wd2"], l),
            _full_spec(w["fnorm"]),
        ],
        out_specs=pl.BlockSpec((None, tl, D_MODEL), lambda b, i: (b, i, 0)),
        out_shape=jax.ShapeDtypeStruct((B, L, D_MODEL), F32),
        compiler_params=pltpu.CompilerParams(
            dimension_semantics=("parallel", "parallel"), vmem_limit_bytes=VMEM_LIMIT),
        name="outproj_ffn2",
    )(h, osl, orw, w["wout"], w["n2"], w["wg2"], w["wu2"], w["wd2"], w["fnorm"])


def _s5_prep_kernel(lr_ref, li_ref, ldt_ref, br_ref, bi_ref, abr_ref, abi_ref, bbr_ref, bbi_ref):
    lr = lr_ref[...]
    li = li_ref[...]
    dt = jnp.exp(ldt_ref[...])
    mag = jnp.exp(lr * dt)
    ab_re = mag * jnp.cos(li * dt)
    ab_im = mag * jnp.sin(li * dt)
    den = lr * lr + li * li
    f_re = ((ab_re - 1.0) * lr + ab_im * li) / den
    f_im = (ab_im * lr - (ab_re - 1.0) * li) / den
    abr_ref[...] = ab_re
    abi_ref[...] = ab_im
    br = br_ref[...]
    bi = bi_ref[...]
    bbr_ref[...] = f_re[:, :, None, :] * br - f_im[:, :, None, :] * bi
    bbi_ref[...] = f_re[:, :, None, :] * bi + f_im[:, :, None, :] * br


def _s5_prep(lam_re, lam_im, log_dt, b_re, b_im):
    G, N, C = S5_GROUPS, S5_STATE, S5_GROUP
    ldt = jnp.broadcast_to(log_dt[:, :, None], (DEPTH, G, N))
    br = jnp.swapaxes(b_re, 2, 3)
    bi = jnp.swapaxes(b_im, 2, 3)
    return pl.pallas_call(
        _s5_prep_kernel,
        out_shape=[
            jax.ShapeDtypeStruct((DEPTH, G, N), F32),
            jax.ShapeDtypeStruct((DEPTH, G, N), F32),
            jax.ShapeDtypeStruct((DEPTH, G, C, N), F32),
            jax.ShapeDtypeStruct((DEPTH, G, C, N), F32),
        ],
        name="s5_discretise",
    )(lam_re, lam_im, ldt, br, bi)


def _s5lru_kernel(psl_ref, ab_ref, win_ref, cre_ref, cim_ref, d_ref, gw_ref, gb_ref,
                  cw_ref, cb_ref, wa_ref, wx_ref, ba_ref, bx_ref, lam_ref,
                  h0_ref, l0_ref, c0_ref,
                  o_ref, hT_ref, lT_ref, cT_ref,
                  hs_scr, ls_scr, xc_scr, xr_scr, xi_scr, la_scr, lb_scr, *, l, B, Tc, unroll):
    c = pl.program_id(0)
    R = Tc * B

    @pl.when(c == 0)
    def _():
        hs_scr[...] = h0_ref[...]
        ls_scr[...] = l0_ref[...]
        xc_scr[0:CONV_HIST] = c0_ref[...]

    psl = psl_ref[...]
    u = psl[:, :, 0:S5_WIDTH].reshape(R, S5_WIDTH)
    gate_in = psl[:, :, S5_WIDTH + LRU_WIDTH:SL_IN].reshape(R, LRU_WIDTH)

    ub = u.astype(BF16)
    half = S5_BLOCK_STATES
    for j in range(S5_BLOCKS):
        x = _dot(ub[:, LANE * j:LANE * (j + 1)], win_ref[j])
        xr_scr[:, :, half * j:half * (j + 1)] = x[:, 0:half].reshape(Tc, B, half)
        xi_scr[:, :, half * j:half * (j + 1)] = x[:, half:2 * half].reshape(Tc, B, half)

    xc_scr[CONV_HIST:CONV_HIST + Tc] = psl[:, :, S5_WIDTH:S5_WIDTH + LRU_WIDTH]
    cw = cw_ref[...]
    xc = _row(cb_ref, l) + xc_scr[0:Tc] * cw[0:1]
    for j in range(1, CONV_WIDTH):
        xc = xc + xc_scr[j:j + Tc] * cw[j:j + 1]
    new_hist = xc_scr[Tc:Tc + CONV_HIST]
    xc_scr[0:CONV_HIST] = new_hist
    xc2 = xc.reshape(R, LRU_WIDTH)
    xcb = xc2.astype(BF16)
    gate_a = _sigmoid(_dot(xcb, wa_ref[...]) + _row(ba_ref, l))
    gate_x = _sigmoid(_dot(xcb, wx_ref[...]) + _row(bx_ref, l))
    log_a = LRU_C * gate_a * (-_softplus(-_row(lam_ref, l)))
    la_scr[...] = jnp.exp(log_a).reshape(Tc, B, LRU_WIDTH)
    lb_scr[...] = (jnp.sqrt(_neg_expm1(2.0 * log_a)) * (gate_x * xc2)).reshape(Tc, B, LRU_WIDTH)

    ar = jnp.broadcast_to(ab_ref[0:1, :], (B, S5_LANES))
    ai = jnp.broadcast_to(ab_ref[1:2, :], (B, S5_LANES))

    def step(t, carry):
        hr, hi, hl = carry
        nr = ar * hr - ai * hi + xr_scr[t]
        ni = ar * hi + ai * hr + xi_scr[t]
        xr_scr[t] = nr
        xi_scr[t] = ni
        nl = la_scr[t] * hl + lb_scr[t]
        lb_scr[t] = nl
        return nr, ni, nl

    hr, hi, hl = lax.fori_loop(0, Tc, step, (hs_scr[0], hs_scr[1], ls_scr[...]), unroll=unroll)
    hs_scr[0] = hr
    hs_scr[1] = hi
    ls_scr[...] = hl

    hrb = xr_scr[...].reshape(R, S5_LANES).astype(BF16)
    hib = xi_scr[...].reshape(R, S5_LANES).astype(BF16)
    ys = []
    for j in range(S5_BLOCKS):
        sl = slice(half * j, half * (j + 1))
        ys.append(_dot(hrb[:, sl], cre_ref[j]) - _dot(hib[:, sl], cim_ref[j]))
    y = jnp.concatenate(ys, axis=-1) + _row(d_ref, l) * u
    z = _gelu(y)
    o_s5 = z * _sigmoid(_dot(z.astype(BF16), gw_ref[...].astype(BF16)) + _row(gb_ref, l))
    o_lru = lb_scr[...].reshape(R, LRU_WIDTH) * _gelu(gate_in)
    o_ref[:, :, 0:S5_WIDTH] = o_s5.reshape(Tc, B, S5_WIDTH)
    o_ref[:, :, S5_WIDTH:SL_OUT] = o_lru.reshape(Tc, B, LRU_WIDTH)

    @pl.when(c == pl.num_programs(0) - 1)
    def _():
        hT_ref[...] = hs_scr[...]
        lT_ref[...] = ls_scr[...]
        cT_ref[...] = xc_scr[0:CONV_HIST]


def _s5lru(psl3, w, l, st, Tc):
    L, B, _ = psl3.shape
    li = st["li"]
    unroll = True if Tc <= 2 else 2
    return pl.pallas_call(
        functools.partial(_s5lru_kernel, l=l, B=B, Tc=Tc, unroll=unroll),
        grid=(L // Tc,),
        in_specs=[
            pl.BlockSpec((Tc, B, SL_IN), lambda c: (c, 0, 0)),
            _layer_spec(w["s5_ab"], l),
            _layer_spec(w["s5_win"], l),
            _layer_spec(w["s5_cre"], l),
            _layer_spec(w["s5_cim"], l),
            _full_spec(w["s5_d"]),
            _layer_spec(w["s5_glu_w"], l),
            _full_spec(w["s5_glu_b"]),
            _layer_spec(w["lru_conv_w"], l),
            _full_spec(w["lru_conv_b"]),
            _layer_spec(w["lru_wa"], l),
            _layer_spec(w["lru_wx"], l),
            _full_spec(w["lru_b_a"]),
            _full_spec(w["lru_b_x"]),
            _full_spec(w["lru_lambda"]),
            _layer_spec(st["s5"], li),
            _layer_spec(st["lru"], li),
            _layer_spec(st["conv"], li),
        ],
        out_specs=[
            pl.BlockSpec((Tc, B, SL_OUT), lambda c: (c, 0, 0)),
            pl.BlockSpec((2, B, S5_LANES), lambda c: (0, 0, 0)),
            pl.BlockSpec((B, LRU_WIDTH), lambda c: (0, 0)),
            pl.BlockSpec((CONV_HIST, B, LRU_WIDTH), lambda c: (0, 0, 0)),
        ],
        out_shape=[
            jax.ShapeDtypeStruct((L, B, SL_OUT), F32),
            jax.ShapeDtypeStruct((2, B, S5_LANES), F32),
            jax.ShapeDtypeStruct((B, LRU_WIDTH), F32),
            jax.ShapeDtypeStruct((CONV_HIST, B, LRU_WIDTH), F32),
        ],
        scratch_shapes=[
            pltpu.VMEM((2, B, S5_LANES), F32),
            pltpu.VMEM((B, LRU_WIDTH), F32),
            pltpu.VMEM((Tc + CONV_HIST, B, LRU_WIDTH), F32),
            pltpu.VMEM((Tc, B, S5_LANES), F32),
            pltpu.VMEM((Tc, B, S5_LANES), F32),
            pltpu.VMEM((Tc, B, LRU_WIDTH), F32),
            pltpu.VMEM((Tc, B, LRU_WIDTH), F32),
        ],
        compiler_params=pltpu.CompilerParams(
            dimension_semantics=("arbitrary",), vmem_limit_bytes=VMEM_LIMIT),
        name="s5_rglru_scan",
    )(psl3, w["s5_ab"], w["s5_win"], w["s5_cre"], w["s5_cim"], w["s5_d"], w["s5_glu_w"], w["s5_glu_b"],
      w["lru_conv_w"], w["lru_conv_b"], w["lru_wa"], w["lru_wx"], w["lru_b_a"], w["lru_b_x"], w["lru_lambda"],
      st["s5"], st["lru"], st["conv"])


_RWKV_PARAMS = ("rwkv_mu", "rwkv_w0", "rwkv_w_up", "rwkv_a0", "rwkv_a_up", "rwkv_g_up", "rwkv_k_k", "rwkv_k_a",
                "rwkv_r_k", "rwkv_ln_w", "rwkv_ln_b", "rwkv_ones")


def _rwkv_param_specs(w, l):
    layered = ("rwkv_w_up", "rwkv_a_up", "rwkv_g_up")
    return [_layer_spec(w[k], l) if k in layered else _full_spec(w[k]) for k in _RWKV_PARAMS]


def _rwkv_mats(prm):
    wup_ref, aup_ref, gup_ref, ones_ref = prm[2], prm[4], prm[5], prm[11]
    zpad = jnp.zeros((DECAY_RANK, RWKV_WIDTH), BF16)
    wup = jnp.concatenate([wup_ref[...].astype(BF16), zpad], axis=0)
    aup = jnp.concatenate([zpad, aup_ref[...].astype(BF16)], axis=0)
    return wup, aup, gup_ref[...].astype(BF16), ones_ref[...]


def _segsum(x, ones):
    xb = x.astype(BF16)
    return jnp.concatenate([_dot(xb[:, LANE * p:LANE * (p + 1)], ones) for p in range(RWKV_PAIRS)], axis=-1)


def _rwkv_token_math(p, prev, prm, mats, l):
    mu_ref, w0_ref, _, a0_ref, _, _, kk_ref, ka_ref, rk_ref = prm[0:9]
    wup, aup, gup, ones = mats
    W = RWKV_WIDTH
    xm = p + (prev - p) * _row(mu_ref, l)
    r = xm[:, 0:W]
    k = xm[:, W:2 * W]
    v = xm[:, 2 * W:3 * W]
    xwa = xm[:, 3 * W:3 * W + DECAY_RANK + ICL_RANK]
    xg = xm[:, 3 * W + DECAY_RANK + ICL_RANK:RWKV_PROJ]
    lw = _row(w0_ref, l) + _dot(jnp.tanh(xwa).astype(BF16), wup)
    ld = -math.exp(-0.5) * _sigmoid(lw)
    a = _sigmoid(_row(a0_ref, l) + _dot(xwa.astype(BF16), aup))
    g = _dot(_sigmoid(xg).astype(BF16), gup)
    kkr = k * _row(kk_ref, l)
    kk = kkr / jnp.maximum(jnp.sqrt(_segsum(kkr * kkr, ones)), 1e-12)
    k2 = k * (1.0 + (a - 1.0) * _row(ka_ref, l))
    bonus = _segsum(r * k2 * _row(rk_ref, l), ones) * v
    return r, k2, v, kk, a, ld, g, bonus


def _rwkv_out(y, bonus, g, prm, ones, l):
    lnw_ref, lnb_ref = prm[9], prm[10]
    inv = 1.0 / RWKV_HEAD
    mean = _segsum(y, ones) * inv
    d = y - mean
    var = _segsum(d * d, ones) * inv
    yn = d * lax.rsqrt(var + RWKV_LN_EPS) * _row(lnw_ref, l) + _row(lnb_ref, l)
    return (yn + bonus) * g


RWKV_GROUP_HEADS = 4
RWKV_GROUP_LANES = RWKV_GROUP_HEADS * RWKV_HEAD


def _rwkv_groups(B):
    full = [[(b, 0, RWKV_GROUP_LANES)] for b in range(B)]
    rest = RWKV_WIDTH - RWKV_GROUP_LANES
    tail = [[(b, RWKV_GROUP_LANES, rest), (b + 1, RWKV_GROUP_LANES, rest)] for b in range(0, B, 2)]
    return full + tail


def _rwkv_chunk_kernel(*refs, l, B, Tc):
    n_prm = len(_RWKV_PARAMS)
    p_ref, s0_ref, sh0_ref = refs[0:3]
    prm = refs[3:3 + n_prm]
    tri_ref, o_ref, sT_ref = refs[3 + n_prm:6 + n_prm]
    st_scr, xs_scr, y_scr, pt_scr = refs[6 + n_prm:10 + n_prm]
    src = dict(zip(("a", "b", "k", "r", "v", "bp", "kp"), refs[10 + n_prm:]))
    c = pl.program_id(0)
    R = B * Tc
    W = RWKV_WIDTH
    GL = RWKV_GROUP_LANES
    GH = RWKV_GROUP_HEADS
    carry_row = SUBLANE - 1

    @pl.when(c == 0)
    def _():
        st_scr[...] = s0_ref[...]
        xs_scr[:, carry_row:SUBLANE, :] = sh0_ref[...]

    p = p_ref[...]
    xs_scr[:, SUBLANE:SUBLANE + Tc, :] = p
    prev = xs_scr[:, carry_row:carry_row + Tc, :]
    xs_scr[:, carry_row:SUBLANE, :] = p[:, Tc - 1:Tc, :]
    mats = _rwkv_mats(prm)
    r, k2, v, kk, a, ld, g, bonus = _rwkv_token_math(
        p.reshape(R, RWKV_PROJ), prev.reshape(R, RWKV_PROJ), prm, mats, l)

    h1, h2, h3 = _split3(ld)
    tri = tri_ref[...]
    cls = []
    for b in range(B):
        rows = slice(b * Tc, (b + 1) * Tc)
        cls.append(_dot(tri, h1[rows]) + _dot(tri, h2[rows]) + _dot(tri, h3[rows]))
    cl = jnp.concatenate(cls, axis=0)
    pt = jnp.exp(cl.reshape(B, Tc, W)[:, Tc - 1:Tc, :])
    inv_p = jnp.exp(-cl)
    to_end = (pt * inv_p.reshape(B, Tc, W)).reshape(R, W)
    beta = kk * a
    src["a"][...] = (-kk * jnp.exp(cl - ld)).astype(BF16)
    src["b"][...] = (beta * inv_p).astype(BF16)
    src["k"][...] = (k2 * inv_p).astype(BF16)
    src["r"][...] = (r * jnp.exp(cl)).astype(BF16)
    src["v"][...] = v.astype(BF16)
    src["bp"][...] = (beta * to_end).astype(BF16)
    src["kp"][...] = (k2 * to_end).astype(BF16)
    pt_scr[...] = pt

    assert Tc & (Tc - 1) == 0
    n_doubling = max(1, (Tc - 1).bit_length())
    groups = _rwkv_groups(B)

    def block_masks(blk):
        lane = lax.broadcasted_iota(jnp.int32, (1, GH * blk), 1)
        return [((lane >= blk * h) & (lane < blk * (h + 1))).astype(BF16) for h in range(GH)]

    head_masks = block_masks(RWKV_HEAD)
    time_masks = block_masks(Tc)

    def stack(x, masks):
        return jnp.concatenate([x * m for m in masks], axis=0)

    ti = lax.broadcasted_iota(jnp.int32, (Tc, GH * Tc), 0)
    si = lax.broadcasted_iota(jnp.int32, (Tc, GH * Tc), 1) & (Tc - 1)
    strict = (si < ti).astype(BF16)
    incl = (si <= ti).astype(BF16)
    same_head = ((lax.broadcasted_iota(jnp.int32, (GL, GL), 0) // RWKV_HEAD)
                 == (lax.broadcasted_iota(jnp.int32, (GL, GL), 1) // RWKV_HEAD)).astype(F32)

    def operand(name, grp):
        parts = [src[name][b * Tc:(b + 1) * Tc, lo:lo + n] for b, lo, n in grp]
        return parts[0] if len(parts) == 1 else jnp.concatenate(parts, axis=-1)

    v4s, gss, ns, wrbs, wrks, xs = [], [], [], [], [], []
    for gi, grp in enumerate(groups):
        lhs = jnp.concatenate([operand("a", grp), operand("r", grp)], axis=0)
        v4 = stack(operand("v", grp), head_masks)
        gb = _dot_nt(lhs, stack(operand("b", grp), head_masks)).astype(BF16)
        gk = _dot_nt(lhs, stack(operand("k", grp), head_masks)).astype(BF16)
        gs = _dot_nt(lhs, st_scr[gi].astype(BF16))
        v4s.append(v4)
        gss.append(gs[Tc:2 * Tc])
        ns.append(gb[0:Tc] * strict)
        wrbs.append(gb[Tc:2 * Tc] * incl)
        wrks.append(gk[Tc:2 * Tc] * incl)
        xs.append(gs[0:Tc] + _dot(gk[0:Tc] * strict, v4))
    for j in range(n_doubling):
        for i in range(len(groups)):
            xs[i] = xs[i] + _dot(ns[i], stack(xs[i].astype(BF16), head_masks))
        if j + 1 < n_doubling:
            for i in range(len(groups)):
                ns[i] = _dot(ns[i], stack(ns[i], time_masks)).astype(BF16)
    for gi, grp in enumerate(groups):
        sab = xs[gi].astype(BF16)
        y = gss[gi] + _dot(wrbs[gi], stack(sab, head_masks)) + _dot(wrks[gi], v4s[gi])
        lane0 = 0
        for b, lo, n in grp:
            y_scr[b * Tc:(b + 1) * Tc, lo:lo + n] = y[:, lane0:lane0 + n]
            lane0 += n
        upd = _dot_tn(jnp.concatenate([sab, operand("v", grp)], axis=0),
                      jnp.concatenate([operand("bp", grp), operand("kp", grp)], axis=0))
        ptg = [pt_scr[b, :, lo:lo + n] for b, lo, n in grp]
        ptg = ptg[0] if len(ptg) == 1 else jnp.concatenate(ptg, axis=-1)
        st_scr[gi] = st_scr[gi] * ptg + upd * same_head

    out = _rwkv_out(y_scr[...], bonus, g, prm, mats[3], l)
    o_ref[...] = out.reshape(B, Tc, W)

    @pl.when(c == pl.num_programs(0) - 1)
    def _():
        sT_ref[...] = st_scr[...]


def _rwkv_chunk(prw, w, l, s0, sh0, Tc):
    B, L, _ = prw.shape
    R = B * Tc
    W = RWKV_WIDTH
    GL = RWKV_GROUP_LANES
    NG = len(_rwkv_groups(B))
    tri = jnp.tril(jnp.ones((Tc, Tc), F32)).astype(BF16)
    return pl.pallas_call(
        functools.partial(_rwkv_chunk_kernel, l=l, B=B, Tc=Tc),
        grid=(L // Tc,),
        in_specs=[
            pl.BlockSpec((B, Tc, RWKV_PROJ), lambda c: (0, c, 0)),
            _full_spec(s0),
            _full_spec(sh0),
        ] + _rwkv_param_specs(w, l) + [_full_spec(tri)],
        out_specs=[
            pl.BlockSpec((B, Tc, W), lambda c: (0, c, 0)),
            pl.BlockSpec((NG, GL, GL), lambda c: (0, 0, 0)),
        ],
        out_shape=[
            jax.ShapeDtypeStruct((B, L, W), F32),
            jax.ShapeDtypeStruct((NG, GL, GL), F32),
        ],
        scratch_shapes=[
            pltpu.VMEM((NG, GL, GL), F32),
            pltpu.VMEM((B, Tc + SUBLANE, RWKV_PROJ), F32),
            pltpu.VMEM((R, W), F32),
            pltpu.VMEM((B, 1, W), F32),
        ] + [pltpu.VMEM((R, W), BF16)] * 7,
        compiler_params=pltpu.CompilerParams(
            dimension_semantics=("arbitrary",), vmem_limit_bytes=VMEM_LIMIT),
        name="rwkv7_chunked",
    )(prw, s0, sh0, *[w[k] for k in _RWKV_PARAMS], tri)


def _rwkv_step_kernel(*refs, l):
    n_prm = len(_RWKV_PARAMS)
    p_ref, sh_ref, s_ref = refs[0:3]
    prm = refs[3:3 + n_prm]
    o_ref, sn_ref = refs[3 + n_prm:5 + n_prm]
    nkk_t, w_t, beta_t, v_t, k2_t, r_t, y_t, g_scr, bonus_scr = refs[5 + n_prm:]
    h = pl.program_id(0)
    N = RWKV_HEAD

    @pl.when(h == 0)
    def _():
        r, k2, v, kk, a, ld, g, bonus = _rwkv_token_math(p_ref[...], sh_ref[...], prm, _rwkv_mats(prm), l)
        nkk_t[...] = (-kk).T
        w_t[...] = jnp.exp(ld).T
        beta_t[...] = (kk * a).T
        v_t[...] = v.T
        k2_t[...] = k2.T
        r_t[...] = r.T
        g_scr[...] = g
        bonus_scr[...] = bonus

    rows = pl.ds(pl.multiple_of(h * N, N), N)
    s = s_ref[...]
    sa = jnp.sum(s * nkk_t[rows, :][None], axis=1)
    sn = s * w_t[rows, :][None] + sa[:, None, :] * beta_t[rows, :][None] + v_t[rows, :][:, None, :] * k2_t[rows, :][None]
    sn_ref[...] = sn
    y_t[rows, :] = jnp.sum(sn * r_t[rows, :][None], axis=1)

    @pl.when(h == pl.num_programs(0) - 1)
    def _():
        o_ref[...] = _rwkv_out(y_t[...].T, bonus_scr[...], g_scr[...], prm, prm[11][...], l)


def _rwkv_step(prw, w, l, s_all, sh_all):
    B = prw.shape[0]
    W = RWKV_WIDTH
    N = RWKV_HEAD
    return pl.pallas_call(
        functools.partial(_rwkv_step_kernel, l=l),
        grid=(RWKV_HEADS,),
        in_specs=[
            _full_spec(prw),
            _layer_spec(sh_all, l),
            pl.BlockSpec((None, None, N, N, B), lambda h: (l, h, 0, 0, 0)),
        ] + _rwkv_param_specs(w, l),
        out_specs=[
            pl.BlockSpec((B, W), lambda h: (0, 0)),
            pl.BlockSpec((None, N, N, B), lambda h: (h, 0, 0, 0)),
        ],
        out_shape=[
            jax.ShapeDtypeStruct((B, W), F32),
            jax.ShapeDtypeStruct((RWKV_HEADS, N, N, B), F32),
        ],
        scratch_shapes=[pltpu.VMEM((W, B), F32)] * 7 + [pltpu.VMEM((B, W), F32)] * 2,
        compiler_params=pltpu.CompilerParams(
            dimension_semantics=("arbitrary",), vmem_limit_bytes=VMEM_LIMIT),
        name="rwkv7_step",
    )(prw, sh_all, s_all, *[w[k] for k in _RWKV_PARAMS])


def _block_diag(blocks):
    *lead, n, r, c = blocks.shape
    eye = jnp.eye(n, dtype=blocks.dtype)
    out = blocks[..., :, :, None, :] * eye[:, None, :, None]
    return out.reshape(*lead, n * r, n * c)


def _prepare_weights(a):
    G, C, N = S5_GROUPS, S5_GROUP, S5_STATE
    abr, abi, bbr, bbi = _s5_prep(a["s5_lambda_re"], a["s5_lambda_im"], a["s5_log_dt"], a["s5_b_re"], a["s5_b_im"])

    def in_side(bb):
        return _block_diag(bb.reshape(DEPTH, S5_BLOCKS, S5_GROUPS_PER_BLOCK, C, N))

    def out_side(cc):
        ct = jnp.swapaxes(cc, 2, 3)
        return _block_diag(ct.reshape(DEPTH, S5_BLOCKS, S5_GROUPS_PER_BLOCK, N, C))

    w = {k: a[k] for k in (
        "s5_d", "s5_glu_w", "s5_glu_b", "lru_conv_w", "lru_conv_b", "lru_b_a", "lru_b_x", "lru_lambda",
        "rwkv_mu", "rwkv_w0", "rwkv_w_up", "rwkv_a0", "rwkv_a_up", "rwkv_g_up", "rwkv_k_k", "rwkv_k_a",
        "rwkv_ln_w", "rwkv_ln_b")}
    w.update({
        "n1": a["ffn1_norm"], "nm": a["mix_norm"], "n2": a["ffn2_norm"],
        "fnorm": a["final_norm"].reshape(1, D_MODEL),
        "wg1": a["ffn1_w_gate"].astype(BF16), "wu1": a["ffn1_w_up"].astype(BF16),
        "wd1": a["ffn1_w_down"].astype(BF16), "win": a["w_in"].astype(BF16), "wout": a["w_out"].astype(BF16),
        "wg2": a["ffn2_w_gate"].astype(BF16), "wu2": a["ffn2_w_up"].astype(BF16),
        "wd2": a["ffn2_w_down"].astype(BF16),
        "s5_ab": jnp.stack([abr.reshape(DEPTH, S5_LANES), abi.reshape(DEPTH, S5_LANES)], axis=1),
        "s5_win": jnp.concatenate([in_side(bbr), in_side(bbi)], axis=-1).astype(BF16),
        "s5_cre": out_side(a["s5_c_re"]).astype(BF16),
        "s5_cim": out_side(a["s5_c_im"]).astype(BF16),
        "lru_wa": _block_diag(a["lru_w_a"]).astype(BF16),
        "lru_wx": _block_diag(a["lru_w_x"]).astype(BF16),
        "rwkv_r_k": a["rwkv_r_k"].reshape(DEPTH, RWKV_WIDTH),
        "rwkv_ones": _block_diag(jnp.ones((2, RWKV_HEAD, RWKV_HEAD), F32)).astype(BF16),
    })
    return w


def _groups_to_heads(sg, B):
    N = RWKV_HEAD
    diag = jnp.stack([sg[:, N * i:N * (i + 1), N * i:N * (i + 1)] for i in range(RWKV_GROUP_HEADS)], axis=1)
    first = diag[:B]
    rest = diag[B:].reshape(B, RWKV_HEADS - RWKV_GROUP_HEADS, N, N)
    return jnp.concatenate([first, rest], axis=1)


def _seq_mixers(prw, psl3, st, w, l, Tc):
    L = prw.shape[1]
    osl, hT, lT, cT = _s5lru(psl3, w, l, st, Tc)
    orw, sT = _rwkv_chunk(prw, w, l, st["rwkv"], st["shift"], Tc)
    new = {"s5": hT[None], "lru": lT[None], "conv": cT[None], "li": 0,
           "rwkv": sT, "shift": prw[:, L - 1:L, :]}
    return osl, orw, new


def kernel(x_prompt, x_sample, state_s5_re, state_s5_im, state_rwkv, state_rwkv_shift, state_lru, state_lru_conv, meta_tokens, ffn1_norm, ffn1_w_gate, ffn1_w_up, ffn1_w_down, mix_norm, w_in, s5_lambda_re, s5_lambda_im, s5_log_dt, s5_b_re, s5_b_im, s5_c_re, s5_c_im, s5_d, s5_glu_w, s5_glu_b, rwkv_mu, rwkv_w0, rwkv_w_up, rwkv_a0, rwkv_a_up, rwkv_g_up, rwkv_k_k, rwkv_k_a, rwkv_r_k, rwkv_ln_w, rwkv_ln_b, lru_conv_w, lru_conv_b, lru_w_a, lru_b_a, lru_w_x, lru_b_x, lru_lambda, w_out, ffn2_norm, ffn2_w_gate, ffn2_w_up, ffn2_w_down, final_norm):
    w = _prepare_weights(dict(locals()))
    BP, SEQ, _ = x_prompt.shape
    BS = x_sample.shape[0]
    NS = N_META + BS

    zero = {
        "s5": jnp.zeros((1, 2, BP, S5_LANES), F32),
        "lru": jnp.zeros((1, BP, LRU_WIDTH), F32),
        "conv": jnp.zeros((1, CONV_HIST, BP, LRU_WIDTH), F32),
        "li": 0,
        "rwkv": jnp.zeros((len(_rwkv_groups(BP)), RWKV_GROUP_LANES, RWKV_GROUP_LANES), F32),
        "shift": jnp.zeros((BP, 1, RWKV_PROJ), F32),
    }
    sample_in = {
        "s5": jnp.stack([state_s5_re.reshape(DEPTH, BS, S5_LANES), state_s5_im.reshape(DEPTH, BS, S5_LANES)], axis=1),
        "lru": state_lru,
        "conv": jnp.swapaxes(state_lru_conv, 1, 2),
    }

    rwkv_in = jnp.transpose(state_rwkv, (0, 2, 3, 4, 1))

    xp = x_prompt
    xs = jnp.concatenate([meta_tokens.astype(F32), x_sample.reshape(BS, D_MODEL)], axis=0)[None]
    p_states = []
    s_states = []
    for l in range(DEPTH):
        final = l == DEPTH - 1
        hs, prw_s, psl_s = _k1(xs, w, l, NS)
        prw_m = jnp.broadcast_to(prw_s[:, :N_META], (BP, N_META, RWKV_PROJ))
        psl_m = jnp.broadcast_to(psl_s[:N_META, None, :], (N_META, BP, SL_IN))
        osl_m, orw_m, st_meta = _seq_mixers(prw_m, psl_m, zero, w, l, N_META)

        st_l = dict(sample_in, li=l)
        osl_s, hT, lT, cT = _s5lru(psl_s[N_META:].reshape(1, BS, SL_IN), w, l, st_l, 1)
        orw_s, sT = _rwkv_step(prw_s[0, N_META:], w, l, rwkv_in, state_rwkv_shift)
        s_states.append({"s5": hT, "lru": lT, "conv": cT, "rwkv": sT, "shift": prw_s[0, N_META:]})

        hp, prw_p, psl_p = _k1(xp, w, l, PROMPT_TOKEN_TILE)
        osl_p, orw_p, st_p = _seq_mixers(prw_p, psl_p.reshape(SEQ, BP, SL_IN), st_meta, w, l, PROMPT_CHUNK)
        p_states.append(st_p)
        xp = _k3(hp, osl_p.reshape(SEQ, BP * SL_OUT), orw_p, w, l, PROMPT_TOKEN_TILE, final)

        if final:
            xs = _k3(hs[:, N_META:], osl_s.reshape(BS, SL_OUT), orw_s[None], w, l, BS, True)
        else:
            osl = jnp.concatenate([osl_m[:, 0, :], osl_s.reshape(BS, SL_OUT)], axis=0)
            orw = jnp.concatenate([orw_m[0], orw_s], axis=0)[None]
            xs = _k3(hs, osl, orw, w, l, NS, False)

    p_out = (
        jnp.stack([s["s5"][0, 0].reshape(BP, S5_GROUPS, S5_STATE) for s in p_states]),
        jnp.stack([s["s5"][0, 1].reshape(BP, S5_GROUPS, S5_STATE) for s in p_states]),
        jnp.stack([_groups_to_heads(s["rwkv"], BP) for s in p_states]),
        jnp.stack([s["shift"][:, 0, :] for s in p_states]),
        jnp.stack([s["lru"][0] for s in p_states]),
        jnp.stack([jnp.swapaxes(s["conv"][0], 0, 1) for s in p_states]),
    )
    s_out = (
        jnp.stack([s["s5"][0].reshape(BS, S5_GROUPS, S5_STATE) for s in s_states]),
        jnp.stack([s["s5"][1].reshape(BS, S5_GROUPS, S5_STATE) for s in s_states]),
        jnp.transpose(jnp.stack([s["rwkv"] for s in s_states]), (0, 4, 1, 2, 3)),
        jnp.stack([s["shift"] for s in s_states]),
        jnp.stack([s["lru"] for s in s_states]),
        jnp.stack([jnp.swapaxes(s["conv"], 0, 1) for s in s_states]),
    )
    return (xp, xs.reshape(BS, 1, D_MODEL)) + p_out + s_out
```

```python
import functools
import math

import jax
import jax.numpy as jnp
from jax import lax
from jax.experimental import pallas as pl
from jax.experimental.pallas import tpu as pltpu

F32 = jnp.float32
BF16 = jnp.bfloat16

D_MODEL = 1024
DEPTH = 2
N_META = 16
D_FF = 2816
NORM_EPS = 1e-6
FFN_RES_SCALE = 0.5
S5_WIDTH = 384
S5_GROUP = 16
S5_GROUPS = 24
S5_STATE = 64
S5_LANES = S5_GROUPS * S5_STATE
RWKV_HEAD = 64
RWKV_WIDTH = 384
RWKV_HEADS = 6
RWKV_PAIRS = 3
DECAY_RANK = 64
ICL_RANK = 64
GATE_RANK = 128
RWKV_PROJ = 1408
RWKV_LN_EPS = 64e-5
LRU_WIDTH = 256
LRU_BLOCKS = 4
CONV_WIDTH = 4
CONV_HIST = CONV_WIDTH - 1
LRU_C = 8.0
D_IN = 2304
SL_IN = S5_WIDTH + 2 * LRU_WIDTH
SL_OUT = S5_WIDTH + LRU_WIDTH
LANE = 128
SUBLANE = 8
S5_GROUPS_PER_BLOCK = LANE // S5_GROUP
S5_BLOCKS = S5_GROUPS // S5_GROUPS_PER_BLOCK
S5_BLOCK_STATES = S5_GROUPS_PER_BLOCK * S5_STATE

V7X_VMEM_BYTES = 64 * 2 ** 20
VMEM_LIMIT = (V7X_VMEM_BYTES * 7) // 8

PROMPT_TOKEN_TILE = 512
ROW_SUBTILE = 256
PROMPT_CHUNK = 64


def _dot(a, b):
    return jnp.dot(a, b, preferred_element_type=F32)


def _dot_nt(a, b):
    return lax.dot_general(a, b, (((1,), (1,)), ((), ())), preferred_element_type=F32)


def _dot_tn(a, b):
    return lax.dot_general(a, b, (((0,), (0,)), ((), ())), preferred_element_type=F32)


def _split3(x):
    hi = x.astype(BF16)
    r = x - hi.astype(F32)
    mid = r.astype(BF16)
    lo = (r - mid.astype(F32)).astype(BF16)
    return hi, mid, lo


def _sigmoid(x):
    return 0.5 * (jnp.tanh(0.5 * x) + 1.0)


def _gelu(x):
    c = math.sqrt(2.0 / math.pi)
    return 0.5 * x * (1.0 + jnp.tanh(c * (x + 0.044715 * (x * x * x))))


def _neg_expm1(z):
    t = jnp.tanh(0.5 * z)
    return -2.0 * t / (1.0 - t)


def _softplus(x):
    return jnp.maximum(x, 0.0) + jnp.log1p(jnp.exp(-jnp.abs(x)))


def _rms(x, g):
    return x * lax.rsqrt(jnp.mean(x * x, axis=-1, keepdims=True) + NORM_EPS) * g


def _swiglu_res(x, g_norm, wg_ref, wu_ref, wd_ref):
    xn = _rms(x, g_norm).astype(BF16)
    yield
    g = _dot(xn, wg_ref[...])
    u = _dot(xn, wu_ref[...])
    yield
    a = (g * _sigmoid(g) * u).astype(BF16)
    yield
    d = _dot(a, wd_ref[...])
    yield
    return x + FFN_RES_SCALE * d


def _staggered(gens):
    pending = list(gens)
    active = []
    while pending or active:
        if pending:
            active.append(pending.pop(0))
        for g in list(active):
            try:
                next(g)
            except StopIteration:
                active.remove(g)


def _row_tiles(n_rows):
    sub = ROW_SUBTILE if n_rows % ROW_SUBTILE == 0 else n_rows
    return [slice(r, r + sub) for r in range(0, n_rows, sub)]


def _full_spec(arr):
    nd = arr.ndim
    return pl.BlockSpec(arr.shape, lambda *_: (0,) * nd, pipeline_mode=pl.Buffered(1))


def _layer_spec(arr, l):
    nd = arr.ndim - 1
    return pl.BlockSpec((None,) + arr.shape[1:], lambda *_: (l,) + (0,) * nd, pipeline_mode=pl.Buffered(1))


def _row(ref, l):
    return ref[l:l + 1, :]


def _k1_kernel(x_ref, n1_ref, wg_ref, wu_ref, wd_ref, nm_ref, win_ref, h_ref, prw_ref, psl_ref, *, l):
    def rows_pipeline(rows):
        h = yield from _swiglu_res(x_ref[rows, :], _row(n1_ref, l), wg_ref, wu_ref, wd_ref)
        h_ref[rows, :] = h
        hn = _rms(h, _row(nm_ref, l)).astype(BF16)
        yield
        psl_ref[rows, 0:S5_WIDTH] = _dot(hn, win_ref[:, 0:S5_WIDTH])
        prw_ref[rows, :] = _dot(hn, win_ref[:, S5_WIDTH:S5_WIDTH + RWKV_PROJ])
        psl_ref[rows, S5_WIDTH:SL_IN] = _dot(hn, win_ref[:, S5_WIDTH + RWKV_PROJ:D_IN])

    _staggered([rows_pipeline(rows) for rows in _row_tiles(x_ref.shape[0])])


def _k1(x, w, l, tl):
    B, L, _ = x.shape
    return pl.pallas_call(
        functools.partial(_k1_kernel, l=l),
        grid=(B, L // tl),
        in_specs=[
            pl.BlockSpec((None, tl, D_MODEL), lambda b, i: (b, i, 0)),
            _full_spec(w["n1"]),
            _layer_spec(w["wg1"], l),
            _layer_spec(w["wu1"], l),
            _layer_spec(w["wd1"], l),
            _full_spec(w["nm"]),
            _layer_spec(w["win"], l),
        ],
        out_specs=[
            pl.BlockSpec((None, tl, D_MODEL), lambda b, i: (b, i, 0)),
            pl.BlockSpec((None, tl, RWKV_PROJ), lambda b, i: (b, i, 0)),
            pl.BlockSpec((tl, SL_IN), lambda b, i: (i, b)),
        ],
        out_shape=[
            jax.ShapeDtypeStruct((B, L, D_MODEL), F32),
            jax.ShapeDtypeStruct((B, L, RWKV_PROJ), F32),
            jax.ShapeDtypeStruct((L, B * SL_IN), F32),
        ],
        compiler_params=pltpu.CompilerParams(
            dimension_semantics=("parallel", "parallel"), vmem_limit_bytes=VMEM_LIMIT),
        name="ffn1_inproj",
    )(x, w["n1"], w["wg1"], w["wu1"], w["wd1"], w["nm"], w["win"])


def _k3_kernel(h_ref, osl_ref, orw_ref, wout_ref, n2_ref, wg_ref, wu_ref, wd_ref, fn_ref, o_ref, *, l, final):
    def rows_pipeline(rows):
        osl = osl_ref[rows, :]
        mix = (_dot(osl[:, 0:S5_WIDTH].astype(BF16), wout_ref[0:S5_WIDTH, :])
               + _dot(orw_ref[rows, :].astype(BF16), wout_ref[S5_WIDTH:S5_WIDTH + RWKV_WIDTH, :])
               + _dot(osl[:, S5_WIDTH:SL_OUT].astype(BF16), wout_ref[S5_WIDTH + RWKV_WIDTH:D_MODEL, :]))
        yield
        h3 = yield from _swiglu_res(h_ref[rows, :] + mix, _row(n2_ref, l), wg_ref, wu_ref, wd_ref)
        if final:
            h3 = _rms(h3, fn_ref[...])
        o_ref[rows, :] = h3

    _staggered([rows_pipeline(rows) for rows in _row_tiles(h_ref.shape[0])])


def _k3(h, osl, orw, w, l, tl, final):
    B, L, _ = h.shape
    return pl.pallas_call(
        functools.partial(_k3_kernel, l=l, final=final),
        grid=(B, L // tl),
        in_specs=[
            pl.BlockSpec((None, tl, D_MODEL), lambda b, i: (b, i, 0)),
            pl.BlockSpec((tl, SL_OUT), lambda b, i: (i, b)),
            pl.BlockSpec((None, tl, RWKV_WIDTH), lambda b, i: (b, i, 0)),
            _layer_spec(w["wout"], l),
            _full_spec(w["n2"]),
            _layer_spec(w["wg2"], l),
            _layer_spec(w["wu2"], l),
            _layer_spec(w["wd2"], l),
            _full_spec(w["fnorm"]),
        ],
        out_specs=pl.BlockSpec((None, tl, D_MODEL), lambda b, i: (b, i, 0)),
        out_shape=jax.ShapeDtypeStruct((B, L, D_MODEL), F32),
        compiler_params=pltpu.CompilerParams(
            dimension_semantics=("parallel", "parallel"), vmem_limit_bytes=VMEM_LIMIT),
        name="outproj_ffn2",
    )(h, osl, orw, w["wout"], w["n2"], w["wg2"], w["wu2"], w["wd2"], w["fnorm"])


def _s5_prep_kernel(lr_ref, li_ref, ldt_ref, br_ref, bi_ref, abr_ref, abi_ref, bbr_ref, bbi_ref):
    lr = lr_ref[...]
    li = li_ref[...]
    dt = jnp.exp(ldt_ref[...])
    mag = jnp.exp(lr * dt)
    ab_re = mag * jnp.cos(li * dt)
    ab_im = mag * jnp.sin(li * dt)
    den = lr * lr + li * li
    f_re = ((ab_re - 1.0) * lr + ab_im * li) / den
    f_im = (ab_im * lr - (ab_re - 1.0) * li) / den
    abr_ref[...] = ab_re
    abi_ref[...] = ab_im
    br = br_ref[...]
    bi = bi_ref[...]
    bbr_ref[...] = f_re[:, :, None, :] * br - f_im[:, :, None, :] * bi
    bbi_ref[...] = f_re[:, :, None, :] * bi + f_im[:, :, None, :] * br


def _s5_prep(lam_re, lam_im, log_dt, b_re, b_im):
    G, N, C = S5_GROUPS, S5_STATE, S5_GROUP
    ldt = jnp.broadcast_to(log_dt[:, :, None], (DEPTH, G, N))
    br = jnp.swapaxes(b_re, 2, 3)
    bi = jnp.swapaxes(b_im, 2, 3)
    return pl.pallas_call(
        _s5_prep_kernel,
        out_shape=[
            jax.ShapeDtypeStruct((DEPTH, G, N), F32),
            jax.ShapeDtypeStruct((DEPTH, G, N), F32),
            jax.ShapeDtypeStruct((DEPTH, G, C, N), F32),
            jax.ShapeDtypeStruct((DEPTH, G, C, N), F32),
        ],
        name="s5_discretise",
    )(lam_re, lam_im, ldt, br, bi)


def _s5lru_kernel(psl_ref, ab_ref, win_ref, cre_ref, cim_ref, d_ref, gw_ref, gb_ref,
                  cw_ref, cb_ref, wa_ref, wx_ref, ba_ref, bx_ref, lam_ref,
                  h0_ref, l0_ref, c0_ref,
                  o_ref, hT_ref, lT_ref, cT_ref,
                  hs_scr, ls_scr, xc_scr, xr_scr, xi_scr, la_scr, lb_scr, *, l, B, Tc, unroll):
    c = pl.program_id(0)
    R = Tc * B

    @pl.when(c == 0)
    def _():
        hs_scr[...] = h0_ref[...]
        ls_scr[...] = l0_ref[...]
        xc_scr[0:CONV_HIST] = c0_ref[...]

    psl = psl_ref[...]
    u = psl[:, :, 0:S5_WIDTH].reshape(R, S5_WIDTH)
    gate_in = psl[:, :, S5_WIDTH + LRU_WIDTH:SL_IN].reshape(R, LRU_WIDTH)

    ub = u.astype(BF16)
    half = S5_BLOCK_STATES
    for j in range(S5_BLOCKS):
        x = _dot(ub[:, LANE * j:LANE * (j + 1)], win_ref[j])
        xr_scr[:, :, half * j:half * (j + 1)] = x[:, 0:half].reshape(Tc, B, half)
        xi_scr[:, :, half * j:half * (j + 1)] = x[:, half:2 * half].reshape(Tc, B, half)

    xc_scr[CONV_HIST:CONV_HIST + Tc] = psl[:, :, S5_WIDTH:S5_WIDTH + LRU_WIDTH]
    cw = cw_ref[...]
    xc = _row(cb_ref, l) + xc_scr[0:Tc] * cw[0:1]
    for j in range(1, CONV_WIDTH):
        xc = xc + xc_scr[j:j + Tc] * cw[j:j + 1]
    new_hist = xc_scr[Tc:Tc + CONV_HIST]
    xc_scr[0:CONV_HIST] = new_hist
    xc2 = xc.reshape(R, LRU_WIDTH)
    xcb = xc2.astype(BF16)
    gate_a = _sigmoid(_dot(xcb, wa_ref[...]) + _row(ba_ref, l))
    gate_x = _sigmoid(_dot(xcb, wx_ref[...]) + _row(bx_ref, l))
    log_a = LRU_C * gate_a * (-_softplus(-_row(lam_ref, l)))
    la_scr[...] = jnp.exp(log_a).reshape(Tc, B, LRU_WIDTH)
    lb_scr[...] = (jnp.sqrt(_neg_expm1(2.0 * log_a)) * (gate_x * xc2)).reshape(Tc, B, LRU_WIDTH)

    ar = jnp.broadcast_to(ab_ref[0:1, :], (B, S5_LANES))
    ai = jnp.broadcast_to(ab_ref[1:2, :], (B, S5_LANES))

    def step(t, carry):
        hr, hi, hl = carry
        nr = ar * hr - ai * hi + xr_scr[t]
        ni = ar * hi + ai * hr + xi_scr[t]
        xr_scr[t] = nr
        xi_scr[t] = ni
        nl = la_scr[t] * hl + lb_scr[t]
        lb_scr[t] = nl
        return nr, ni, nl

    hr, hi, hl = lax.fori_loop(0, Tc, step, (hs_scr[0], hs_scr[1], ls_scr[...]), unroll=unroll)
    hs_scr[0] = hr
    hs_scr[1] = hi
    ls_scr[...] = hl

    hrb = xr_scr[...].reshape(R, S5_LANES).astype(BF16)
    hib = xi_scr[...].reshape(R, S5_LANES).astype(BF16)
    ys = []
    for j in range(S5_BLOCKS):
        sl = slice(half * j, half * (j + 1))
        ys.append(_dot(hrb[:, sl], cre_ref[j]) - _dot(hib[:, sl], cim_ref[j]))
    y = jnp.concatenate(ys, axis=-1) + _row(d_ref, l) * u
    z = _gelu(y)
    o_s5 = z * _sigmoid(_dot(z.astype(BF16), gw_ref[...].astype(BF16)) + _row(gb_ref, l))
    o_lru = lb_scr[...].reshape(R, LRU_WIDTH) * _gelu(gate_in)
    o_ref[:, :, 0:S5_WIDTH] = o_s5.reshape(Tc, B, S5_WIDTH)
    o_ref[:, :, S5_WIDTH:SL_OUT] = o_lru.reshape(Tc, B, LRU_WIDTH)

    @pl.when(c == pl.num_programs(0) - 1)
    def _():
        hT_ref[...] = hs_scr[...]
        lT_ref[...] = ls_scr[...]
        cT_ref[...] = xc_scr[0:CONV_HIST]


def _s5lru(psl3, w, l, st, Tc):
    L, B, _ = psl3.shape
    li = st["li"]
    unroll = True
    return pl.pallas_call(
        functools.partial(_s5lru_kernel, l=l, B=B, Tc=Tc, unroll=unroll),
        grid=(L // Tc,),
        in_specs=[
            pl.BlockSpec((Tc, B, SL_IN), lambda c: (c, 0, 0)),
            _layer_spec(w["s5_ab"], l),
            _layer_spec(w["s5_win"], l),
            _layer_spec(w["s5_cre"], l),
            _layer_spec(w["s5_cim"], l),
            _full_spec(w["s5_d"]),
            _layer_spec(w["s5_glu_w"], l),
            _full_spec(w["s5_glu_b"]),
            _layer_spec(w["lru_conv_w"], l),
            _full_spec(w["lru_conv_b"]),
            _layer_spec(w["lru_wa"], l),
            _layer_spec(w["lru_wx"], l),
            _full_spec(w["lru_b_a"]),
            _full_spec(w["lru_b_x"]),
            _full_spec(w["lru_lambda"]),
            _layer_spec(st["s5"], li),
            _layer_spec(st["lru"], li),
            _layer_spec(st["conv"], li),
        ],
        out_specs=[
            pl.BlockSpec((Tc, B, SL_OUT), lambda c: (c, 0, 0)),
            pl.BlockSpec((2, B, S5_LANES), lambda c: (0, 0, 0)),
            pl.BlockSpec((B, LRU_WIDTH), lambda c: (0, 0)),
            pl.BlockSpec((CONV_HIST, B, LRU_WIDTH), lambda c: (0, 0, 0)),
        ],
        out_shape=[
            jax.ShapeDtypeStruct((L, B, SL_OUT), F32),
            jax.ShapeDtypeStruct((2, B, S5_LANES), F32),
            jax.ShapeDtypeStruct((B, LRU_WIDTH), F32),
            jax.ShapeDtypeStruct((CONV_HIST, B, LRU_WIDTH), F32),
        ],
        scratch_shapes=[
            pltpu.VMEM((2, B, S5_LANES), F32),
            pltpu.VMEM((B, LRU_WIDTH), F32),
            pltpu.VMEM((Tc + CONV_HIST, B, LRU_WIDTH), F32),
            pltpu.VMEM((Tc, B, S5_LANES), F32),
            pltpu.VMEM((Tc, B, S5_LANES), F32),
            pltpu.VMEM((Tc, B, LRU_WIDTH), F32),
            pltpu.VMEM((Tc, B, LRU_WIDTH), F32),
        ],
        compiler_params=pltpu.CompilerParams(
            dimension_semantics=("arbitrary",), vmem_limit_bytes=VMEM_LIMIT),
        name="s5_rglru_scan",
    )(psl3, w["s5_ab"], w["s5_win"], w["s5_cre"], w["s5_cim"], w["s5_d"], w["s5_glu_w"], w["s5_glu_b"],
      w["lru_conv_w"], w["lru_conv_b"], w["lru_wa"], w["lru_wx"], w["lru_b_a"], w["lru_b_x"], w["lru_lambda"],
      st["s5"], st["lru"], st["conv"])


_RWKV_PARAMS = ("rwkv_mu", "rwkv_w0", "rwkv_w_up", "rwkv_a0", "rwkv_a_up", "rwkv_g_up", "rwkv_k_k", "rwkv_k_a",
                "rwkv_r_k", "rwkv_ln_w", "rwkv_ln_b", "rwkv_ones")


def _rwkv_param_specs(w, l):
    layered = ("rwkv_w_up", "rwkv_a_up", "rwkv_g_up")
    return [_layer_spec(w[k], l) if k in layered else _full_spec(w[k]) for k in _RWKV_PARAMS]


def _rwkv_mats(prm):
    wup_ref, aup_ref, gup_ref, ones_ref = prm[2], prm[4], prm[5], prm[11]
    zpad = jnp.zeros((DECAY_RANK, RWKV_WIDTH), BF16)
    wup = jnp.concatenate([wup_ref[...].astype(BF16), zpad], axis=0)
    aup = jnp.concatenate([zpad, aup_ref[...].astype(BF16)], axis=0)
    return wup, aup, gup_ref[...].astype(BF16), ones_ref[...]


def _segsum(x, ones):
    xb = x.astype(BF16)
    return jnp.concatenate([_dot(xb[:, LANE * p:LANE * (p + 1)], ones) for p in range(RWKV_PAIRS)], axis=-1)


def _rwkv_token_math(p, prev, prm, mats, l):
    mu_ref, w0_ref, _, a0_ref, _, _, kk_ref, ka_ref, rk_ref = prm[0:9]
    wup, aup, gup, ones = mats
    W = RWKV_WIDTH
    xm = p + (prev - p) * _row(mu_ref, l)
    r = xm[:, 0:W]
    k = xm[:, W:2 * W]
    v = xm[:, 2 * W:3 * W]
    xwa = xm[:, 3 * W:3 * W + DECAY_RANK + ICL_RANK]
    xg = xm[:, 3 * W + DECAY_RANK + ICL_RANK:RWKV_PROJ]
    lw = _row(w0_ref, l) + _dot(jnp.tanh(xwa).astype(BF16), wup)
    ld = -math.exp(-0.5) * _sigmoid(lw)
    a = _sigmoid(_row(a0_ref, l) + _dot(xwa.astype(BF16), aup))
    g = _dot(_sigmoid(xg).astype(BF16), gup)
    kkr = k * _row(kk_ref, l)
    kk = kkr / jnp.maximum(jnp.sqrt(_segsum(kkr * kkr, ones)), 1e-12)
    k2 = k * (1.0 + (a - 1.0) * _row(ka_ref, l))
    bonus = _segsum(r * k2 * _row(rk_ref, l), ones) * v
    return r, k2, v, kk, a, ld, g, bonus


def _rwkv_out(y, bonus, g, prm, ones, l):
    lnw_ref, lnb_ref = prm[9], prm[10]
    inv = 1.0 / RWKV_HEAD
    mean = _segsum(y, ones) * inv
    d = y - mean
    var = _segsum(d * d, ones) * inv
    yn = d * lax.rsqrt(var + RWKV_LN_EPS) * _row(lnw_ref, l) + _row(lnb_ref, l)
    return (yn + bonus) * g


RWKV_GROUP_HEADS = 4
RWKV_GROUP_LANES = RWKV_GROUP_HEADS * RWKV_HEAD


def _rwkv_groups(B):
    full = [[(b, 0, RWKV_GROUP_LANES)] for b in range(B)]
    rest = RWKV_WIDTH - RWKV_GROUP_LANES
    tail = [[(b, RWKV_GROUP_LANES, rest), (b + 1, RWKV_GROUP_LANES, rest)] for b in range(0, B, 2)]
    return full + tail


def _rwkv_chunk_kernel(*refs, l, B, Tc):
    n_prm = len(_RWKV_PARAMS)
    p_ref, s0_ref, sh0_ref = refs[0:3]
    prm = refs[3:3 + n_prm]
    tri_ref, o_ref, sT_ref = refs[3 + n_prm:6 + n_prm]
    st_scr, xs_scr, y_scr, pt_scr = refs[6 + n_prm:10 + n_prm]
    src = dict(zip(("a", "b", "k", "r", "v", "bp", "kp"), refs[10 + n_prm:]))
    c = pl.program_id(0)
    R = B * Tc
    W = RWKV_WIDTH
    GL = RWKV_GROUP_LANES
    GH = RWKV_GROUP_HEADS
    carry_row = SUBLANE - 1

    @pl.when(c == 0)
    def _():
        st_scr[...] = s0_ref[...]
        xs_scr[:, carry_row:SUBLANE, :] = sh0_ref[...]

    p = p_ref[...]
    xs_scr[:, SUBLANE:SUBLANE + Tc, :] = p
    prev = xs_scr[:, carry_row:carry_row + Tc, :]
    xs_scr[:, carry_row:SUBLANE, :] = p[:, Tc - 1:Tc, :]
    mats = _rwkv_mats(prm)
    r, k2, v, kk, a, ld, g, bonus = _rwkv_token_math(
        p.reshape(R, RWKV_PROJ), prev.reshape(R, RWKV_PROJ), prm, mats, l)

    h1, h2, h3 = _split3(ld)
    tri = tri_ref[...]
    cls = []
    for b in range(B):
        rows = slice(b * Tc, (b + 1) * Tc)
        cls.append(_dot(tri, h1[rows]) + _dot(tri, h2[rows]) + _dot(tri, h3[rows]))
    cl = jnp.concatenate(cls, axis=0)
    pt = jnp.exp(cl.reshape(B, Tc, W)[:, Tc - 1:Tc, :])
    inv_p = jnp.exp(-cl)
    to_end = (pt * inv_p.reshape(B, Tc, W)).reshape(R, W)
    beta = kk * a
    src["a"][...] = (-kk * jnp.exp(cl - ld)).astype(BF16)
    src["b"][...] = (beta * inv_p).astype(BF16)
    src["k"][...] = (k2 * inv_p).astype(BF16)
    src["r"][...] = (r * jnp.exp(cl)).astype(BF16)
    src["v"][...] = v.astype(BF16)
    src["bp"][...] = (beta * to_end).astype(BF16)
    src["kp"][...] = (k2 * to_end).astype(BF16)
    pt_scr[...] = pt

    assert Tc & (Tc - 1) == 0
    n_doubling = max(1, (Tc - 1).bit_length())
    groups = _rwkv_groups(B)

    def block_masks(blk):
        lane = lax.broadcasted_iota(jnp.int32, (1, GH * blk), 1)
        return [((lane >= blk * h) & (lane < blk * (h + 1))).astype(BF16) for h in range(GH)]

    head_masks = block_masks(RWKV_HEAD)
    time_masks = block_masks(Tc)

    def stack(x, masks):
        return jnp.concatenate([x * m for m in masks], axis=0)

    ti = lax.broadcasted_iota(jnp.int32, (Tc, GH * Tc), 0)
    si = lax.broadcasted_iota(jnp.int32, (Tc, GH * Tc), 1) & (Tc - 1)
    strict = (si < ti).astype(BF16)
    incl = (si <= ti).astype(BF16)
    same_head = ((lax.broadcasted_iota(jnp.int32, (GL, GL), 0) // RWKV_HEAD)
                 == (lax.broadcasted_iota(jnp.int32, (GL, GL), 1) // RWKV_HEAD)).astype(F32)

    def operand(name, grp):
        parts = [src[name][b * Tc:(b + 1) * Tc, lo:lo + n] for b, lo, n in grp]
        return parts[0] if len(parts) == 1 else jnp.concatenate(parts, axis=-1)

    v4s, gss, ns, wrbs, wrks, xs = [], [], [], [], [], []
    for gi, grp in enumerate(groups):
        lhs = jnp.concatenate([operand("a", grp), operand("r", grp)], axis=0)
        v4 = stack(operand("v", grp), head_masks)
        gb = _dot_nt(lhs, stack(operand("b", grp), head_masks)).astype(BF16)
        gk = _dot_nt(lhs, stack(operand("k", grp), head_masks)).astype(BF16)
        gs = _dot_nt(lhs, st_scr[gi].astype(BF16))
        v4s.append(v4)
        gss.append(gs[Tc:2 * Tc])
        ns.append(gb[0:Tc] * strict)
        wrbs.append(gb[Tc:2 * Tc] * incl)
        wrks.append(gk[Tc:2 * Tc] * incl)
        xs.append(gs[0:Tc] + _dot(gk[0:Tc] * strict, v4))
    for j in range(n_doubling):
        for i in range(len(groups)):
            xs[i] = xs[i] + _dot(ns[i], stack(xs[i].astype(BF16), head_masks))
        if j + 1 < n_doubling:
            for i in range(len(groups)):
                ns[i] = _dot(ns[i], stack(ns[i], time_masks)).astype(BF16)
    for gi, grp in enumerate(groups):
        sab = xs[gi].astype(BF16)
        y = gss[gi] + _dot(wrbs[gi], stack(sab, head_masks)) + _dot(wrks[gi], v4s[gi])
        lane0 = 0
        for b, lo, n in grp:
            y_scr[b * Tc:(b + 1) * Tc, lo:lo + n] = y[:, lane0:lane0 + n]
            lane0 += n
        upd = _dot_tn(jnp.concatenate([sab, operand("v", grp)], axis=0),
                      jnp.concatenate([operand("bp", grp), operand("kp", grp)], axis=0))
        ptg = [pt_scr[b, :, lo:lo + n] for b, lo, n in grp]
        ptg = ptg[0] if len(ptg) == 1 else jnp.concatenate(ptg, axis=-1)
        st_scr[gi] = st_scr[gi] * ptg + upd * same_head

    out = _rwkv_out(y_scr[...], bonus, g, prm, mats[3], l)
    o_ref[...] = out.reshape(B, Tc, W)

    @pl.when(c == pl.num_programs(0) - 1)
    def _():
        sT_ref[...] = st_scr[...]


def _rwkv_chunk(prw, w, l, s0, sh0, Tc):
    B, L, _ = prw.shape
    R = B * Tc
    W = RWKV_WIDTH
    GL = RWKV_GROUP_LANES
    NG = len(_rwkv_groups(B))
    tri = jnp.tril(jnp.ones((Tc, Tc), F32)).astype(BF16)
    return pl.pallas_call(
        functools.partial(_rwkv_chunk_kernel, l=l, B=B, Tc=Tc),
        grid=(L // Tc,),
        in_specs=[
            pl.BlockSpec((B, Tc, RWKV_PROJ), lambda c: (0, c, 0)),
            _full_spec(s0),
            _full_spec(sh0),
        ] + _rwkv_param_specs(w, l) + [_full_spec(tri)],
        out_specs=[
            pl.BlockSpec((B, Tc, W), lambda c: (0, c, 0)),
            pl.BlockSpec((NG, GL, GL), lambda c: (0, 0, 0)),
        ],
        out_shape=[
            jax.ShapeDtypeStruct((B, L, W), F32),
            jax.ShapeDtypeStruct((NG, GL, GL), F32),
        ],
        scratch_shapes=[
            pltpu.VMEM((NG, GL, GL), F32),
            pltpu.VMEM((B, Tc + SUBLANE, RWKV_PROJ), F32),
            pltpu.VMEM((R, W), F32),
            pltpu.VMEM((B, 1, W), F32),
        ] + [pltpu.VMEM((R, W), BF16)] * 7,
        compiler_params=pltpu.CompilerParams(
            dimension_semantics=("arbitrary",), vmem_limit_bytes=VMEM_LIMIT),
        name="rwkv7_chunked",
    )(prw, s0, sh0, *[w[k] for k in _RWKV_PARAMS], tri)


def _rwkv_step_kernel(*refs, l):
    n_prm = len(_RWKV_PARAMS)
    p_ref, sh_ref, s_ref = refs[0:3]
    prm = refs[3:3 + n_prm]
    o_ref, sn_ref = refs[3 + n_prm:5 + n_prm]
    nkk_t, w_t, beta_t, v_t, k2_t, r_t, y_t, g_scr, bonus_scr = refs[5 + n_prm:]
    h = pl.program_id(0)
    N = RWKV_HEAD

    @pl.when(h == 0)
    def _():
        r, k2, v, kk, a, ld, g, bonus = _rwkv_token_math(p_ref[...], sh_ref[...], prm, _rwkv_mats(prm), l)
        nkk_t[...] = (-kk).T
        w_t[...] = jnp.exp(ld).T
        beta_t[...] = (kk * a).T
        v_t[...] = v.T
        k2_t[...] = k2.T
        r_t[...] = r.T
        g_scr[...] = g
        bonus_scr[...] = bonus

    rows = pl.ds(pl.multiple_of(h * N, N), N)
    s = s_ref[...]
    sa = jnp.sum(s * nkk_t[rows, :][None], axis=1)
    sn = s * w_t[rows, :][None] + sa[:, None, :] * beta_t[rows, :][None] + v_t[rows, :][:, None, :] * k2_t[rows, :][None]
    sn_ref[...] = sn
    y_t[rows, :] = jnp.sum(sn * r_t[rows, :][None], axis=1)

    @pl.when(h == pl.num_programs(0) - 1)
    def _():
        o_ref[...] = _rwkv_out(y_t[...].T, bonus_scr[...], g_scr[...], prm, prm[11][...], l)


def _rwkv_step(prw, w, l, s_all, sh_all):
    B = prw.shape[0]
    W = RWKV_WIDTH
    N = RWKV_HEAD
    return pl.pallas_call(
        functools.partial(_rwkv_step_kernel, l=l),
        grid=(RWKV_HEADS,),
        in_specs=[
            _full_spec(prw),
            _layer_spec(sh_all, l),
            pl.BlockSpec((None, None, N, N, B), lambda h: (l, h, 0, 0, 0)),
        ] + _rwkv_param_specs(w, l),
        out_specs=[
            pl.BlockSpec((B, W), lambda h: (0, 0)),
            pl.BlockSpec((None, N, N, B), lambda h: (h, 0, 0, 0)),
        ],
        out_shape=[
            jax.ShapeDtypeStruct((B, W), F32),
            jax.ShapeDtypeStruct((RWKV_HEADS, N, N, B), F32),
        ],
        scratch_shapes=[pltpu.VMEM((W, B), F32)] * 7 + [pltpu.VMEM((B, W), F32)] * 2,
        compiler_params=pltpu.CompilerParams(
            dimension_semantics=("arbitrary",), vmem_limit_bytes=VMEM_LIMIT),
        name="rwkv7_step",
    )(prw, sh_all, s_all, *[w[k] for k in _RWKV_PARAMS])


def _block_diag(blocks):
    *lead, n, r, c = blocks.shape
    eye = jnp.eye(n, dtype=blocks.dtype)
    out = blocks[..., :, :, None, :] * eye[:, None, :, None]
    return out.reshape(*lead, n * r, n * c)


def _prepare_weights(a):
    G, C, N = S5_GROUPS, S5_GROUP, S5_STATE
    abr, abi, bbr, bbi = _s5_prep(a["s5_lambda_re"], a["s5_lambda_im"], a["s5_log_dt"], a["s5_b_re"], a["s5_b_im"])

    def in_side(bb):
        return _block_diag(bb.reshape(DEPTH, S5_BLOCKS, S5_GROUPS_PER_BLOCK, C, N))

    def out_side(cc):
        ct = jnp.swapaxes(cc, 2, 3)
        return _block_diag(ct.reshape(DEPTH, S5_BLOCKS, S5_GROUPS_PER_BLOCK, N, C))

    w = {k: a[k] for k in (
        "s5_d", "s5_glu_w", "s5_glu_b", "lru_conv_w", "lru_conv_b", "lru_b_a", "lru_b_x", "lru_lambda",
        "rwkv_mu", "rwkv_w0", "rwkv_w_up", "rwkv_a0", "rwkv_a_up", "rwkv_g_up", "rwkv_k_k", "rwkv_k_a",
        "rwkv_ln_w", "rwkv_ln_b")}
    w.update({
        "n1": a["ffn1_norm"], "nm": a["mix_norm"], "n2": a["ffn2_norm"],
        "fnorm": a["final_norm"].reshape(1, D_MODEL),
        "wg1": a["ffn1_w_gate"].astype(BF16), "wu1": a["ffn1_w_up"].astype(BF16),
        "wd1": a["ffn1_w_down"].astype(BF16), "win": a["w_in"].astype(BF16), "wout": a["w_out"].astype(BF16),
        "wg2": a["ffn2_w_gate"].astype(BF16), "wu2": a["ffn2_w_up"].astype(BF16),
        "wd2": a["ffn2_w_down"].astype(BF16),
        "s5_ab": jnp.stack([abr.reshape(DEPTH, S5_LANES), abi.reshape(DEPTH, S5_LANES)], axis=1),
        "s5_win": jnp.concatenate([in_side(bbr), in_side(bbi)], axis=-1).astype(BF16),
        "s5_cre": out_side(a["s5_c_re"]).astype(BF16),
        "s5_cim": out_side(a["s5_c_im"]).astype(BF16),
        "lru_wa": _block_diag(a["lru_w_a"]).astype(BF16),
        "lru_wx": _block_diag(a["lru_w_x"]).astype(BF16),
        "rwkv_r_k": a["rwkv_r_k"].reshape(DEPTH, RWKV_WIDTH),
        "rwkv_ones": _block_diag(jnp.ones((2, RWKV_HEAD, RWKV_HEAD), F32)).astype(BF16),
    })
    return w


def _groups_to_heads(sg, B):
    N = RWKV_HEAD
    diag = jnp.stack([sg[:, N * i:N * (i + 1), N * i:N * (i + 1)] for i in range(RWKV_GROUP_HEADS)], axis=1)
    first = diag[:B]
    rest = diag[B:].reshape(B, RWKV_HEADS - RWKV_GROUP_HEADS, N, N)
    return jnp.concatenate([first, rest], axis=1)


def _seq_mixers(prw, psl3, st, w, l, Tc):
    L = prw.shape[1]
    osl, hT, lT, cT = _s5lru(psl3, w, l, st, Tc)
    orw, sT = _rwkv_chunk(prw, w, l, st["rwkv"], st["shift"], Tc)
    new = {"s5": hT[None], "lru": lT[None], "conv": cT[None], "li": 0,
           "rwkv": sT, "shift": prw[:, L - 1:L, :]}
    return osl, orw, new


def kernel(x_prompt, x_sample, state_s5_re, state_s5_im, state_rwkv, state_rwkv_shift, state_lru, state_lru_conv, meta_tokens, ffn1_norm, ffn1_w_gate, ffn1_w_up, ffn1_w_down, mix_norm, w_in, s5_lambda_re, s5_lambda_im, s5_log_dt, s5_b_re, s5_b_im, s5_c_re, s5_c_im, s5_d, s5_glu_w, s5_glu_b, rwkv_mu, rwkv_w0, rwkv_w_up, rwkv_a0, rwkv_a_up, rwkv_g_up, rwkv_k_k, rwkv_k_a, rwkv_r_k, rwkv_ln_w, rwkv_ln_b, lru_conv_w, lru_conv_b, lru_w_a, lru_b_a, lru_w_x, lru_b_x, lru_lambda, w_out, ffn2_norm, ffn2_w_gate, ffn2_w_up, ffn2_w_down, final_norm):
    w = _prepare_weights(dict(locals()))
    BP, SEQ, _ = x_prompt.shape
    BS = x_sample.shape[0]
    NS = N_META + BS

    zero = {
        "s5": jnp.zeros((1, 2, BP, S5_LANES), F32),
        "lru": jnp.zeros((1, BP, LRU_WIDTH), F32),
        "conv": jnp.zeros((1, CONV_HIST, BP, LRU_WIDTH), F32),
        "li": 0,
        "rwkv": jnp.zeros((len(_rwkv_groups(BP)), RWKV_GROUP_LANES, RWKV_GROUP_LANES), F32),
        "shift": jnp.zeros((BP, 1, RWKV_PROJ), F32),
    }
    sample_in = {
        "s5": jnp.stack([state_s5_re.reshape(DEPTH, BS, S5_LANES), state_s5_im.reshape(DEPTH, BS, S5_LANES)], axis=1),
        "lru": state_lru,
        "conv": jnp.swapaxes(state_lru_conv, 1, 2),
    }

    rwkv_in = jnp.transpose(state_rwkv, (0, 2, 3, 4, 1))

    xp = x_prompt
    xs = jnp.concatenate([meta_tokens.astype(F32), x_sample.reshape(BS, D_MODEL)], axis=0)[None]
    p_states = []
    s_states = []
    for l in range(DEPTH):
        final = l == DEPTH - 1
        hs, prw_s, psl_s = _k1(xs, w, l, NS)
        prw_m = jnp.broadcast_to(prw_s[:, :N_META], (BP, N_META, RWKV_PROJ))
        psl_m = jnp.broadcast_to(psl_s[:N_META, None, :], (N_META, BP, SL_IN))
        osl_m, orw_m, st_meta = _seq_mixers(prw_m, psl_m, zero, w, l, N_META)

        st_l = dict(sample_in, li=l)
        osl_s, hT, lT, cT = _s5lru(psl_s[N_META:].reshape(1, BS, SL_IN), w, l, st_l, 1)
        orw_s, sT = _rwkv_step(prw_s[0, N_META:], w, l, rwkv_in, state_rwkv_shift)
        s_states.append({"s5": hT, "lru": lT, "conv": cT, "rwkv": sT, "shift": prw_s[0, N_META:]})

        hp, prw_p, psl_p = _k1(xp, w, l, PROMPT_TOKEN_TILE)
        osl_p, orw_p, st_p = _seq_mixers(prw_p, psl_p.reshape(SEQ, BP, SL_IN), st_meta, w, l, PROMPT_CHUNK)
        p_states.append(st_p)
        xp = _k3(hp, osl_p.reshape(SEQ, BP * SL_OUT), orw_p, w, l, PROMPT_TOKEN_TILE, final)

        if final:
            xs = _k3(hs[:, N_META:], osl_s.reshape(BS, SL_OUT), orw_s[None], w, l, BS, True)
        else:
            osl = jnp.concatenate([osl_m[:, 0, :], osl_s.reshape(BS, SL_OUT)], axis=0)
            orw = jnp.concatenate([orw_m[0], orw_s], axis=0)[None]
            xs = _k3(hs, osl, orw, w, l, NS, False)

    p_out = (
        jnp.stack([s["s5"][0, 0].reshape(BP, S5_GROUPS, S5_STATE) for s in p_states]),
        jnp.stack([s["s5"][0, 1].reshape(BP, S5_GROUPS, S5_STATE) for s in p_states]),
        jnp.stack([_groups_to_heads(s["rwkv"], BP) for s in p_states]),
        jnp.stack([s["shift"][:, 0, :] for s in p_states]),
        jnp.stack([s["lru"][0] for s in p_states]),
        jnp.stack([jnp.swapaxes(s["conv"][0], 0, 1) for s in p_states]),
    )
    s_out = (
        jnp.stack([s["s5"][0].reshape(BS, S5_GROUPS, S5_STATE) for s in s_states]),
        jnp.stack([s["s5"][1].reshape(BS, S5_GROUPS, S5_STATE) for s in s_states]),
        jnp.transpose(jnp.stack([s["rwkv"] for s in s_states]), (0, 4, 1, 2, 3)),
        jnp.stack([s["shift"] for s in s_states]),
        jnp.stack([s["lru"] for s in s_states]),
        jnp.stack([jnp.swapaxes(s["conv"], 0, 1) for s in s_states]),
    )
    return (xp, xs.reshape(BS, 1, D_MODEL)) + p_out + s_out
```

```python
import functools
import math

import jax
import jax.numpy as jnp
from jax import lax
from jax.experimental import pallas as pl
from jax.experimental.pallas import tpu as pltpu

F32 = jnp.float32
BF16 = jnp.bfloat16

D_MODEL = 1024
DEPTH = 2
N_META = 16
D_FF = 2816
NORM_EPS = 1e-6
FFN_RES_SCALE = 0.5
S5_WIDTH = 384
S5_GROUP = 16
S5_GROUPS = 24
S5_STATE = 64
S5_LANES = S5_GROUPS * S5_STATE
RWKV_HEAD = 64
RWKV_WIDTH = 384
RWKV_HEADS = 6
RWKV_PAIRS = 3
DECAY_RANK = 64
ICL_RANK = 64
GATE_RANK = 128
RWKV_PROJ = 1408
RWKV_LN_EPS = 64e-5
LRU_WIDTH = 256
LRU_BLOCKS = 4
CONV_WIDTH = 4
CONV_HIST = CONV_WIDTH - 1
LRU_C = 8.0
D_IN = 2304
SL_IN = S5_WIDTH + 2 * LRU_WIDTH
SL_OUT = S5_WIDTH + LRU_WIDTH
LANE = 128
SUBLANE = 8
S5_GROUPS_PER_BLOCK = LANE // S5_GROUP
S5_BLOCKS = S5_GROUPS // S5_GROUPS_PER_BLOCK
S5_BLOCK_STATES = S5_GROUPS_PER_BLOCK * S5_STATE

V7X_VMEM_BYTES = 64 * 2 ** 20
VMEM_LIMIT = (V7X_VMEM_BYTES * 7) // 8

PROMPT_TOKEN_TILE = 512
ROW_SUBTILE = 256
PROMPT_CHUNK = 64


def _dot(a, b):
    return jnp.dot(a, b, preferred_element_type=F32)


def _dot_nt(a, b):
    bt = jnp.transpose(b.astype(F32)).astype(BF16)
    return jnp.dot(a, bt, preferred_element_type=F32)


def _dot_tn(a, b):
    return lax.dot_general(a, b, (((0,), (0,)), ((), ())), preferred_element_type=F32)


def _split3(x):
    hi = x.astype(BF16)
    r = x - hi.astype(F32)
    mid = r.astype(BF16)
    lo = (r - mid.astype(F32)).astype(BF16)
    return hi, mid, lo


def _sigmoid(x):
    return 0.5 * (jnp.tanh(0.5 * x) + 1.0)


def _gelu(x):
    c = math.sqrt(2.0 / math.pi)
    return 0.5 * x * (1.0 + jnp.tanh(c * (x + 0.044715 * (x * x * x))))


def _neg_expm1(z):
    t = jnp.tanh(0.5 * z)
    return -2.0 * t / (1.0 - t)


def _softplus(x):
    return jnp.maximum(x, 0.0) + jnp.log1p(jnp.exp(-jnp.abs(x)))


def _rms(x, g):
    return x * lax.rsqrt(jnp.mean(x * x, axis=-1, keepdims=True) + NORM_EPS) * g


def _swiglu_res(x, g_norm, wg_ref, wu_ref, wd_ref):
    xn = _rms(x, g_norm).astype(BF16)
    yield
    g = _dot(xn, wg_ref[...])
    u = _dot(xn, wu_ref[...])
    yield
    a = (g * _sigmoid(g) * u).astype(BF16)
    yield
    d = _dot(a, wd_ref[...])
    yield
    return x + FFN_RES_SCALE * d


def _staggered(gens):
    pending = list(gens)
    active = []
    while pending or active:
        if pending:
            active.append(pending.pop(0))
        for g in list(active):
            try:
                next(g)
            except StopIteration:
                active.remove(g)


def _row_tiles(n_rows):
    sub = ROW_SUBTILE if n_rows % ROW_SUBTILE == 0 else n_rows
    return [slice(r, r + sub) for r in range(0, n_rows, sub)]


def _full_spec(arr):
    nd = arr.ndim
    return pl.BlockSpec(arr.shape, lambda *_: (0,) * nd, pipeline_mode=pl.Buffered(1))


def _layer_spec(arr, l):
    nd = arr.ndim - 1
    return pl.BlockSpec((None,) + arr.shape[1:], lambda *_: (l,) + (0,) * nd, pipeline_mode=pl.Buffered(1))


def _row(ref, l):
    return ref[l:l + 1, :]


def _k1_kernel(x_ref, n1_ref, wg_ref, wu_ref, wd_ref, nm_ref, win_ref, h_ref, prw_ref, psl_ref, *, l):
    def rows_pipeline(rows):
        h = yield from _swiglu_res(x_ref[rows, :], _row(n1_ref, l), wg_ref, wu_ref, wd_ref)
        h_ref[rows, :] = h
        hn = _rms(h, _row(nm_ref, l)).astype(BF16)
        yield
        psl_ref[rows, 0:S5_WIDTH] = _dot(hn, win_ref[:, 0:S5_WIDTH])
        prw_ref[rows, :] = _dot(hn, win_ref[:, S5_WIDTH:S5_WIDTH + RWKV_PROJ])
        psl_ref[rows, S5_WIDTH:SL_IN] = _dot(hn, win_ref[:, S5_WIDTH + RWKV_PROJ:D_IN])

    _staggered([rows_pipeline(rows) for rows in _row_tiles(x_ref.shape[0])])


def _k1(x, w, l, tl):
    B, L, _ = x.shape
    return pl.pallas_call(
        functools.partial(_k1_kernel, l=l),
        grid=(B, L // tl),
        in_specs=[
            pl.BlockSpec((None, tl, D_MODEL), lambda b, i: (b, i, 0)),
            _full_spec(w["n1"]),
            _layer_spec(w["wg1"], l),
            _layer_spec(w["wu1"], l),
            _layer_spec(w["wd1"], l),
            _full_spec(w["nm"]),
            _layer_spec(w["win"], l),
        ],
        out_specs=[
            pl.BlockSpec((None, tl, D_MODEL), lambda b, i: (b, i, 0)),
            pl.BlockSpec((None, tl, RWKV_PROJ), lambda b, i: (b, i, 0)),
            pl.BlockSpec((tl, SL_IN), lambda b, i: (i, b)),
        ],
        out_shape=[
            jax.ShapeDtypeStruct((B, L, D_MODEL), F32),
            jax.ShapeDtypeStruct((B, L, RWKV_PROJ), F32),
            jax.ShapeDtypeStruct((L, B * SL_IN), F32),
        ],
        compiler_params=pltpu.CompilerParams(
            dimension_semantics=("parallel", "parallel"), vmem_limit_bytes=VMEM_LIMIT),
        name="ffn1_inproj",
    )(x, w["n1"], w["wg1"], w["wu1"], w["wd1"], w["nm"], w["win"])


def _k3_kernel(h_ref, osl_ref, orw_ref, wout_ref, n2_ref, wg_ref, wu_ref, wd_ref, fn_ref, o_ref, *, l, final):
    def rows_pipeline(rows):
        osl = osl_ref[rows, :]
        mix = (_dot(osl[:, 0:S5_WIDTH].astype(BF16), wout_ref[0:S5_WIDTH, :])
               + _dot(orw_ref[rows, :].astype(BF16), wout_ref[S5_WIDTH:S5_WIDTH + RWKV_WIDTH, :])
               + _dot(osl[:, S5_WIDTH:SL_OUT].astype(BF16), wout_ref[S5_WIDTH + RWKV_WIDTH:D_MODEL, :]))
        yield
        h3 = yield from _swiglu_res(h_ref[rows, :] + mix, _row(n2_ref, l), wg_ref, wu_ref, wd_ref)
        if final:
            h3 = _rms(h3, fn_ref[...])
        o_ref[rows, :] = h3

    _staggered([rows_pipeline(rows) for rows in _row_tiles(h_ref.shape[0])])


def _k3(h, osl, orw, w, l, tl, final):
    B, L, _ = h.shape
    return pl.pallas_call(
        functools.partial(_k3_kernel, l=l, final=final),
        grid=(B, L // tl),
        in_specs=[
            pl.BlockSpec((None, tl, D_MODEL), lambda b, i: (b, i, 0)),
            pl.BlockSpec((tl, SL_OUT), lambda b, i: (i, b)),
            pl.BlockSpec((None, tl, RWKV_WIDTH), lambda b, i: (b, i, 0)),
            _layer_spec(w["wout"], l),
            _full_spec(w["n2"]),
            _layer_spec(w["wg2"], l),
            _layer_spec(w["wu2"], l),
            _layer_spec(w["wd2"], l),
            _full_spec(w["fnorm"]),
        ],
        out_specs=pl.BlockSpec((None, tl, D_MODEL), lambda b, i: (b, i, 0)),
        out_shape=jax.ShapeDtypeStruct((B, L, D_MODEL), F32),
        compiler_params=pltpu.CompilerParams(
            dimension_semantics=("parallel", "parallel"), vmem_limit_bytes=VMEM_LIMIT),
        name="outproj_ffn2",
    )(h, osl, orw, w["wout"], w["n2"], w["wg2"], w["wu2"], w["wd2"], w["fnorm"])


def _s5_prep_kernel(lr_ref, li_ref, ldt_ref, br_ref, bi_ref, abr_ref, abi_ref, bbr_ref, bbi_ref):
    lr = lr_ref[...]
    li = li_ref[...]
    dt = jnp.exp(ldt_ref[...])
    mag = jnp.exp(lr * dt)
    ab_re = mag * jnp.cos(li * dt)
    ab_im = mag * jnp.sin(li * dt)
    den = lr * lr + li * li
    f_re = ((ab_re - 1.0) * lr + ab_im * li) / den
    f_im = (ab_im * lr - (ab_re - 1.0) * li) / den
    abr_ref[...] = ab_re
    abi_ref[...] = ab_im
    br = br_ref[...]
    bi = bi_ref[...]
    bbr_ref[...] = f_re[:, :, None, :] * br - f_im[:, :, None, :] * bi
    bbi_ref[...] = f_re[:, :, None, :] * bi + f_im[:, :, None, :] * br


def _s5_prep(lam_re, lam_im, log_dt, b_re, b_im):
    G, N, C = S5_GROUPS, S5_STATE, S5_GROUP
    ldt = jnp.broadcast_to(log_dt[:, :, None], (DEPTH, G, N))
    br = jnp.swapaxes(b_re, 2, 3)
    bi = jnp.swapaxes(b_im, 2, 3)
    return pl.pallas_call(
        _s5_prep_kernel,
        out_shape=[
            jax.ShapeDtypeStruct((DEPTH, G, N), F32),
            jax.ShapeDtypeStruct((DEPTH, G, N), F32),
            jax.ShapeDtypeStruct((DEPTH, G, C, N), F32),
            jax.ShapeDtypeStruct((DEPTH, G, C, N), F32),
        ],
        name="s5_discretise",
    )(lam_re, lam_im, ldt, br, bi)


def _s5lru_kernel(psl_ref, ab_ref, win_ref, cre_ref, cim_ref, d_ref, gw_ref, gb_ref,
                  cw_ref, cb_ref, wa_ref, wx_ref, ba_ref, bx_ref, lam_ref,
                  h0_ref, l0_ref, c0_ref,
                  o_ref, hT_ref, lT_ref, cT_ref,
                  hs_scr, ls_scr, xc_scr, xr_scr, xi_scr, la_scr, lb_scr, *, l, B, Tc, unroll):
    c = pl.program_id(0)
    R = Tc * B

    @pl.when(c == 0)
    def _():
        hs_scr[...] = h0_ref[...]
        ls_scr[...] = l0_ref[...]
        xc_scr[0:CONV_HIST] = c0_ref[...]

    psl = psl_ref[...]
    u = psl[:, :, 0:S5_WIDTH].reshape(R, S5_WIDTH)
    gate_in = psl[:, :, S5_WIDTH + LRU_WIDTH:SL_IN].reshape(R, LRU_WIDTH)

    ub = u.astype(BF16)
    half = S5_BLOCK_STATES
    for j in range(S5_BLOCKS):
        x = _dot(ub[:, LANE * j:LANE * (j + 1)], win_ref[j])
        xr_scr[:, :, half * j:half * (j + 1)] = x[:, 0:half].reshape(Tc, B, half)
        xi_scr[:, :, half * j:half * (j + 1)] = x[:, half:2 * half].reshape(Tc, B, half)

    xc_scr[CONV_HIST:CONV_HIST + Tc] = psl[:, :, S5_WIDTH:S5_WIDTH + LRU_WIDTH]
    cw = cw_ref[...]
    xc = _row(cb_ref, l) + xc_scr[0:Tc] * cw[0:1]
    for j in range(1, CONV_WIDTH):
        xc = xc + xc_scr[j:j + Tc] * cw[j:j + 1]
    new_hist = xc_scr[Tc:Tc + CONV_HIST]
    xc_scr[0:CONV_HIST] = new_hist
    xc2 = xc.reshape(R, LRU_WIDTH)
    xcb = xc2.astype(BF16)
    gate_a = _sigmoid(_dot(xcb, wa_ref[...]) + _row(ba_ref, l))
    gate_x = _sigmoid(_dot(xcb, wx_ref[...]) + _row(bx_ref, l))
    log_a = LRU_C * gate_a * (-_softplus(-_row(lam_ref, l)))
    la_scr[...] = jnp.exp(log_a).reshape(Tc, B, LRU_WIDTH)
    lb_scr[...] = (jnp.sqrt(_neg_expm1(2.0 * log_a)) * (gate_x * xc2)).reshape(Tc, B, LRU_WIDTH)

    ar = jnp.broadcast_to(ab_ref[0:1, :], (B, S5_LANES))
    ai = jnp.broadcast_to(ab_ref[1:2, :], (B, S5_LANES))

    def step(t, carry):
        hr, hi, hl = carry
        nr = ar * hr - ai * hi + xr_scr[t]
        ni = ar * hi + ai * hr + xi_scr[t]
        xr_scr[t] = nr
        xi_scr[t] = ni
        nl = la_scr[t] * hl + lb_scr[t]
        lb_scr[t] = nl
        return nr, ni, nl

    hr, hi, hl = lax.fori_loop(0, Tc, step, (hs_scr[0], hs_scr[1], ls_scr[...]), unroll=unroll)
    hs_scr[0] = hr
    hs_scr[1] = hi
    ls_scr[...] = hl

    hrb = xr_scr[...].reshape(R, S5_LANES).astype(BF16)
    hib = xi_scr[...].reshape(R, S5_LANES).astype(BF16)
    ys = []
    for j in range(S5_BLOCKS):
        sl = slice(half * j, half * (j + 1))
        ys.append(_dot(hrb[:, sl], cre_ref[j]) - _dot(hib[:, sl], cim_ref[j]))
    y = jnp.concatenate(ys, axis=-1) + _row(d_ref, l) * u
    z = _gelu(y)
    o_s5 = z * _sigmoid(_dot(z.astype(BF16), gw_ref[...].astype(BF16)) + _row(gb_ref, l))
    o_lru = lb_scr[...].reshape(R, LRU_WIDTH) * _gelu(gate_in)
    o_ref[:, :, 0:S5_WIDTH] = o_s5.reshape(Tc, B, S5_WIDTH)
    o_ref[:, :, S5_WIDTH:SL_OUT] = o_lru.reshape(Tc, B, LRU_WIDTH)

    @pl.when(c == pl.num_programs(0) - 1)
    def _():
        hT_ref[...] = hs_scr[...]
        lT_ref[...] = ls_scr[...]
        cT_ref[...] = xc_scr[0:CONV_HIST]


def _s5lru(psl3, w, l, st, Tc):
    L, B, _ = psl3.shape
    li = st["li"]
    unroll = True
    return pl.pallas_call(
        functools.partial(_s5lru_kernel, l=l, B=B, Tc=Tc, unroll=unroll),
        grid=(L // Tc,),
        in_specs=[
            pl.BlockSpec((Tc, B, SL_IN), lambda c: (c, 0, 0)),
            _layer_spec(w["s5_ab"], l),
            _layer_spec(w["s5_win"], l),
            _layer_spec(w["s5_cre"], l),
            _layer_spec(w["s5_cim"], l),
            _full_spec(w["s5_d"]),
            _layer_spec(w["s5_glu_w"], l),
            _full_spec(w["s5_glu_b"]),
            _layer_spec(w["lru_conv_w"], l),
            _full_spec(w["lru_conv_b"]),
            _layer_spec(w["lru_wa"], l),
            _layer_spec(w["lru_wx"], l),
            _full_spec(w["lru_b_a"]),
            _full_spec(w["lru_b_x"]),
            _full_spec(w["lru_lambda"]),
            _layer_spec(st["s5"], li),
            _layer_spec(st["lru"], li),
            _layer_spec(st["conv"], li),
        ],
        out_specs=[
            pl.BlockSpec((Tc, B, SL_OUT), lambda c: (c, 0, 0)),
            pl.BlockSpec((2, B, S5_LANES), lambda c: (0, 0, 0)),
            pl.BlockSpec((B, LRU_WIDTH), lambda c: (0, 0)),
            pl.BlockSpec((CONV_HIST, B, LRU_WIDTH), lambda c: (0, 0, 0)),
        ],
        out_shape=[
            jax.ShapeDtypeStruct((L, B, SL_OUT), F32),
            jax.ShapeDtypeStruct((2, B, S5_LANES), F32),
            jax.ShapeDtypeStruct((B, LRU_WIDTH), F32),
            jax.ShapeDtypeStruct((CONV_HIST, B, LRU_WIDTH), F32),
        ],
        scratch_shapes=[
            pltpu.VMEM((2, B, S5_LANES), F32),
            pltpu.VMEM((B, LRU_WIDTH), F32),
            pltpu.VMEM((Tc + CONV_HIST, B, LRU_WIDTH), F32),
            pltpu.VMEM((Tc, B, S5_LANES), F32),
            pltpu.VMEM((Tc, B, S5_LANES), F32),
            pltpu.VMEM((Tc, B, LRU_WIDTH), F32),
            pltpu.VMEM((Tc, B, LRU_WIDTH), F32),
        ],
        compiler_params=pltpu.CompilerParams(
            dimension_semantics=("arbitrary",), vmem_limit_bytes=VMEM_LIMIT),
        name="s5_rglru_scan",
    )(psl3, w["s5_ab"], w["s5_win"], w["s5_cre"], w["s5_cim"], w["s5_d"], w["s5_glu_w"], w["s5_glu_b"],
      w["lru_conv_w"], w["lru_conv_b"], w["lru_wa"], w["lru_wx"], w["lru_b_a"], w["lru_b_x"], w["lru_lambda"],
      st["s5"], st["lru"], st["conv"])


_RWKV_PARAMS = ("rwkv_mu", "rwkv_w0", "rwkv_w_up", "rwkv_a0", "rwkv_a_up", "rwkv_g_up", "rwkv_k_k", "rwkv_k_a",
                "rwkv_r_k", "rwkv_ln_w", "rwkv_ln_b", "rwkv_ones")


def _rwkv_param_specs(w, l):
    layered = ("rwkv_w_up", "rwkv_a_up", "rwkv_g_up")
    return [_layer_spec(w[k], l) if k in layered else _full_spec(w[k]) for k in _RWKV_PARAMS]


def _rwkv_mats(prm):
    wup_ref, aup_ref, gup_ref, ones_ref = prm[2], prm[4], prm[5], prm[11]
    zpad = jnp.zeros((DECAY_RANK, RWKV_WIDTH), BF16)
    wup = jnp.concatenate([wup_ref[...].astype(BF16), zpad], axis=0)
    aup = jnp.concatenate([zpad, aup_ref[...].astype(BF16)], axis=0)
    return wup, aup, gup_ref[...].astype(BF16), ones_ref[...]


def _segsum(x, ones):
    xb = x.astype(BF16)
    return jnp.concatenate([_dot(xb[:, LANE * p:LANE * (p + 1)], ones) for p in range(RWKV_PAIRS)], axis=-1)


def _rwkv_token_math(p, prev, prm, mats, l):
    mu_ref, w0_ref, _, a0_ref, _, _, kk_ref, ka_ref, rk_ref = prm[0:9]
    wup, aup, gup, ones = mats
    W = RWKV_WIDTH
    xm = p + (prev - p) * _row(mu_ref, l)
    r = xm[:, 0:W]
    k = xm[:, W:2 * W]
    v = xm[:, 2 * W:3 * W]
    xwa = xm[:, 3 * W:3 * W + DECAY_RANK + ICL_RANK]
    xg = xm[:, 3 * W + DECAY_RANK + ICL_RANK:RWKV_PROJ]
    lw = _row(w0_ref, l) + _dot(jnp.tanh(xwa).astype(BF16), wup)
    ld = -math.exp(-0.5) * _sigmoid(lw)
    a = _sigmoid(_row(a0_ref, l) + _dot(xwa.astype(BF16), aup))
    g = _dot(_sigmoid(xg).astype(BF16), gup)
    kkr = k * _row(kk_ref, l)
    kk = kkr * lax.rsqrt(jnp.maximum(_segsum(kkr * kkr, ones), 1e-24))
    k2 = k * (1.0 + (a - 1.0) * _row(ka_ref, l))
    bonus = _segsum(r * k2 * _row(rk_ref, l), ones) * v
    return r, k2, v, kk, a, ld, g, bonus


def _rwkv_out(y, bonus, g, prm, ones, l):
    lnw_ref, lnb_ref = prm[9], prm[10]
    inv = 1.0 / RWKV_HEAD
    mean = _segsum(y, ones) * inv
    d = y - mean
    var = _segsum(d * d, ones) * inv
    yn = d * lax.rsqrt(var + RWKV_LN_EPS) * _row(lnw_ref, l) + _row(lnb_ref, l)
    return (yn + bonus) * g


RWKV_GROUP_HEADS = 4
RWKV_GROUP_LANES = RWKV_GROUP_HEADS * RWKV_HEAD


def _rwkv_groups(B):
    full = [[(b, 0, RWKV_GROUP_LANES)] for b in range(B)]
    rest = RWKV_WIDTH - RWKV_GROUP_LANES
    tail = [[(b, RWKV_GROUP_LANES, rest), (b + 1, RWKV_GROUP_LANES, rest)] for b in range(0, B, 2)]
    return full + tail


def _rwkv_chunk_kernel(*refs, l, B, Tc):
    n_prm = len(_RWKV_PARAMS)
    p_ref, s0_ref, sh0_ref = refs[0:3]
    prm = refs[3:3 + n_prm]
    tri_ref, o_ref, sT_ref = refs[3 + n_prm:6 + n_prm]
    st_scr, xs_scr, y_scr, pt_scr = refs[6 + n_prm:10 + n_prm]
    src = dict(zip(("a", "b", "k", "r", "v", "bp", "kp"), refs[10 + n_prm:]))
    c = pl.program_id(0)
    R = B * Tc
    W = RWKV_WIDTH
    GL = RWKV_GROUP_LANES
    GH = RWKV_GROUP_HEADS
    carry_row = SUBLANE - 1

    @pl.when(c == 0)
    def _():
        st_scr[...] = s0_ref[...]
        xs_scr[:, carry_row:SUBLANE, :] = sh0_ref[...]

    ps, prevs = [], []
    for b in range(B):
        pb = p_ref[b]
        xs_scr[b, SUBLANE:SUBLANE + Tc, :] = pb
        prevs.append(xs_scr[b, carry_row:carry_row + Tc, :])
        xs_scr[b, carry_row:SUBLANE, :] = pb[Tc - 1:Tc, :]
        ps.append(pb)
    mats = _rwkv_mats(prm)
    r, k2, v, kk, a, ld, g, bonus = _rwkv_token_math(
        jnp.concatenate(ps, axis=0), jnp.concatenate(prevs, axis=0), prm, mats, l)

    h1, h2, h3 = _split3(ld)
    tri = tri_ref[...]
    cls = []
    for b in range(B):
        rows = slice(b * Tc, (b + 1) * Tc)
        cls.append(_dot(tri, h1[rows]) + _dot(tri, h2[rows]) + _dot(tri, h3[rows]))
    cl = jnp.concatenate(cls, axis=0)
    pt = jnp.exp(cl.reshape(B, Tc, W)[:, Tc - 1:Tc, :])
    inv_p = jnp.exp(-cl)
    to_end = (pt * inv_p.reshape(B, Tc, W)).reshape(R, W)
    beta = kk * a
    src["a"][...] = (-kk * jnp.exp(cl - ld)).astype(BF16)
    src["b"][...] = (beta * inv_p).astype(BF16)
    src["k"][...] = (k2 * inv_p).astype(BF16)
    src["r"][...] = (r * jnp.exp(cl)).astype(BF16)
    src["v"][...] = v.astype(BF16)
    src["bp"][...] = (beta * to_end).astype(BF16)
    src["kp"][...] = (k2 * to_end).astype(BF16)
    pt_scr[...] = pt

    assert Tc & (Tc - 1) == 0
    n_doubling = max(1, (Tc - 1).bit_length())
    groups = _rwkv_groups(B)

    def block_masks(blk):
        lane = lax.broadcasted_iota(jnp.int32, (1, GH * blk), 1)
        return [((lane >= blk * h) & (lane < blk * (h + 1))).astype(BF16) for h in range(GH)]

    head_masks = block_masks(RWKV_HEAD)
    time_masks = block_masks(Tc)

    def stack(x, masks):
        return jnp.concatenate([x * m for m in masks], axis=0)

    ti = lax.broadcasted_iota(jnp.int32, (Tc, GH * Tc), 0)
    si = lax.broadcasted_iota(jnp.int32, (Tc, GH * Tc), 1) & (Tc - 1)
    strict = (si < ti).astype(BF16)
    incl = (si <= ti).astype(BF16)
    same_head = ((lax.broadcasted_iota(jnp.int32, (GL, GL), 0) // RWKV_HEAD)
                 == (lax.broadcasted_iota(jnp.int32, (GL, GL), 1) // RWKV_HEAD)).astype(F32)

    def operand(name, grp):
        parts = [src[name][b * Tc:(b + 1) * Tc, lo:lo + n] for b, lo, n in grp]
        return parts[0] if len(parts) == 1 else jnp.concatenate(parts, axis=-1)

    v4s, gss, ns, wrbs, wrks, xs = [], [], [], [], [], []
    for gi, grp in enumerate(groups):
        lhs = jnp.concatenate([operand("a", grp), operand("r", grp)], axis=0)
        v4 = stack(operand("v", grp), head_masks)
        gb = _dot_nt(lhs, stack(operand("b", grp), head_masks)).astype(BF16)
        gk = _dot_nt(lhs, stack(operand("k", grp), head_masks)).astype(BF16)
        gs = _dot_nt(lhs, st_scr[gi].astype(BF16))
        v4s.append(v4)
        gss.append(gs[Tc:2 * Tc])
        ns.append(gb[0:Tc] * strict)
        wrbs.append(gb[Tc:2 * Tc] * incl)
        wrks.append(gk[Tc:2 * Tc] * incl)
        xs.append(gs[0:Tc] + _dot(gk[0:Tc] * strict, v4))
    for j in range(n_doubling):
        for i in range(len(groups)):
            xs[i] = xs[i] + _dot(ns[i], stack(xs[i].astype(BF16), head_masks))
        if j + 1 < n_doubling:
            for i in range(len(groups)):
                ns[i] = _dot(ns[i], stack(ns[i], time_masks)).astype(BF16)
    for gi, grp in enumerate(groups):
        sab = xs[gi].astype(BF16)
        y = gss[gi] + _dot(wrbs[gi], stack(sab, head_masks)) + _dot(wrks[gi], v4s[gi])
        lane0 = 0
        for b, lo, n in grp:
            y_scr[b * Tc:(b + 1) * Tc, lo:lo + n] = y[:, lane0:lane0 + n]
            lane0 += n
        upd = _dot_tn(jnp.concatenate([sab, operand("v", grp)], axis=0),
                      jnp.concatenate([operand("bp", grp), operand("kp", grp)], axis=0))
        ptg = [pt_scr[b, :, lo:lo + n] for b, lo, n in grp]
        ptg = ptg[0] if len(ptg) == 1 else jnp.concatenate(ptg, axis=-1)
        st_scr[gi] = st_scr[gi] * ptg + upd * same_head

    out = _rwkv_out(y_scr[...], bonus, g, prm, mats[3], l)
    for b in range(B):
        o_ref[b] = out[b * Tc:(b + 1) * Tc]

    @pl.when(c == pl.num_programs(0) - 1)
    def _():
        sT_ref[...] = st_scr[...]


def _rwkv_chunk(prw, w, l, s0, sh0, Tc):
    B, L, _ = prw.shape
    R = B * Tc
    W = RWKV_WIDTH
    GL = RWKV_GROUP_LANES
    NG = len(_rwkv_groups(B))
    tri = jnp.tril(jnp.ones((Tc, Tc), F32)).astype(BF16)
    return pl.pallas_call(
        functools.partial(_rwkv_chunk_kernel, l=l, B=B, Tc=Tc),
        grid=(L // Tc,),
        in_specs=[
            pl.BlockSpec((B, Tc, RWKV_PROJ), lambda c: (0, c, 0)),
            _full_spec(s0),
            _full_spec(sh0),
        ] + _rwkv_param_specs(w, l) + [_full_spec(tri)],
        out_specs=[
            pl.BlockSpec((B, Tc, W), lambda c: (0, c, 0)),
            pl.BlockSpec((NG, GL, GL), lambda c: (0, 0, 0)),
        ],
        out_shape=[
            jax.ShapeDtypeStruct((B, L, W), F32),
            jax.ShapeDtypeStruct((NG, GL, GL), F32),
        ],
        scratch_shapes=[
            pltpu.VMEM((NG, GL, GL), F32),
            pltpu.VMEM((B, Tc + SUBLANE, RWKV_PROJ), F32),
            pltpu.VMEM((R, W), F32),
            pltpu.VMEM((B, 1, W), F32),
        ] + [pltpu.VMEM((R, W), BF16)] * 7,
        compiler_params=pltpu.CompilerParams(
            dimension_semantics=("arbitrary",), vmem_limit_bytes=VMEM_LIMIT),
        name="rwkv7_chunked",
    )(prw, s0, sh0, *[w[k] for k in _RWKV_PARAMS], tri)


def _rwkv_step_kernel(*refs, l):
    n_prm = len(_RWKV_PARAMS)
    p_ref, sh_ref, s_ref = refs[0:3]
    prm = refs[3:3 + n_prm]
    o_ref, sn_ref = refs[3 + n_prm:5 + n_prm]
    nkk_t, w_t, beta_t, v_t, k2_t, r_t, y_t, g_scr, bonus_scr = refs[5 + n_prm:]
    h = pl.program_id(0)
    N = RWKV_HEAD

    @pl.when(h == 0)
    def _():
        r, k2, v, kk, a, ld, g, bonus = _rwkv_token_math(p_ref[...], sh_ref[...], prm, _rwkv_mats(prm), l)
        nkk_t[...] = (-kk).T
        w_t[...] = jnp.exp(ld).T
        beta_t[...] = (kk * a).T
        v_t[...] = v.T
        k2_t[...] = k2.T
        r_t[...] = r.T
        g_scr[...] = g
        bonus_scr[...] = bonus

    rows = pl.ds(pl.multiple_of(h * N, N), N)
    s = s_ref[...]
    sa = jnp.sum(s * nkk_t[rows, :][None], axis=1)
    sn = s * w_t[rows, :][None] + sa[:, None, :] * beta_t[rows, :][None] + v_t[rows, :][:, None, :] * k2_t[rows, :][None]
    sn_ref[...] = sn
    y_t[rows, :] = jnp.sum(sn * r_t[rows, :][None], axis=1)

    @pl.when(h == pl.num_programs(0) - 1)
    def _():
        o_ref[...] = _rwkv_out(y_t[...].T, bonus_scr[...], g_scr[...], prm, prm[11][...], l)


def _rwkv_step(prw, w, l, s_all, sh_all):
    B = prw.shape[0]
    W = RWKV_WIDTH
    N = RWKV_HEAD
    return pl.pallas_call(
        functools.partial(_rwkv_step_kernel, l=l),
        grid=(RWKV_HEADS,),
        in_specs=[
            _full_spec(prw),
            _layer_spec(sh_all, l),
            pl.BlockSpec((None, None, N, N, B), lambda h: (l, h, 0, 0, 0)),
        ] + _rwkv_param_specs(w, l),
        out_specs=[
            pl.BlockSpec((B, W), lambda h: (0, 0)),
            pl.BlockSpec((None, N, N, B), lambda h: (h, 0, 0, 0)),
        ],
        out_shape=[
            jax.ShapeDtypeStruct((B, W), F32),
            jax.ShapeDtypeStruct((RWKV_HEADS, N, N, B), F32),
        ],
        scratch_shapes=[pltpu.VMEM((W, B), F32)] * 7 + [pltpu.VMEM((B, W), F32)] * 2,
        compiler_params=pltpu.CompilerParams(
            dimension_semantics=("arbitrary",), vmem_limit_bytes=VMEM_LIMIT),
        name="rwkv7_step",
    )(prw, sh_all, s_all, *[w[k] for k in _RWKV_PARAMS])


def _block_diag(blocks):
    *lead, n, r, c = blocks.shape
    eye = jnp.eye(n, dtype=blocks.dtype)
    out = blocks[..., :, :, None, :] * eye[:, None, :, None]
    return out.reshape(*lead, n * r, n * c)


def _prepare_weights(a):
    G, C, N = S5_GROUPS, S5_GROUP, S5_STATE
    abr, abi, bbr, bbi = _s5_prep(a["s5_lambda_re"], a["s5_lambda_im"], a["s5_log_dt"], a["s5_b_re"], a["s5_b_im"])

    def in_side(bb):
        return _block_diag(bb.reshape(DEPTH, S5_BLOCKS, S5_GROUPS_PER_BLOCK, C, N))

    def out_side(cc):
        ct = jnp.swapaxes(cc, 2, 3)
        return _block_diag(ct.reshape(DEPTH, S5_BLOCKS, S5_GROUPS_PER_BLOCK, N, C))

    w = {k: a[k] for k in (
        "s5_d", "s5_glu_w", "s5_glu_b", "lru_conv_w", "lru_conv_b", "lru_b_a", "lru_b_x", "lru_lambda",
        "rwkv_mu", "rwkv_w0", "rwkv_w_up", "rwkv_a0", "rwkv_a_up", "rwkv_g_up", "rwkv_k_k", "rwkv_k_a",
        "rwkv_ln_w", "rwkv_ln_b")}
    w.update({
        "n1": a["ffn1_norm"], "nm": a["mix_norm"], "n2": a["ffn2_norm"],
        "fnorm": a["final_norm"].reshape(1, D_MODEL),
        "wg1": a["ffn1_w_gate"].astype(BF16), "wu1": a["ffn1_w_up"].astype(BF16),
        "wd1": a["ffn1_w_down"].astype(BF16), "win": a["w_in"].astype(BF16), "wout": a["w_out"].astype(BF16),
        "wg2": a["ffn2_w_gate"].astype(BF16), "wu2": a["ffn2_w_up"].astype(BF16),
        "wd2": a["ffn2_w_down"].astype(BF16),
        "s5_ab": jnp.stack([abr.reshape(DEPTH, S5_LANES), abi.reshape(DEPTH, S5_LANES)], axis=1),
        "s5_win": jnp.concatenate([in_side(bbr), in_side(bbi)], axis=-1).astype(BF16),
        "s5_cre": out_side(a["s5_c_re"]).astype(BF16),
        "s5_cim": out_side(a["s5_c_im"]).astype(BF16),
        "lru_wa": _block_diag(a["lru_w_a"]).astype(BF16),
        "lru_wx": _block_diag(a["lru_w_x"]).astype(BF16),
        "rwkv_r_k": a["rwkv_r_k"].reshape(DEPTH, RWKV_WIDTH),
        "rwkv_ones": _block_diag(jnp.ones((2, RWKV_HEAD, RWKV_HEAD), F32)).astype(BF16),
    })
    return w


def _groups_to_heads(sg, B):
    N = RWKV_HEAD
    diag = jnp.stack([sg[:, N * i:N * (i + 1), N * i:N * (i + 1)] for i in range(RWKV_GROUP_HEADS)], axis=1)
    first = diag[:B]
    rest = diag[B:].reshape(B, RWKV_HEADS - RWKV_GROUP_HEADS, N, N)
    return jnp.concatenate([first, rest], axis=1)


def _seq_mixers(prw, psl3, st, w, l, Tc):
    L = prw.shape[1]
    osl, hT, lT, cT = _s5lru(psl3, w, l, st, Tc)
    orw, sT = _rwkv_chunk(prw, w, l, st["rwkv"], st["shift"], Tc)
    new = {"s5": hT[None], "lru": lT[None], "conv": cT[None], "li": 0,
           "rwkv": sT, "shift": prw[:, L - 1:L, :]}
    return osl, orw, new


def kernel(x_prompt, x_sample, state_s5_re, state_s5_im, state_rwkv, state_rwkv_shift, state_lru, state_lru_conv, meta_tokens, ffn1_norm, ffn1_w_gate, ffn1_w_up, ffn1_w_down, mix_norm, w_in, s5_lambda_re, s5_lambda_im, s5_log_dt, s5_b_re, s5_b_im, s5_c_re, s5_c_im, s5_d, s5_glu_w, s5_glu_b, rwkv_mu, rwkv_w0, rwkv_w_up, rwkv_a0, rwkv_a_up, rwkv_g_up, rwkv_k_k, rwkv_k_a, rwkv_r_k, rwkv_ln_w, rwkv_ln_b, lru_conv_w, lru_conv_b, lru_w_a, lru_b_a, lru_w_x, lru_b_x, lru_lambda, w_out, ffn2_norm, ffn2_w_gate, ffn2_w_up, ffn2_w_down, final_norm):
    w = _prepare_weights(dict(locals()))
    BP, SEQ, _ = x_prompt.shape
    BS = x_sample.shape[0]
    NS = N_META + BS

    zero = {
        "s5": jnp.zeros((1, 2, BP, S5_LANES), F32),
        "lru": jnp.zeros((1, BP, LRU_WIDTH), F32),
        "conv": jnp.zeros((1, CONV_HIST, BP, LRU_WIDTH), F32),
        "li": 0,
        "rwkv": jnp.zeros((len(_rwkv_groups(BP)), RWKV_GROUP_LANES, RWKV_GROUP_LANES), F32),
        "shift": jnp.zeros((BP, 1, RWKV_PROJ), F32),
    }
    sample_in = {
        "s5": jnp.stack([state_s5_re.reshape(DEPTH, BS, S5_LANES), state_s5_im.reshape(DEPTH, BS, S5_LANES)], axis=1),
        "lru": state_lru,
        "conv": jnp.swapaxes(state_lru_conv, 1, 2),
    }

    rwkv_in = jnp.transpose(state_rwkv, (0, 2, 3, 4, 1))

    xp = x_prompt
    xs = jnp.concatenate([meta_tokens.astype(F32), x_sample.reshape(BS, D_MODEL)], axis=0)[None]
    p_states = []
    s_states = []
    for l in range(DEPTH):
        final = l == DEPTH - 1
        hs, prw_s, psl_s = _k1(xs, w, l, NS)
        prw_m = jnp.broadcast_to(prw_s[:, :N_META], (BP, N_META, RWKV_PROJ))
        psl_m = jnp.broadcast_to(psl_s[:N_META, None, :], (N_META, BP, SL_IN))
        osl_m, orw_m, st_meta = _seq_mixers(prw_m, psl_m, zero, w, l, N_META)

        st_l = dict(sample_in, li=l)
        osl_s, hT, lT, cT = _s5lru(psl_s[N_META:].reshape(1, BS, SL_IN), w, l, st_l, 1)
        orw_s, sT = _rwkv_step(prw_s[0, N_META:], w, l, rwkv_in, state_rwkv_shift)
        s_states.append({"s5": hT, "lru": lT, "conv": cT, "rwkv": sT, "shift": prw_s[0, N_META:]})

        hp, prw_p, psl_p = _k1(xp, w, l, PROMPT_TOKEN_TILE)
        osl_p, orw_p, st_p = _seq_mixers(prw_p, psl_p.reshape(SEQ, BP, SL_IN), st_meta, w, l, PROMPT_CHUNK)
        p_states.append(st_p)
        xp = _k3(hp, osl_p.reshape(SEQ, BP * SL_OUT), orw_p, w, l, PROMPT_TOKEN_TILE, final)

        if final:
            xs = _k3(hs[:, N_META:], osl_s.reshape(BS, SL_OUT), orw_s[None], w, l, BS, True)
        else:
            osl = jnp.concatenate([osl_m[:, 0, :], osl_s.reshape(BS, SL_OUT)], axis=0)
            orw = jnp.concatenate([orw_m[0], orw_s], axis=0)[None]
            xs = _k3(hs, osl, orw, w, l, NS, False)

    p_out = (
        jnp.stack([s["s5"][0, 0].reshape(BP, S5_GROUPS, S5_STATE) for s in p_states]),
        jnp.stack([s["s5"][0, 1].reshape(BP, S5_GROUPS, S5_STATE) for s in p_states]),
        jnp.stack([_groups_to_heads(s["rwkv"], BP) for s in p_states]),
        jnp.stack([s["shift"][:, 0, :] for s in p_states]),
        jnp.stack([s["lru"][0] for s in p_states]),
        jnp.stack([jnp.swapaxes(s["conv"][0], 0, 1) for s in p_states]),
    )
    s_out = (
        jnp.stack([s["s5"][0].reshape(BS, S5_GROUPS, S5_STATE) for s in s_states]),
        jnp.stack([s["s5"][1].reshape(BS, S5_GROUPS, S5_STATE) for s in s_states]),
        jnp.transpose(jnp.stack([s["rwkv"] for s in s_states]), (0, 4, 1, 2, 3)),
        jnp.stack([s["shift"] for s in s_states]),
        jnp.stack([s["lru"] for s in s_states]),
        jnp.stack([jnp.swapaxes(s["conv"], 0, 1) for s in s_states]),
    )
    return (xp, xs.reshape(BS, 1, D_MODEL)) + p_out + s_out
```

```python
import functools
import math

import jax
import jax.numpy as jnp
from jax import lax
from jax.experimental import pallas as pl
from jax.experimental.pallas import tpu as pltpu

F32 = jnp.float32
BF16 = jnp.bfloat16

D_MODEL = 1024
DEPTH = 2
N_META = 16
D_FF = 2816
NORM_EPS = 1e-6
FFN_RES_SCALE = 0.5
S5_WIDTH = 384
S5_GROUP = 16
S5_GROUPS = 24
S5_STATE = 64
S5_LANES = S5_GROUPS * S5_STATE
RWKV_HEAD = 64
RWKV_WIDTH = 384
RWKV_HEADS = 6
RWKV_PAIRS = 3
DECAY_RANK = 64
ICL_RANK = 64
GATE_RANK = 128
RWKV_PROJ = 1408
RWKV_LN_EPS = 64e-5
LRU_WIDTH = 256
LRU_BLOCKS = 4
CONV_WIDTH = 4
CONV_HIST = CONV_WIDTH - 1
LRU_C = 8.0
D_IN = 2304
SL_IN = S5_WIDTH + 2 * LRU_WIDTH
SL_OUT = S5_WIDTH + LRU_WIDTH
LANE = 128
SUBLANE = 8
S5_GROUPS_PER_BLOCK = LANE // S5_GROUP
S5_BLOCKS = S5_GROUPS // S5_GROUPS_PER_BLOCK
S5_BLOCK_STATES = S5_GROUPS_PER_BLOCK * S5_STATE

V7X_VMEM_BYTES = 64 * 2 ** 20
VMEM_LIMIT = (V7X_VMEM_BYTES * 7) // 8

PROMPT_TOKEN_TILE = 512
WEIGHT_CHUNK = 256
ROW_SUBTILE = 256
PROMPT_CHUNK = 64


def _dot(a, b):
    return jnp.dot(a, b, preferred_element_type=F32)


def _dot_nt(a, b):
    bt = jnp.transpose(b.astype(F32)).astype(BF16)
    return jnp.dot(a, bt, preferred_element_type=F32)


def _dot_tn(a, b):
    return lax.dot_general(a, b, (((0,), (0,)), ((), ())), preferred_element_type=F32)


def _split3(x):
    hi = x.astype(BF16)
    r = x - hi.astype(F32)
    mid = r.astype(BF16)
    lo = (r - mid.astype(F32)).astype(BF16)
    return hi, mid, lo


def _sigmoid(x):
    return 0.5 * (jnp.tanh(0.5 * x) + 1.0)


def _gelu(x):
    c = math.sqrt(2.0 / math.pi)
    return 0.5 * x * (1.0 + jnp.tanh(c * (x + 0.044715 * (x * x * x))))


def _neg_expm1(z):
    t = jnp.tanh(0.5 * z)
    return -2.0 * t / (1.0 - t)


def _softplus(x):
    return jnp.maximum(x, 0.0) + jnp.log1p(jnp.exp(-jnp.abs(x)))


def _rms(x, g):
    return x * lax.rsqrt(jnp.mean(x * x, axis=-1, keepdims=True) + NORM_EPS) * g


def _swiglu_res(x, g_norm, wg_ref, wu_ref, wd_ref):
    xn = _rms(x, g_norm).astype(BF16)
    yield
    g = _dot(xn, wg_ref[...])
    u = _dot(xn, wu_ref[...])
    yield
    a = (g * _sigmoid(g) * u).astype(BF16)
    yield
    d = _dot(a, wd_ref[...])
    yield
    return x + FFN_RES_SCALE * d


def _staggered(gens):
    pending = list(gens)
    active = []
    while pending or active:
        if pending:
            active.append(pending.pop(0))
        for g in list(active):
            try:
                next(g)
            except StopIteration:
                active.remove(g)


def _row_tiles(n_rows):
    sub = ROW_SUBTILE if n_rows % ROW_SUBTILE == 0 else n_rows
    return [slice(r, r + sub) for r in range(0, n_rows, sub)]


def _full_spec(arr):
    nd = arr.ndim
    return pl.BlockSpec(arr.shape, lambda *_: (0,) * nd, pipeline_mode=pl.Buffered(1))


def _layer_spec(arr, l):
    nd = arr.ndim - 1
    return pl.BlockSpec((None,) + arr.shape[1:], lambda *_: (l,) + (0,) * nd, pipeline_mode=pl.Buffered(1))


def _row(ref, l):
    return ref[l:l + 1, :]


def _k1_kernel(x_ref, n1_ref, wg_ref, wu_ref, wd_ref, nm_ref, win_ref, h_ref, prw_ref, psl_ref, *, l):
    def rows_pipeline(rows):
        h = yield from _swiglu_res(x_ref[rows, :], _row(n1_ref, l), wg_ref, wu_ref, wd_ref)
        h_ref[rows, :] = h
        hn = _rms(h, _row(nm_ref, l)).astype(BF16)
        yield
        psl_ref[rows, 0:S5_WIDTH] = _dot(hn, win_ref[:, 0:S5_WIDTH])
        prw_ref[rows, :] = _dot(hn, win_ref[:, S5_WIDTH:S5_WIDTH + RWKV_PROJ])
        psl_ref[rows, S5_WIDTH:SL_IN] = _dot(hn, win_ref[:, S5_WIDTH + RWKV_PROJ:D_IN])

    _staggered([rows_pipeline(rows) for rows in _row_tiles(x_ref.shape[0])])


def _k1(x, w, l, tl):
    B, L, _ = x.shape
    return pl.pallas_call(
        functools.partial(_k1_kernel, l=l),
        grid=(B, L // tl),
        in_specs=[
            pl.BlockSpec((None, tl, D_MODEL), lambda b, i: (b, i, 0)),
            _full_spec(w["n1"]),
            _full_spec(w["wg1"]),
            _full_spec(w["wu1"]),
            _full_spec(w["wd1"]),
            _full_spec(w["nm"]),
            _full_spec(w["win"]),
        ],
        out_specs=[
            pl.BlockSpec((None, tl, D_MODEL), lambda b, i: (b, i, 0)),
            pl.BlockSpec((None, tl, RWKV_PROJ), lambda b, i: (b, i, 0)),
            pl.BlockSpec((tl, SL_IN), lambda b, i: (i, b)),
        ],
        out_shape=[
            jax.ShapeDtypeStruct((B, L, D_MODEL), F32),
            jax.ShapeDtypeStruct((B, L, RWKV_PROJ), F32),
            jax.ShapeDtypeStruct((L, B * SL_IN), F32),
        ],
        compiler_params=pltpu.CompilerParams(
            dimension_semantics=("parallel", "parallel"), vmem_limit_bytes=VMEM_LIMIT),
        name="ffn1_inproj",
    )(x, w["n1"], w["wg1"], w["wu1"], w["wd1"], w["nm"], w["win"])


def _k3_kernel(h_ref, osl_ref, orw_ref, wout_ref, n2_ref, wg_ref, wu_ref, wd_ref, fn_ref, o_ref, *, l, final):
    def rows_pipeline(rows):
        osl = osl_ref[rows, :]
        mix = (_dot(osl[:, 0:S5_WIDTH].astype(BF16), wout_ref[0:S5_WIDTH, :])
               + _dot(orw_ref[rows, :].astype(BF16), wout_ref[S5_WIDTH:S5_WIDTH + RWKV_WIDTH, :])
               + _dot(osl[:, S5_WIDTH:SL_OUT].astype(BF16), wout_ref[S5_WIDTH + RWKV_WIDTH:D_MODEL, :]))
        yield
        h3 = yield from _swiglu_res(h_ref[rows, :] + mix, _row(n2_ref, l), wg_ref, wu_ref, wd_ref)
        if final:
            h3 = _rms(h3, fn_ref[...])
        o_ref[rows, :] = h3

    _staggered([rows_pipeline(rows) for rows in _row_tiles(h_ref.shape[0])])


def _k3(h, osl, orw, w, l, tl, final):
    B, L, _ = h.shape
    return pl.pallas_call(
        functools.partial(_k3_kernel, l=l, final=final),
        grid=(B, L // tl),
        in_specs=[
            pl.BlockSpec((None, tl, D_MODEL), lambda b, i: (b, i, 0)),
            pl.BlockSpec((tl, SL_OUT), lambda b, i: (i, b)),
            pl.BlockSpec((None, tl, RWKV_WIDTH), lambda b, i: (b, i, 0)),
            _full_spec(w["wout"]),
            _full_spec(w["n2"]),
            _full_spec(w["wg2"]),
            _full_spec(w["wu2"]),
            _full_spec(w["wd2"]),
            _full_spec(w["fnorm"]),
        ],
        out_specs=pl.BlockSpec((None, tl, D_MODEL), lambda b, i: (b, i, 0)),
        out_shape=jax.ShapeDtypeStruct((B, L, D_MODEL), F32),
        compiler_params=pltpu.CompilerParams(
            dimension_semantics=("parallel", "parallel"), vmem_limit_bytes=VMEM_LIMIT),
        name="outproj_ffn2",
    )(h, osl, orw, w["wout"], w["n2"], w["wg2"], w["wu2"], w["wd2"], w["fnorm"])


def _stream_spec(arr, l, axis, first, count):
    shape = [None] + list(arr.shape[1:])
    shape[axis] = WEIGHT_CHUNK

    def index(c):
        idx = [l] + [0] * (arr.ndim - 1)
        idx[axis] = jnp.clip(c - first, 0, count - 1)
        return tuple(idx)

    return pl.BlockSpec(tuple(shape), index)


def _cast_spec(rows, cols, axis, first, count):
    shape = [rows, cols]
    shape[axis] = WEIGHT_CHUNK

    def index(c):
        idx = [0, 0]
        idx[axis] = jnp.clip(c - first, 0, count - 1)
        return tuple(idx)

    return pl.BlockSpec(tuple(shape), index)


def _ffn_chunk(c, first, count, xn_scr, acc_scr, wg_ref, wu_ref, wd_ref, wgb_ref, wub_ref, wdb_ref):
    @pl.when((c >= first) & (c < first + count))
    def _():
        wg = wg_ref[...].astype(BF16)
        wu = wu_ref[...].astype(BF16)
        wd = wd_ref[...].astype(BF16)
        wgb_ref[...] = wg
        wub_ref[...] = wu
        wdb_ref[...] = wd
        xn = xn_scr[...]
        g = _dot(xn, wg)
        u = _dot(xn, wu)
        a = (g * _sigmoid(g) * u).astype(BF16)
        acc_scr[...] += _dot(a, wd)


def _k1_stream_kernel(x_ref, n1_ref, nm_ref, wg_ref, wu_ref, wd_ref, win_ref,
                      h_ref, prw_ref, psl_ref, wgb_ref, wub_ref, wdb_ref, winb_ref,
                      xn_scr, acc_scr, proj_scr, *, l, nf, ni):
    c = pl.program_id(0)

    @pl.when(c == 0)
    def _():
        xn_scr[...] = _rms(x_ref[...], _row(n1_ref, l)).astype(BF16)
        acc_scr[...] = jnp.zeros(acc_scr.shape, F32)

    _ffn_chunk(c, 0, nf, xn_scr, acc_scr, wg_ref, wu_ref, wd_ref, wgb_ref, wub_ref, wdb_ref)

    @pl.when(c == nf - 1)
    def _():
        h = x_ref[...] + FFN_RES_SCALE * acc_scr[...]
        h_ref[...] = h
        xn_scr[...] = _rms(h, _row(nm_ref, l)).astype(BF16)

    @pl.when(c >= nf)
    def _():
        win = win_ref[...].astype(BF16)
        winb_ref[...] = win
        col = pl.multiple_of((c - nf) * WEIGHT_CHUNK, WEIGHT_CHUNK)
        proj_scr[:, pl.ds(col, WEIGHT_CHUNK)] = _dot(xn_scr[...], win)

    @pl.when(c == nf + ni - 1)
    def _():
        psl_ref[:, 0:S5_WIDTH] = proj_scr[:, 0:S5_WIDTH]
        prw_ref[...] = proj_scr[:, S5_WIDTH:S5_WIDTH + RWKV_PROJ]
        psl_ref[:, S5_WIDTH:SL_IN] = proj_scr[:, S5_WIDTH + RWKV_PROJ:D_IN]


def _k1_stream(x, a, l):
    NS = x.shape[0]
    nf, ni = D_FF // WEIGHT_CHUNK, D_IN // WEIGHT_CHUNK
    const = lambda shape: pl.BlockSpec(shape, lambda c: (0,) * len(shape))
    return pl.pallas_call(
        functools.partial(_k1_stream_kernel, l=l, nf=nf, ni=ni),
        grid=(nf + ni,),
        in_specs=[
            const((NS, D_MODEL)),
            _full_spec(a["ffn1_norm"]),
            _full_spec(a["mix_norm"]),
            _stream_spec(a["ffn1_w_gate"], l, 2, 0, nf),
            _stream_spec(a["ffn1_w_up"], l, 2, 0, nf),
            _stream_spec(a["ffn1_w_down"], l, 1, 0, nf),
            _stream_spec(a["w_in"], l, 2, nf, ni),
        ],
        out_specs=[
            const((NS, D_MODEL)),
            const((NS, RWKV_PROJ)),
            const((NS, SL_IN)),
            _cast_spec(D_MODEL, D_FF, 1, 0, nf),
            _cast_spec(D_MODEL, D_FF, 1, 0, nf),
            _cast_spec(D_FF, D_MODEL, 0, 0, nf),
            _cast_spec(D_MODEL, D_IN, 1, nf, ni),
        ],
        out_shape=[
            jax.ShapeDtypeStruct((NS, D_MODEL), F32),
            jax.ShapeDtypeStruct((NS, RWKV_PROJ), F32),
            jax.ShapeDtypeStruct((NS, SL_IN), F32),
            jax.ShapeDtypeStruct((D_MODEL, D_FF), BF16),
            jax.ShapeDtypeStruct((D_MODEL, D_FF), BF16),
            jax.ShapeDtypeStruct((D_FF, D_MODEL), BF16),
            jax.ShapeDtypeStruct((D_MODEL, D_IN), BF16),
        ],
        scratch_shapes=[
            pltpu.VMEM((NS, D_MODEL), BF16),
            pltpu.VMEM((NS, D_MODEL), F32),
            pltpu.VMEM((NS, D_IN), F32),
        ],
        compiler_params=pltpu.CompilerParams(dimension_semantics=("arbitrary",), vmem_limit_bytes=VMEM_LIMIT),
        name="ffn1_inproj_stream",
    )(x, a["ffn1_norm"], a["mix_norm"], a["ffn1_w_gate"], a["ffn1_w_up"], a["ffn1_w_down"], a["w_in"])


def _k3_stream_kernel(h_ref, osl_ref, orw_ref, n2_ref, fn_ref, wout_ref, wg_ref, wu_ref, wd_ref,
                      o_ref, woutb_ref, wgb_ref, wub_ref, wdb_ref,
                      mix_scr, h2_scr, xn_scr, acc_scr, *, l, no, nf, final):
    c = pl.program_id(0)

    @pl.when(c == 0)
    def _():
        osl = osl_ref[...]
        mix_scr[...] = jnp.concatenate(
            [osl[:, 0:S5_WIDTH], orw_ref[...], osl[:, S5_WIDTH:SL_OUT]], axis=-1).astype(BF16)
        h2_scr[...] = h_ref[...]

    @pl.when(c < no)
    def _():
        wout = wout_ref[...].astype(BF16)
        woutb_ref[...] = wout
        row = pl.multiple_of(c * WEIGHT_CHUNK, WEIGHT_CHUNK)
        h2_scr[...] += _dot(mix_scr[:, pl.ds(row, WEIGHT_CHUNK)], wout)

    @pl.when(c == no - 1)
    def _():
        xn_scr[...] = _rms(h2_scr[...], _row(n2_ref, l)).astype(BF16)
        acc_scr[...] = jnp.zeros(acc_scr.shape, F32)

    _ffn_chunk(c, no, nf, xn_scr, acc_scr, wg_ref, wu_ref, wd_ref, wgb_ref, wub_ref, wdb_ref)

    @pl.when(c == no + nf - 1)
    def _():
        h3 = h2_scr[...] + FFN_RES_SCALE * acc_scr[...]
        if final:
            h3 = _rms(h3, fn_ref[...])
        o_ref[...] = h3


def _k3_stream(h, osl, orw, a, fnorm, l, final):
    NS = h.shape[0]
    no, nf = D_MODEL // WEIGHT_CHUNK, D_FF // WEIGHT_CHUNK
    const = lambda shape: pl.BlockSpec(shape, lambda c: (0,) * len(shape))
    return pl.pallas_call(
        functools.partial(_k3_stream_kernel, l=l, no=no, nf=nf, final=final),
        grid=(no + nf,),
        in_specs=[
            const((NS, D_MODEL)),
            const((NS, SL_OUT)),
            const((NS, RWKV_WIDTH)),
            _full_spec(a["ffn2_norm"]),
            _full_spec(fnorm),
            _stream_spec(a["w_out"], l, 1, 0, no),
            _stream_spec(a["ffn2_w_gate"], l, 2, no, nf),
            _stream_spec(a["ffn2_w_up"], l, 2, no, nf),
            _stream_spec(a["ffn2_w_down"], l, 1, no, nf),
        ],
        out_specs=[
            const((NS, D_MODEL)),
            _cast_spec(D_MODEL, D_MODEL, 0, 0, no),
            _cast_spec(D_MODEL, D_FF, 1, no, nf),
            _cast_spec(D_MODEL, D_FF, 1, no, nf),
            _cast_spec(D_FF, D_MODEL, 0, no, nf),
        ],
        out_shape=[
            jax.ShapeDtypeStruct((NS, D_MODEL), F32),
            jax.ShapeDtypeStruct((D_MODEL, D_MODEL), BF16),
            jax.ShapeDtypeStruct((D_MODEL, D_FF), BF16),
            jax.ShapeDtypeStruct((D_MODEL, D_FF), BF16),
            jax.ShapeDtypeStruct((D_FF, D_MODEL), BF16),
        ],
        scratch_shapes=[
            pltpu.VMEM((NS, D_MODEL), BF16),
            pltpu.VMEM((NS, D_MODEL), F32),
            pltpu.VMEM((NS, D_MODEL), BF16),
            pltpu.VMEM((NS, D_MODEL), F32),
        ],
        compiler_params=pltpu.CompilerParams(dimension_semantics=("arbitrary",), vmem_limit_bytes=VMEM_LIMIT),
        name="outproj_ffn2_stream",
    )(h, osl, orw, a["ffn2_norm"], fnorm, a["w_out"], a["ffn2_w_gate"], a["ffn2_w_up"], a["ffn2_w_down"])


def _s5_prep_kernel(lr_ref, li_ref, ldt_ref, br_ref, bi_ref, abr_ref, abi_ref, bbr_ref, bbi_ref):
    lr = lr_ref[...]
    li = li_ref[...]
    dt = jnp.exp(ldt_ref[...])
    mag = jnp.exp(lr * dt)
    ab_re = mag * jnp.cos(li * dt)
    ab_im = mag * jnp.sin(li * dt)
    den = lr * lr + li * li
    f_re = ((ab_re - 1.0) * lr + ab_im * li) / den
    f_im = (ab_im * lr - (ab_re - 1.0) * li) / den
    abr_ref[...] = ab_re
    abi_ref[...] = ab_im
    br = br_ref[...]
    bi = bi_ref[...]
    bbr_ref[...] = f_re[:, :, None, :] * br - f_im[:, :, None, :] * bi
    bbi_ref[...] = f_re[:, :, None, :] * bi + f_im[:, :, None, :] * br


def _s5_prep(lam_re, lam_im, log_dt, b_re, b_im):
    G, N, C = S5_GROUPS, S5_STATE, S5_GROUP
    ldt = jnp.broadcast_to(log_dt[:, :, None], (DEPTH, G, N))
    br = jnp.swapaxes(b_re, 2, 3)
    bi = jnp.swapaxes(b_im, 2, 3)
    return pl.pallas_call(
        _s5_prep_kernel,
        out_shape=[
            jax.ShapeDtypeStruct((DEPTH, G, N), F32),
            jax.ShapeDtypeStruct((DEPTH, G, N), F32),
            jax.ShapeDtypeStruct((DEPTH, G, C, N), F32),
            jax.ShapeDtypeStruct((DEPTH, G, C, N), F32),
        ],
        name="s5_discretise",
    )(lam_re, lam_im, ldt, br, bi)


def _s5lru_kernel(psl_ref, ab_ref, win_ref, cre_ref, cim_ref, d_ref, gw_ref, gb_ref,
                  cw_ref, cb_ref, wa_ref, wx_ref, ba_ref, bx_ref, lam_ref,
                  h0_ref, l0_ref, c0_ref,
                  o_ref, hT_ref, lT_ref, cT_ref,
                  hs_scr, ls_scr, xc_scr, xr_scr, xi_scr, la_scr, lb_scr, *, l, B, Tc, unroll):
    c = pl.program_id(0)
    R = Tc * B

    @pl.when(c == 0)
    def _():
        hs_scr[...] = h0_ref[...]
        ls_scr[...] = l0_ref[...]
        xc_scr[0:CONV_HIST] = c0_ref[...]

    psl = psl_ref[...]
    u = psl[:, :, 0:S5_WIDTH].reshape(R, S5_WIDTH)
    gate_in = psl[:, :, S5_WIDTH + LRU_WIDTH:SL_IN].reshape(R, LRU_WIDTH)

    ub = u.astype(BF16)
    half = S5_BLOCK_STATES
    for j in range(S5_BLOCKS):
        x = _dot(ub[:, LANE * j:LANE * (j + 1)], win_ref[j])
        xr_scr[:, :, half * j:half * (j + 1)] = x[:, 0:half].reshape(Tc, B, half)
        xi_scr[:, :, half * j:half * (j + 1)] = x[:, half:2 * half].reshape(Tc, B, half)

    xc_scr[CONV_HIST:CONV_HIST + Tc] = psl[:, :, S5_WIDTH:S5_WIDTH + LRU_WIDTH]
    cw = cw_ref[...]
    xc = _row(cb_ref, l) + xc_scr[0:Tc] * cw[0:1]
    for j in range(1, CONV_WIDTH):
        xc = xc + xc_scr[j:j + Tc] * cw[j:j + 1]
    new_hist = xc_scr[Tc:Tc + CONV_HIST]
    xc_scr[0:CONV_HIST] = new_hist
    xc2 = xc.reshape(R, LRU_WIDTH)
    xcb = xc2.astype(BF16)
    gate_a = _sigmoid(_dot(xcb, wa_ref[...]) + _row(ba_ref, l))
    gate_x = _sigmoid(_dot(xcb, wx_ref[...]) + _row(bx_ref, l))
    log_a = LRU_C * gate_a * (-_softplus(-_row(lam_ref, l)))
    la_scr[...] = jnp.exp(log_a).reshape(Tc, B, LRU_WIDTH)
    lb_scr[...] = (jnp.sqrt(_neg_expm1(2.0 * log_a)) * (gate_x * xc2)).reshape(Tc, B, LRU_WIDTH)

    ar = jnp.broadcast_to(ab_ref[0:1, :], (B, S5_LANES))
    ai = jnp.broadcast_to(ab_ref[1:2, :], (B, S5_LANES))

    def step(t, carry):
        hr, hi, hl = carry
        nr = ar * hr - ai * hi + xr_scr[t]
        ni = ar * hi + ai * hr + xi_scr[t]
        xr_scr[t] = nr
        xi_scr[t] = ni
        nl = la_scr[t] * hl + lb_scr[t]
        lb_scr[t] = nl
        return nr, ni, nl

    hr, hi, hl = lax.fori_loop(0, Tc, step, (hs_scr[0], hs_scr[1], ls_scr[...]), unroll=unroll)
    hs_scr[0] = hr
    hs_scr[1] = hi
    ls_scr[...] = hl

    hrb = xr_scr[...].reshape(R, S5_LANES).astype(BF16)
    hib = xi_scr[...].reshape(R, S5_LANES).astype(BF16)
    ys = []
    for j in range(S5_BLOCKS):
        sl = slice(half * j, half * (j + 1))
        ys.append(_dot(hrb[:, sl], cre_ref[j]) - _dot(hib[:, sl], cim_ref[j]))
    y = jnp.concatenate(ys, axis=-1) + _row(d_ref, l) * u
    z = _gelu(y)
    o_s5 = z * _sigmoid(_dot(z.astype(BF16), gw_ref[...].astype(BF16)) + _row(gb_ref, l))
    o_lru = lb_scr[...].reshape(R, LRU_WIDTH) * _gelu(gate_in)
    o_ref[:, :, 0:S5_WIDTH] = o_s5.reshape(Tc, B, S5_WIDTH)
    o_ref[:, :, S5_WIDTH:SL_OUT] = o_lru.reshape(Tc, B, LRU_WIDTH)

    @pl.when(c == pl.num_programs(0) - 1)
    def _():
        hT_ref[...] = hs_scr[...]
        lT_ref[...] = ls_scr[...]
        cT_ref[...] = xc_scr[0:CONV_HIST]


def _s5lru(psl3, w, l, st, Tc):
    L, B, _ = psl3.shape
    li = st["li"]
    unroll = True
    return pl.pallas_call(
        functools.partial(_s5lru_kernel, l=l, B=B, Tc=Tc, unroll=unroll),
        grid=(L // Tc,),
        in_specs=[
            pl.BlockSpec((Tc, B, SL_IN), lambda c: (c, 0, 0)),
            _layer_spec(w["s5_ab"], l),
            _layer_spec(w["s5_win"], l),
            _layer_spec(w["s5_cre"], l),
            _layer_spec(w["s5_cim"], l),
            _full_spec(w["s5_d"]),
            _layer_spec(w["s5_glu_w"], l),
            _full_spec(w["s5_glu_b"]),
            _layer_spec(w["lru_conv_w"], l),
            _full_spec(w["lru_conv_b"]),
            _layer_spec(w["lru_wa"], l),
            _layer_spec(w["lru_wx"], l),
            _full_spec(w["lru_b_a"]),
            _full_spec(w["lru_b_x"]),
            _full_spec(w["lru_lambda"]),
            _layer_spec(st["s5"], li),
            _layer_spec(st["lru"], li),
            _layer_spec(st["conv"], li),
        ],
        out_specs=[
            pl.BlockSpec((Tc, B, SL_OUT), lambda c: (c, 0, 0)),
            pl.BlockSpec((2, B, S5_LANES), lambda c: (0, 0, 0)),
            pl.BlockSpec((B, LRU_WIDTH), lambda c: (0, 0)),
            pl.BlockSpec((CONV_HIST, B, LRU_WIDTH), lambda c: (0, 0, 0)),
        ],
        out_shape=[
            jax.ShapeDtypeStruct((L, B, SL_OUT), F32),
            jax.ShapeDtypeStruct((2, B, S5_LANES), F32),
            jax.ShapeDtypeStruct((B, LRU_WIDTH), F32),
            jax.ShapeDtypeStruct((CONV_HIST, B, LRU_WIDTH), F32),
        ],
        scratch_shapes=[
            pltpu.VMEM((2, B, S5_LANES), F32),
            pltpu.VMEM((B, LRU_WIDTH), F32),
            pltpu.VMEM((Tc + CONV_HIST, B, LRU_WIDTH), F32),
            pltpu.VMEM((Tc, B, S5_LANES), F32),
            pltpu.VMEM((Tc, B, S5_LANES), F32),
            pltpu.VMEM((Tc, B, LRU_WIDTH), F32),
            pltpu.VMEM((Tc, B, LRU_WIDTH), F32),
        ],
        compiler_params=pltpu.CompilerParams(
            dimension_semantics=("arbitrary",), vmem_limit_bytes=VMEM_LIMIT),
        name="s5_rglru_scan",
    )(psl3, w["s5_ab"], w["s5_win"], w["s5_cre"], w["s5_cim"], w["s5_d"], w["s5_glu_w"], w["s5_glu_b"],
      w["lru_conv_w"], w["lru_conv_b"], w["lru_wa"], w["lru_wx"], w["lru_b_a"], w["lru_b_x"], w["lru_lambda"],
      st["s5"], st["lru"], st["conv"])


_RWKV_PARAMS = ("rwkv_mu", "rwkv_w0", "rwkv_w_up", "rwkv_a0", "rwkv_a_up", "rwkv_g_up", "rwkv_k_k", "rwkv_k_a",
                "rwkv_r_k", "rwkv_ln_w", "rwkv_ln_b", "rwkv_ones")


def _rwkv_param_specs(w, l):
    layered = ("rwkv_w_up", "rwkv_a_up", "rwkv_g_up")
    return [_layer_spec(w[k], l) if k in layered else _full_spec(w[k]) for k in _RWKV_PARAMS]


def _rwkv_mats(prm):
    wup_ref, aup_ref, gup_ref, ones_ref = prm[2], prm[4], prm[5], prm[11]
    zpad = jnp.zeros((DECAY_RANK, RWKV_WIDTH), BF16)
    wup = jnp.concatenate([wup_ref[...].astype(BF16), zpad], axis=0)
    aup = jnp.concatenate([zpad, aup_ref[...].astype(BF16)], axis=0)
    return wup, aup, gup_ref[...].astype(BF16), ones_ref[...]


def _segsum(x, ones):
    xb = x.astype(BF16)
    return jnp.concatenate([_dot(xb[:, LANE * p:LANE * (p + 1)], ones) for p in range(RWKV_PAIRS)], axis=-1)


def _rwkv_token_math(p, prev, prm, mats, l):
    mu_ref, w0_ref, _, a0_ref, _, _, kk_ref, ka_ref, rk_ref = prm[0:9]
    wup, aup, gup, ones = mats
    W = RWKV_WIDTH
    xm = p + (prev - p) * _row(mu_ref, l)
    r = xm[:, 0:W]
    k = xm[:, W:2 * W]
    v = xm[:, 2 * W:3 * W]
    xwa = xm[:, 3 * W:3 * W + DECAY_RANK + ICL_RANK]
    xg = xm[:, 3 * W + DECAY_RANK + ICL_RANK:RWKV_PROJ]
    lw = _row(w0_ref, l) + _dot(jnp.tanh(xwa).astype(BF16), wup)
    ld = -math.exp(-0.5) * _sigmoid(lw)
    a = _sigmoid(_row(a0_ref, l) + _dot(xwa.astype(BF16), aup))
    g = _dot(_sigmoid(xg).astype(BF16), gup)
    kkr = k * _row(kk_ref, l)
    kk = kkr * lax.rsqrt(jnp.maximum(_segsum(kkr * kkr, ones), 1e-24))
    k2 = k * (1.0 + (a - 1.0) * _row(ka_ref, l))
    bonus = _segsum(r * k2 * _row(rk_ref, l), ones) * v
    return r, k2, v, kk, a, ld, g, bonus


def _rwkv_out(y, bonus, g, prm, ones, l):
    lnw_ref, lnb_ref = prm[9], prm[10]
    inv = 1.0 / RWKV_HEAD
    mean = _segsum(y, ones) * inv
    d = y - mean
    var = _segsum(d * d, ones) * inv
    yn = d * lax.rsqrt(var + RWKV_LN_EPS) * _row(lnw_ref, l) + _row(lnb_ref, l)
    return (yn + bonus) * g


RWKV_GROUP_HEADS = 4
RWKV_GROUP_LANES = RWKV_GROUP_HEADS * RWKV_HEAD


def _rwkv_groups(B):
    full = [[(b, 0, RWKV_GROUP_LANES)] for b in range(B)]
    rest = RWKV_WIDTH - RWKV_GROUP_LANES
    tail = [[(b, RWKV_GROUP_LANES, rest), (b + 1, RWKV_GROUP_LANES, rest)] for b in range(0, B, 2)]
    return full + tail


def _rwkv_chunk_kernel(*refs, l, B, Tc):
    n_prm = len(_RWKV_PARAMS)
    p_ref, s0_ref, sh0_ref = refs[0:3]
    prm = refs[3:3 + n_prm]
    tri_ref, o_ref, sT_ref = refs[3 + n_prm:6 + n_prm]
    st_scr, xs_scr, y_scr, pt_scr = refs[6 + n_prm:10 + n_prm]
    src = dict(zip(("a", "b", "k", "r", "v", "bp", "kp"), refs[10 + n_prm:]))
    c = pl.program_id(0)
    R = B * Tc
    W = RWKV_WIDTH
    GL = RWKV_GROUP_LANES
    GH = RWKV_GROUP_HEADS
    carry_row = SUBLANE - 1

    @pl.when(c == 0)
    def _():
        st_scr[...] = s0_ref[...]
        xs_scr[:, carry_row:SUBLANE, :] = sh0_ref[...]

    ps, prevs = [], []
    for b in range(B):
        pb = p_ref[b]
        xs_scr[b, SUBLANE:SUBLANE + Tc, :] = pb
        prevs.append(xs_scr[b, carry_row:carry_row + Tc, :])
        xs_scr[b, carry_row:SUBLANE, :] = pb[Tc - 1:Tc, :]
        ps.append(pb)
    mats = _rwkv_mats(prm)
    r, k2, v, kk, a, ld, g, bonus = _rwkv_token_math(
        jnp.concatenate(ps, axis=0), jnp.concatenate(prevs, axis=0), prm, mats, l)

    h1, h2, h3 = _split3(ld)
    tri = tri_ref[...]
    cls = []
    for b in range(B):
        rows = slice(b * Tc, (b + 1) * Tc)
        cls.append(_dot(tri, h1[rows]) + _dot(tri, h2[rows]) + _dot(tri, h3[rows]))
    cl = jnp.concatenate(cls, axis=0)
    pt = jnp.exp(cl.reshape(B, Tc, W)[:, Tc - 1:Tc, :])
    inv_p = jnp.exp(-cl)
    to_end = (pt * inv_p.reshape(B, Tc, W)).reshape(R, W)
    beta = kk * a
    src["a"][...] = (-kk * jnp.exp(cl - ld)).astype(BF16)
    src["b"][...] = (beta * inv_p).astype(BF16)
    src["k"][...] = (k2 * inv_p).astype(BF16)
    src["r"][...] = (r * jnp.exp(cl)).astype(BF16)
    src["v"][...] = v.astype(BF16)
    src["bp"][...] = (beta * to_end).astype(BF16)
    src["kp"][...] = (k2 * to_end).astype(BF16)
    pt_scr[...] = pt

    assert Tc & (Tc - 1) == 0
    n_doubling = max(1, (Tc - 1).bit_length())
    groups = _rwkv_groups(B)

    def block_masks(blk):
        lane = lax.broadcasted_iota(jnp.int32, (1, GH * blk), 1)
        return [((lane >= blk * h) & (lane < blk * (h + 1))).astype(BF16) for h in range(GH)]

    head_masks = block_masks(RWKV_HEAD)
    time_masks = block_masks(Tc)

    def stack(x, masks):
        return jnp.concatenate([x * m for m in masks], axis=0)

    ti = lax.broadcasted_iota(jnp.int32, (Tc, GH * Tc), 0)
    si = lax.broadcasted_iota(jnp.int32, (Tc, GH * Tc), 1) & (Tc - 1)
    strict = (si < ti).astype(BF16)
    incl = (si <= ti).astype(BF16)
    same_head = ((lax.broadcasted_iota(jnp.int32, (GL, GL), 0) // RWKV_HEAD)
                 == (lax.broadcasted_iota(jnp.int32, (GL, GL), 1) // RWKV_HEAD)).astype(F32)

    def operand(name, grp):
        parts = [src[name][b * Tc:(b + 1) * Tc, lo:lo + n] for b, lo, n in grp]
        return parts[0] if len(parts) == 1 else jnp.concatenate(parts, axis=-1)

    v4s, gss, ns, wrbs, wrks, xs = [], [], [], [], [], []
    for gi, grp in enumerate(groups):
        lhs = jnp.concatenate([operand("a", grp), operand("r", grp)], axis=0)
        v4 = stack(operand("v", grp), head_masks)
        gb = _dot_nt(lhs, stack(operand("b", grp), head_masks)).astype(BF16)
        gk = _dot_nt(lhs, stack(operand("k", grp), head_masks)).astype(BF16)
        gs = _dot_nt(lhs, st_scr[gi].astype(BF16))
        v4s.append(v4)
        gss.append(gs[Tc:2 * Tc])
        ns.append(gb[0:Tc] * strict)
        wrbs.append(gb[Tc:2 * Tc] * incl)
        wrks.append(gk[Tc:2 * Tc] * incl)
        xs.append(gs[0:Tc] + _dot(gk[0:Tc] * strict, v4))
    for j in range(n_doubling):
        for i in range(len(groups)):
            xs[i] = xs[i] + _dot(ns[i], stack(xs[i].astype(BF16), head_masks))
        if j + 1 < n_doubling:
            for i in range(len(groups)):
                ns[i] = _dot(ns[i], stack(ns[i], time_masks)).astype(BF16)
    for gi, grp in enumerate(groups):
        sab = xs[gi].astype(BF16)
        y = gss[gi] + _dot(wrbs[gi], stack(sab, head_masks)) + _dot(wrks[gi], v4s[gi])
        lane0 = 0
        for b, lo, n in grp:
            y_scr[b * Tc:(b + 1) * Tc, lo:lo + n] = y[:, lane0:lane0 + n]
            lane0 += n
        upd = _dot_tn(jnp.concatenate([sab, operand("v", grp)], axis=0),
                      jnp.concatenate([operand("bp", grp), operand("kp", grp)], axis=0))
        ptg = [pt_scr[b, :, lo:lo + n] for b, lo, n in grp]
        ptg = ptg[0] if len(ptg) == 1 else jnp.concatenate(ptg, axis=-1)
        st_scr[gi] = st_scr[gi] * ptg + upd * same_head

    out = _rwkv_out(y_scr[...], bonus, g, prm, mats[3], l)
    for b in range(B):
        o_ref[b] = out[b * Tc:(b + 1) * Tc]

    @pl.when(c == pl.num_programs(0) - 1)
    def _():
        sT_ref[...] = st_scr[...]


def _rwkv_chunk(prw, w, l, s0, sh0, Tc):
    B, L, _ = prw.shape
    R = B * Tc
    W = RWKV_WIDTH
    GL = RWKV_GROUP_LANES
    NG = len(_rwkv_groups(B))
    tri = jnp.tril(jnp.ones((Tc, Tc), F32)).astype(BF16)
    return pl.pallas_call(
        functools.partial(_rwkv_chunk_kernel, l=l, B=B, Tc=Tc),
        grid=(L // Tc,),
        in_specs=[
            pl.BlockSpec((B, Tc, RWKV_PROJ), lambda c: (0, c, 0)),
            _full_spec(s0),
            _full_spec(sh0),
        ] + _rwkv_param_specs(w, l) + [_full_spec(tri)],
        out_specs=[
            pl.BlockSpec((B, Tc, W), lambda c: (0, c, 0)),
            pl.BlockSpec((NG, GL, GL), lambda c: (0, 0, 0)),
        ],
        out_shape=[
            jax.ShapeDtypeStruct((B, L, W), F32),
            jax.ShapeDtypeStruct((NG, GL, GL), F32),
        ],
        scratch_shapes=[
            pltpu.VMEM((NG, GL, GL), F32),
            pltpu.VMEM((B, Tc + SUBLANE, RWKV_PROJ), F32),
            pltpu.VMEM((R, W), F32),
            pltpu.VMEM((B, 1, W), F32),
        ] + [pltpu.VMEM((R, W), BF16)] * 7,
        compiler_params=pltpu.CompilerParams(
            dimension_semantics=("arbitrary",), vmem_limit_bytes=VMEM_LIMIT),
        name="rwkv7_chunked",
    )(prw, s0, sh0, *[w[k] for k in _RWKV_PARAMS], tri)


def _rwkv_step_kernel(*refs, l):
    n_prm = len(_RWKV_PARAMS)
    p_ref, sh_ref, s_ref = refs[0:3]
    prm = refs[3:3 + n_prm]
    o_ref, sn_ref = refs[3 + n_prm:5 + n_prm]
    nkk_t, w_t, beta_t, v_t, k2_t, r_t, y_t, g_scr, bonus_scr = refs[5 + n_prm:]
    h = pl.program_id(0)
    N = RWKV_HEAD

    @pl.when(h == 0)
    def _():
        r, k2, v, kk, a, ld, g, bonus = _rwkv_token_math(p_ref[...], sh_ref[...], prm, _rwkv_mats(prm), l)
        nkk_t[...] = (-kk).T
        w_t[...] = jnp.exp(ld).T
        beta_t[...] = (kk * a).T
        v_t[...] = v.T
        k2_t[...] = k2.T
        r_t[...] = r.T
        g_scr[...] = g
        bonus_scr[...] = bonus

    rows = pl.ds(pl.multiple_of(h * N, N), N)
    s = s_ref[...]
    sa = jnp.sum(s * nkk_t[rows, :][None], axis=1)
    sn = s * w_t[rows, :][None] + sa[:, None, :] * beta_t[rows, :][None] + v_t[rows, :][:, None, :] * k2_t[rows, :][None]
    sn_ref[...] = sn
    y_t[rows, :] = jnp.sum(sn * r_t[rows, :][None], axis=1)

    @pl.when(h == pl.num_programs(0) - 1)
    def _():
        o_ref[...] = _rwkv_out(y_t[...].T, bonus_scr[...], g_scr[...], prm, prm[11][...], l)


def _rwkv_step(prw, w, l, s_all, sh_all):
    B = prw.shape[0]
    W = RWKV_WIDTH
    N = RWKV_HEAD
    return pl.pallas_call(
        functools.partial(_rwkv_step_kernel, l=l),
        grid=(RWKV_HEADS,),
        in_specs=[
            _full_spec(prw),
            _layer_spec(sh_all, l),
            pl.BlockSpec((None, None, N, N, B), lambda h: (l, h, 0, 0, 0)),
        ] + _rwkv_param_specs(w, l),
        out_specs=[
            pl.BlockSpec((B, W), lambda h: (0, 0)),
            pl.BlockSpec((None, N, N, B), lambda h: (h, 0, 0, 0)),
        ],
        out_shape=[
            jax.ShapeDtypeStruct((B, W), F32),
            jax.ShapeDtypeStruct((RWKV_HEADS, N, N, B), F32),
        ],
        scratch_shapes=[pltpu.VMEM((W, B), F32)] * 7 + [pltpu.VMEM((B, W), F32)] * 2,
        compiler_params=pltpu.CompilerParams(
            dimension_semantics=("arbitrary",), vmem_limit_bytes=VMEM_LIMIT),
        name="rwkv7_step",
    )(prw, sh_all, s_all, *[w[k] for k in _RWKV_PARAMS])


def _block_diag(blocks):
    *lead, n, r, c = blocks.shape
    eye = jnp.eye(n, dtype=blocks.dtype)
    out = blocks[..., :, :, None, :] * eye[:, None, :, None]
    return out.reshape(*lead, n * r, n * c)


def _prepare_weights(a):
    G, C, N = S5_GROUPS, S5_GROUP, S5_STATE
    abr, abi, bbr, bbi = _s5_prep(a["s5_lambda_re"], a["s5_lambda_im"], a["s5_log_dt"], a["s5_b_re"], a["s5_b_im"])

    def in_side(bb):
        return _block_diag(bb.reshape(DEPTH, S5_BLOCKS, S5_GROUPS_PER_BLOCK, C, N))

    def out_side(cc):
        ct = jnp.swapaxes(cc, 2, 3)
        return _block_diag(ct.reshape(DEPTH, S5_BLOCKS, S5_GROUPS_PER_BLOCK, N, C))

    w = {k: a[k] for k in (
        "s5_d", "s5_glu_w", "s5_glu_b", "lru_conv_w", "lru_conv_b", "lru_b_a", "lru_b_x", "lru_lambda",
        "rwkv_mu", "rwkv_w0", "rwkv_w_up", "rwkv_a0", "rwkv_a_up", "rwkv_g_up", "rwkv_k_k", "rwkv_k_a",
        "rwkv_ln_w", "rwkv_ln_b")}
    w.update({
        "n1": a["ffn1_norm"], "nm": a["mix_norm"], "n2": a["ffn2_norm"],
        "fnorm": a["final_norm"].reshape(1, D_MODEL),
        "s5_ab": jnp.stack([abr.reshape(DEPTH, S5_LANES), abi.reshape(DEPTH, S5_LANES)], axis=1),
        "s5_win": jnp.concatenate([in_side(bbr), in_side(bbi)], axis=-1).astype(BF16),
        "s5_cre": out_side(a["s5_c_re"]).astype(BF16),
        "s5_cim": out_side(a["s5_c_im"]).astype(BF16),
        "lru_wa": _block_diag(a["lru_w_a"]).astype(BF16),
        "lru_wx": _block_diag(a["lru_w_x"]).astype(BF16),
        "rwkv_r_k": a["rwkv_r_k"].reshape(DEPTH, RWKV_WIDTH),
        "rwkv_ones": _block_diag(jnp.ones((2, RWKV_HEAD, RWKV_HEAD), F32)).astype(BF16),
    })
    return w


def _groups_to_heads(sg, B):
    N = RWKV_HEAD
    diag = jnp.stack([sg[:, N * i:N * (i + 1), N * i:N * (i + 1)] for i in range(RWKV_GROUP_HEADS)], axis=1)
    first = diag[:B]
    rest = diag[B:].reshape(B, RWKV_HEADS - RWKV_GROUP_HEADS, N, N)
    return jnp.concatenate([first, rest], axis=1)


def _seq_mixers(prw, psl3, st, w, l, Tc):
    L = prw.shape[1]
    osl, hT, lT, cT = _s5lru(psl3, w, l, st, Tc)
    orw, sT = _rwkv_chunk(prw, w, l, st["rwkv"], st["shift"], Tc)
    new = {"s5": hT[None], "lru": lT[None], "conv": cT[None], "li": 0,
           "rwkv": sT, "shift": prw[:, L - 1:L, :]}
    return osl, orw, new


def kernel(x_prompt, x_sample, state_s5_re, state_s5_im, state_rwkv, state_rwkv_shift, state_lru, state_lru_conv, meta_tokens, ffn1_norm, ffn1_w_gate, ffn1_w_up, ffn1_w_down, mix_norm, w_in, s5_lambda_re, s5_lambda_im, s5_log_dt, s5_b_re, s5_b_im, s5_c_re, s5_c_im, s5_d, s5_glu_w, s5_glu_b, rwkv_mu, rwkv_w0, rwkv_w_up, rwkv_a0, rwkv_a_up, rwkv_g_up, rwkv_k_k, rwkv_k_a, rwkv_r_k, rwkv_ln_w, rwkv_ln_b, lru_conv_w, lru_conv_b, lru_w_a, lru_b_a, lru_w_x, lru_b_x, lru_lambda, w_out, ffn2_norm, ffn2_w_gate, ffn2_w_up, ffn2_w_down, final_norm):
    a = dict(locals())
    w = _prepare_weights(a)
    BP, SEQ, _ = x_prompt.shape
    BS = x_sample.shape[0]
    assert SEQ % PROMPT_TOKEN_TILE == 0 and SEQ % PROMPT_CHUNK == 0 and BP % 2 == 0

    zero = {
        "s5": jnp.zeros((1, 2, BP, S5_LANES), F32),
        "lru": jnp.zeros((1, BP, LRU_WIDTH), F32),
        "conv": jnp.zeros((1, CONV_HIST, BP, LRU_WIDTH), F32),
        "li": 0,
        "rwkv": jnp.zeros((len(_rwkv_groups(BP)), RWKV_GROUP_LANES, RWKV_GROUP_LANES), F32),
        "shift": jnp.zeros((BP, 1, RWKV_PROJ), F32),
    }
    sample_in = {
        "s5": jnp.stack([state_s5_re.reshape(DEPTH, BS, S5_LANES), state_s5_im.reshape(DEPTH, BS, S5_LANES)], axis=1),
        "lru": state_lru,
        "conv": jnp.swapaxes(state_lru_conv, 1, 2),
    }

    rwkv_in = jnp.transpose(state_rwkv, (0, 2, 3, 4, 1))

    xp = x_prompt
    xs = jnp.concatenate([meta_tokens.astype(F32), x_sample.reshape(BS, D_MODEL)], axis=0)
    p_states = []
    s_states = []
    for l in range(DEPTH):
        final = l == DEPTH - 1
        hs, prw_s, psl_s, w["wg1"], w["wu1"], w["wd1"], w["win"] = _k1_stream(xs, a, l)
        prw_m = jnp.broadcast_to(prw_s[None, :N_META], (BP, N_META, RWKV_PROJ))
        psl_m = jnp.broadcast_to(psl_s[:N_META, None, :], (N_META, BP, SL_IN))
        osl_m, orw_m, st_meta = _seq_mixers(prw_m, psl_m, zero, w, l, N_META)

        st_l = dict(sample_in, li=l)
        osl_s, hT, lT, cT = _s5lru(psl_s[N_META:].reshape(1, BS, SL_IN), w, l, st_l, 1)
        orw_s, sT = _rwkv_step(prw_s[N_META:], w, l, rwkv_in, state_rwkv_shift)
        s_states.append({"s5": hT, "lru": lT, "conv": cT, "rwkv": sT, "shift": prw_s[N_META:]})
        if final:
            rows = (hs[N_META:], osl_s.reshape(BS, SL_OUT), orw_s)
        else:
            rows = (hs, jnp.concatenate([osl_m[:, 0, :], osl_s.reshape(BS, SL_OUT)], axis=0),
                    jnp.concatenate([orw_m[0], orw_s], axis=0))
        xs, w["wout"], w["wg2"], w["wu2"], w["wd2"] = _k3_stream(*rows, a, w["fnorm"], l, final)

        hp, prw_p, psl_p = _k1(xp, w, l, PROMPT_TOKEN_TILE)
        osl_p, orw_p, st_p = _seq_mixers(prw_p, psl_p.reshape(SEQ, BP, SL_IN), st_meta, w, l, PROMPT_CHUNK)
        p_states.append(st_p)
        xp = _k3(hp, osl_p.reshape(SEQ, BP * SL_OUT), orw_p, w, l, PROMPT_TOKEN_TILE, final)

    p_out = (
        jnp.stack([s["s5"][0, 0].reshape(BP, S5_GROUPS, S5_STATE) for s in p_states]),
        jnp.stack([s["s5"][0, 1].reshape(BP, S5_GROUPS, S5_STATE) for s in p_states]),
        jnp.stack([_groups_to_heads(s["rwkv"], BP) for s in p_states]),
        jnp.stack([s["shift"][:, 0, :] for s in p_states]),
        jnp.stack([s["lru"][0] for s in p_states]),
        jnp.stack([jnp.swapaxes(s["conv"][0], 0, 1) for s in p_states]),
    )
    s_out = (
        jnp.stack([s["s5"][0].reshape(BS, S5_GROUPS, S5_STATE) for s in s_states]),
        jnp.stack([s["s5"][1].reshape(BS, S5_GROUPS, S5_STATE) for s in s_states]),
        jnp.transpose(jnp.stack([s["rwkv"] for s in s_states]), (0, 4, 1, 2, 3)),
        jnp.stack([s["shift"] for s in s_states]),
        jnp.stack([s["lru"] for s in s_states]),
        jnp.stack([jnp.swapaxes(s["conv"], 0, 1) for s in s_states]),
    )
    return (xp, xs.reshape(BS, 1, D_MODEL)) + p_out + s_out
```

```python
import functools
import math

import jax
import jax.numpy as jnp
from jax import lax
from jax.experimental import pallas as pl
from jax.experimental.pallas import tpu as pltpu

F32 = jnp.float32
BF16 = jnp.bfloat16

D_MODEL = 1024
DEPTH = 2
N_META = 16
D_FF = 2816
NORM_EPS = 1e-6
FFN_RES_SCALE = 0.5
S5_WIDTH = 384
S5_GROUP = 16
S5_GROUPS = 24
S5_STATE = 64
S5_LANES = S5_GROUPS * S5_STATE
RWKV_HEAD = 64
RWKV_WIDTH = 384
RWKV_HEADS = 6
RWKV_PAIRS = 3
DECAY_RANK = 64
ICL_RANK = 64
GATE_RANK = 128
RWKV_PROJ = 1408
RWKV_LN_EPS = 64e-5
LRU_WIDTH = 256
LRU_BLOCKS = 4
CONV_WIDTH = 4
CONV_HIST = CONV_WIDTH - 1
LRU_C = 8.0
D_IN = 2304
SL_IN = S5_WIDTH + 2 * LRU_WIDTH
SL_OUT = S5_WIDTH + LRU_WIDTH
LANE = 128
SUBLANE = 8
S5_GROUPS_PER_BLOCK = LANE // S5_GROUP
S5_BLOCKS = S5_GROUPS // S5_GROUPS_PER_BLOCK
S5_BLOCK_STATES = S5_GROUPS_PER_BLOCK * S5_STATE

V7X_VMEM_BYTES = 64 * 2 ** 20
VMEM_LIMIT = (V7X_VMEM_BYTES * 7) // 8

PROMPT_TOKEN_TILE = 512
WEIGHT_CHUNK = 256
ROW_SUBTILE = 256
PROMPT_CHUNK = 64


def _dot(a, b):
    return jnp.dot(a, b, preferred_element_type=F32)


def _dot_nt(a, b):
    bt = jnp.transpose(b.astype(F32)).astype(BF16)
    return jnp.dot(a, bt, preferred_element_type=F32)


def _dot_tn(a, b):
    return lax.dot_general(a, b, (((0,), (0,)), ((), ())), preferred_element_type=F32)


def _split3(x):
    hi = x.astype(BF16)
    r = x - hi.astype(F32)
    mid = r.astype(BF16)
    lo = (r - mid.astype(F32)).astype(BF16)
    return hi, mid, lo


def _sigmoid(x):
    return 0.5 * (jnp.tanh(0.5 * x) + 1.0)


def _gelu(x):
    c = math.sqrt(2.0 / math.pi)
    return 0.5 * x * (1.0 + jnp.tanh(c * (x + 0.044715 * (x * x * x))))


def _neg_expm1(z):
    t = jnp.tanh(0.5 * z)
    return -2.0 * t / (1.0 - t)


def _softplus(x):
    return jnp.maximum(x, 0.0) + jnp.log1p(jnp.exp(-jnp.abs(x)))


def _rms(x, g):
    return x * lax.rsqrt(jnp.mean(x * x, axis=-1, keepdims=True) + NORM_EPS) * g


def _swiglu_res(x, g_norm, wg_ref, wu_ref, wd_ref):
    xn = _rms(x, g_norm).astype(BF16)
    yield
    g = _dot(xn, wg_ref[...])
    u = _dot(xn, wu_ref[...])
    yield
    a = (g * _sigmoid(g) * u).astype(BF16)
    yield
    d = _dot(a, wd_ref[...])
    yield
    return x + FFN_RES_SCALE * d


def _staggered(gens):
    pending = list(gens)
    active = []
    while pending or active:
        if pending:
            active.append(pending.pop(0))
        for g in list(active):
            try:
                next(g)
            except StopIteration:
                active.remove(g)


def _row_tiles(n_rows):
    sub = ROW_SUBTILE if n_rows % ROW_SUBTILE == 0 else n_rows
    return [slice(r, r + sub) for r in range(0, n_rows, sub)]


def _full_spec(arr):
    nd = arr.ndim
    return pl.BlockSpec(arr.shape, lambda *_: (0,) * nd, pipeline_mode=pl.Buffered(1))


def _layer_spec(arr, l):
    nd = arr.ndim - 1
    return pl.BlockSpec((None,) + arr.shape[1:], lambda *_: (l,) + (0,) * nd, pipeline_mode=pl.Buffered(1))


def _row(ref, l):
    return ref[l:l + 1, :]


def _k1_kernel(x_ref, n1_ref, wg_ref, wu_ref, wd_ref, nm_ref, win_ref, h_ref, prw_ref, psl_ref, *, l):
    def rows_pipeline(rows):
        h = yield from _swiglu_res(x_ref[rows, :], _row(n1_ref, l), wg_ref, wu_ref, wd_ref)
        h_ref[rows, :] = h
        hn = _rms(h, _row(nm_ref, l)).astype(BF16)
        yield
        psl_ref[rows, 0:S5_WIDTH] = _dot(hn, win_ref[:, 0:S5_WIDTH])
        prw_ref[rows, :] = _dot(hn, win_ref[:, S5_WIDTH:S5_WIDTH + RWKV_PROJ])
        psl_ref[rows, S5_WIDTH:SL_IN] = _dot(hn, win_ref[:, S5_WIDTH + RWKV_PROJ:D_IN])

    _staggered([rows_pipeline(rows) for rows in _row_tiles(x_ref.shape[0])])


def _k1(x, w, l, tl):
    B, L, _ = x.shape
    return pl.pallas_call(
        functools.partial(_k1_kernel, l=l),
        grid=(B, L // tl),
        in_specs=[
            pl.BlockSpec((None, tl, D_MODEL), lambda b, i: (b, i, 0)),
            _full_spec(w["n1"]),
            _full_spec(w["wg1"]),
            _full_spec(w["wu1"]),
            _full_spec(w["wd1"]),
            _full_spec(w["nm"]),
            _full_spec(w["win"]),
        ],
        out_specs=[
            pl.BlockSpec((None, tl, D_MODEL), lambda b, i: (b, i, 0)),
            pl.BlockSpec((None, tl, RWKV_PROJ), lambda b, i: (b, i, 0)),
            pl.BlockSpec((tl, SL_IN), lambda b, i: (i, b)),
        ],
        out_shape=[
            jax.ShapeDtypeStruct((B, L, D_MODEL), F32),
            jax.ShapeDtypeStruct((B, L, RWKV_PROJ), F32),
            jax.ShapeDtypeStruct((L, B * SL_IN), F32),
        ],
        compiler_params=pltpu.CompilerParams(
            dimension_semantics=("parallel", "parallel"), vmem_limit_bytes=VMEM_LIMIT),
        name="ffn1_inproj",
    )(x, w["n1"], w["wg1"], w["wu1"], w["wd1"], w["nm"], w["win"])


def _k3_kernel(h_ref, osl_ref, orw_ref, wout_ref, n2_ref, wg_ref, wu_ref, wd_ref, fn_ref, o_ref, *, l, final):
    def rows_pipeline(rows):
        osl = osl_ref[rows, :]
        mix = (_dot(osl[:, 0:S5_WIDTH].astype(BF16), wout_ref[0:S5_WIDTH, :])
               + _dot(orw_ref[rows, :].astype(BF16), wout_ref[S5_WIDTH:S5_WIDTH + RWKV_WIDTH, :])
               + _dot(osl[:, S5_WIDTH:SL_OUT].astype(BF16), wout_ref[S5_WIDTH + RWKV_WIDTH:D_MODEL, :]))
        yield
        h3 = yield from _swiglu_res(h_ref[rows, :] + mix, _row(n2_ref, l), wg_ref, wu_ref, wd_ref)
        if final:
            h3 = _rms(h3, fn_ref[...])
        o_ref[rows, :] = h3

    _staggered([rows_pipeline(rows) for rows in _row_tiles(h_ref.shape[0])])


def _k3(h, osl, orw, w, l, tl, final):
    B, L, _ = h.shape
    return pl.pallas_call(
        functools.partial(_k3_kernel, l=l, final=final),
        grid=(B, L // tl),
        in_specs=[
            pl.BlockSpec((None, tl, D_MODEL), lambda b, i: (b, i, 0)),
            pl.BlockSpec((tl, SL_OUT), lambda b, i: (i, b)),
            pl.BlockSpec((None, tl, RWKV_WIDTH), lambda b, i: (b, i, 0)),
            _full_spec(w["wout"]),
            _full_spec(w["n2"]),
            _full_spec(w["wg2"]),
            _full_spec(w["wu2"]),
            _full_spec(w["wd2"]),
            _full_spec(w["fnorm"]),
        ],
        out_specs=pl.BlockSpec((None, tl, D_MODEL), lambda b, i: (b, i, 0)),
        out_shape=jax.ShapeDtypeStruct((B, L, D_MODEL), F32),
        compiler_params=pltpu.CompilerParams(
            dimension_semantics=("parallel", "parallel"), vmem_limit_bytes=VMEM_LIMIT),
        name="outproj_ffn2",
    )(h, osl, orw, w["wout"], w["n2"], w["wg2"], w["wu2"], w["wd2"], w["fnorm"])


def _stream_spec(arr, l, axis, first, count):
    shape = [None] + list(arr.shape[1:])
    shape[axis] = WEIGHT_CHUNK

    def index(c):
        idx = [l] + [0] * (arr.ndim - 1)
        idx[axis] = jnp.clip(c - first, 0, count - 1)
        return tuple(idx)

    return pl.BlockSpec(tuple(shape), index)


def _cast_spec(rows, cols, axis, first, count):
    shape = [rows, cols]
    shape[axis] = WEIGHT_CHUNK

    def index(c):
        idx = [0, 0]
        idx[axis] = jnp.clip(c - first, 0, count - 1)
        return tuple(idx)

    return pl.BlockSpec(tuple(shape), index)


def _ffn_chunk(c, first, count, xn_scr, acc_scr, wg_ref, wu_ref, wd_ref, wgb_ref, wub_ref, wdb_ref):
    @pl.when((c >= first) & (c < first + count))
    def _():
        wg = wg_ref[...].astype(BF16)
        wu = wu_ref[...].astype(BF16)
        wd = wd_ref[...].astype(BF16)
        wgb_ref[...] = wg
        wub_ref[...] = wu
        wdb_ref[...] = wd
        xn = xn_scr[...]
        g = _dot(xn, wg)
        u = _dot(xn, wu)
        a = (g * _sigmoid(g) * u).astype(BF16)
        acc_scr[...] += _dot(a, wd)


def _k1_stream_kernel(x_ref, n1_ref, nm_ref, wg_ref, wu_ref, wd_ref, win_ref,
                      h_ref, prw_ref, psl_ref, wgb_ref, wub_ref, wdb_ref, winb_ref,
                      xn_scr, acc_scr, proj_scr, *, l, nf, ni):
    c = pl.program_id(0)

    @pl.when(c == 0)
    def _():
        xn_scr[...] = _rms(x_ref[...], _row(n1_ref, l)).astype(BF16)
        acc_scr[...] = jnp.zeros(acc_scr.shape, F32)

    _ffn_chunk(c, 0, nf, xn_scr, acc_scr, wg_ref, wu_ref, wd_ref, wgb_ref, wub_ref, wdb_ref)

    @pl.when(c == nf - 1)
    def _():
        h = x_ref[...] + FFN_RES_SCALE * acc_scr[...]
        h_ref[...] = h
        xn_scr[...] = _rms(h, _row(nm_ref, l)).astype(BF16)

    @pl.when(c >= nf)
    def _():
        win = win_ref[...].astype(BF16)
        winb_ref[...] = win
        col = pl.multiple_of((c - nf) * WEIGHT_CHUNK, WEIGHT_CHUNK)
        proj_scr[:, pl.ds(col, WEIGHT_CHUNK)] = _dot(xn_scr[...], win)

    @pl.when(c == nf + ni - 1)
    def _():
        psl_ref[:, 0:S5_WIDTH] = proj_scr[:, 0:S5_WIDTH]
        prw_ref[...] = proj_scr[:, S5_WIDTH:S5_WIDTH + RWKV_PROJ]
        psl_ref[:, S5_WIDTH:SL_IN] = proj_scr[:, S5_WIDTH + RWKV_PROJ:D_IN]


def _k1_stream(x, a, l):
    NS = x.shape[0]
    nf, ni = D_FF // WEIGHT_CHUNK, D_IN // WEIGHT_CHUNK
    const = lambda shape: pl.BlockSpec(shape, lambda c: (0,) * len(shape))
    return pl.pallas_call(
        functools.partial(_k1_stream_kernel, l=l, nf=nf, ni=ni),
        grid=(nf + ni,),
        in_specs=[
            const((NS, D_MODEL)),
            _full_spec(a["ffn1_norm"]),
            _full_spec(a["mix_norm"]),
            _stream_spec(a["ffn1_w_gate"], l, 2, 0, nf),
            _stream_spec(a["ffn1_w_up"], l, 2, 0, nf),
            _stream_spec(a["ffn1_w_down"], l, 1, 0, nf),
            _stream_spec(a["w_in"], l, 2, nf, ni),
        ],
        out_specs=[
            const((NS, D_MODEL)),
            const((NS, RWKV_PROJ)),
            const((NS, SL_IN)),
            _cast_spec(D_MODEL, D_FF, 1, 0, nf),
            _cast_spec(D_MODEL, D_FF, 1, 0, nf),
            _cast_spec(D_FF, D_MODEL, 0, 0, nf),
            _cast_spec(D_MODEL, D_IN, 1, nf, ni),
        ],
        out_shape=[
            jax.ShapeDtypeStruct((NS, D_MODEL), F32),
            jax.ShapeDtypeStruct((NS, RWKV_PROJ), F32),
            jax.ShapeDtypeStruct((NS, SL_IN), F32),
            jax.ShapeDtypeStruct((D_MODEL, D_FF), BF16),
            jax.ShapeDtypeStruct((D_MODEL, D_FF), BF16),
            jax.ShapeDtypeStruct((D_FF, D_MODEL), BF16),
            jax.ShapeDtypeStruct((D_MODEL, D_IN), BF16),
        ],
        scratch_shapes=[
            pltpu.VMEM((NS, D_MODEL), BF16),
            pltpu.VMEM((NS, D_MODEL), F32),
            pltpu.VMEM((NS, D_IN), F32),
        ],
        compiler_params=pltpu.CompilerParams(dimension_semantics=("arbitrary",), vmem_limit_bytes=VMEM_LIMIT),
        name="ffn1_inproj_stream",
    )(x, a["ffn1_norm"], a["mix_norm"], a["ffn1_w_gate"], a["ffn1_w_up"], a["ffn1_w_down"], a["w_in"])


def _k3_stream_kernel(h_ref, osl_ref, orw_ref, n2_ref, fn_ref, wout_ref, wg_ref, wu_ref, wd_ref,
                      o_ref, woutb_ref, wgb_ref, wub_ref, wdb_ref,
                      mix_scr, h2_scr, xn_scr, acc_scr, *, l, no, nf, final):
    c = pl.program_id(0)

    @pl.when(c == 0)
    def _():
        osl = osl_ref[...]
        mix_scr[...] = jnp.concatenate(
            [osl[:, 0:S5_WIDTH], orw_ref[...], osl[:, S5_WIDTH:SL_OUT]], axis=-1).astype(BF16)
        h2_scr[...] = h_ref[...]

    @pl.when(c < no)
    def _():
        wout = wout_ref[...].astype(BF16)
        woutb_ref[...] = wout
        row = pl.multiple_of(c * WEIGHT_CHUNK, WEIGHT_CHUNK)
        h2_scr[...] += _dot(mix_scr[:, pl.ds(row, WEIGHT_CHUNK)], wout)

    @pl.when(c == no - 1)
    def _():
        xn_scr[...] = _rms(h2_scr[...], _row(n2_ref, l)).astype(BF16)
        acc_scr[...] = jnp.zeros(acc_scr.shape, F32)

    _ffn_chunk(c, no, nf, xn_scr, acc_scr, wg_ref, wu_ref, wd_ref, wgb_ref, wub_ref, wdb_ref)

    @pl.when(c == no + nf - 1)
    def _():
        h3 = h2_scr[...] + FFN_RES_SCALE * acc_scr[...]
        if final:
            h3 = _rms(h3, fn_ref[...])
        o_ref[...] = h3


def _k3_stream(h, osl, orw, a, fnorm, l, final):
    NS = h.shape[0]
    no, nf = D_MODEL // WEIGHT_CHUNK, D_FF // WEIGHT_CHUNK
    const = lambda shape: pl.BlockSpec(shape, lambda c: (0,) * len(shape))
    return pl.pallas_call(
        functools.partial(_k3_stream_kernel, l=l, no=no, nf=nf, final=final),
        grid=(no + nf,),
        in_specs=[
            const((NS, D_MODEL)),
            const((NS, SL_OUT)),
            const((NS, RWKV_WIDTH)),
            _full_spec(a["ffn2_norm"]),
            _full_spec(fnorm),
            _stream_spec(a["w_out"], l, 1, 0, no),
            _stream_spec(a["ffn2_w_gate"], l, 2, no, nf),
            _stream_spec(a["ffn2_w_up"], l, 2, no, nf),
            _stream_spec(a["ffn2_w_down"], l, 1, no, nf),
        ],
        out_specs=[
            const((NS, D_MODEL)),
            _cast_spec(D_MODEL, D_MODEL, 0, 0, no),
            _cast_spec(D_MODEL, D_FF, 1, no, nf),
            _cast_spec(D_MODEL, D_FF, 1, no, nf),
            _cast_spec(D_FF, D_MODEL, 0, no, nf),
        ],
        out_shape=[
            jax.ShapeDtypeStruct((NS, D_MODEL), F32),
            jax.ShapeDtypeStruct((D_MODEL, D_MODEL), BF16),
            jax.ShapeDtypeStruct((D_MODEL, D_FF), BF16),
            jax.ShapeDtypeStruct((D_MODEL, D_FF), BF16),
            jax.ShapeDtypeStruct((D_FF, D_MODEL), BF16),
        ],
        scratch_shapes=[
            pltpu.VMEM((NS, D_MODEL), BF16),
            pltpu.VMEM((NS, D_MODEL), F32),
            pltpu.VMEM((NS, D_MODEL), BF16),
            pltpu.VMEM((NS, D_MODEL), F32),
        ],
        compiler_params=pltpu.CompilerParams(dimension_semantics=("arbitrary",), vmem_limit_bytes=VMEM_LIMIT),
        name="outproj_ffn2_stream",
    )(h, osl, orw, a["ffn2_norm"], fnorm, a["w_out"], a["ffn2_w_gate"], a["ffn2_w_up"], a["ffn2_w_down"])


def _s5_prep_kernel(lr_ref, li_ref, ldt_ref, br_ref, bi_ref, abr_ref, abi_ref, bbr_ref, bbi_ref):
    lr = lr_ref[...]
    li = li_ref[...]
    dt = jnp.exp(ldt_ref[...])
    mag = jnp.exp(lr * dt)
    ab_re = mag * jnp.cos(li * dt)
    ab_im = mag * jnp.sin(li * dt)
    den = lr * lr + li * li
    f_re = ((ab_re - 1.0) * lr + ab_im * li) / den
    f_im = (ab_im * lr - (ab_re - 1.0) * li) / den
    abr_ref[...] = ab_re
    abi_ref[...] = ab_im
    br = br_ref[...]
    bi = bi_ref[...]
    bbr_ref[...] = f_re[:, :, None, :] * br - f_im[:, :, None, :] * bi
    bbi_ref[...] = f_re[:, :, None, :] * bi + f_im[:, :, None, :] * br


def _s5_prep(lam_re, lam_im, log_dt, b_re, b_im):
    G, N, C = S5_GROUPS, S5_STATE, S5_GROUP
    ldt = jnp.broadcast_to(log_dt[:, :, None], (DEPTH, G, N))
    br = jnp.swapaxes(b_re, 2, 3)
    bi = jnp.swapaxes(b_im, 2, 3)
    return pl.pallas_call(
        _s5_prep_kernel,
        out_shape=[
            jax.ShapeDtypeStruct((DEPTH, G, N), F32),
            jax.ShapeDtypeStruct((DEPTH, G, N), F32),
            jax.ShapeDtypeStruct((DEPTH, G, C, N), F32),
            jax.ShapeDtypeStruct((DEPTH, G, C, N), F32),
        ],
        name="s5_discretise",
    )(lam_re, lam_im, ldt, br, bi)


def _s5lru_kernel(psl_ref, ab_ref, win_ref, cre_ref, cim_ref, d_ref, gw_ref, gb_ref,
                  cw_ref, cb_ref, wa_ref, wx_ref, ba_ref, bx_ref, lam_ref,
                  h0_ref, l0_ref, c0_ref,
                  o_ref, hT_ref, lT_ref, cT_ref,
                  hs_scr, ls_scr, xc_scr, xr_scr, xi_scr, la_scr, lb_scr, pin_scr, pout_scr, *, l, B, Tc, unroll):
    c = pl.program_id(0)
    R = Tc * B

    @pl.when(c == 0)
    def _():
        hs_scr[...] = h0_ref[...]
        ls_scr[...] = l0_ref[...]
        xc_scr[0:CONV_HIST] = c0_ref[...]

    for b in range(B):
        for j in range(SL_IN // LANE):
            pin_scr[j, pl.ds(b, Tc, stride=B), :] = psl_ref[:, SL_IN * b + LANE * j:SL_IN * b + LANE * (j + 1)]
    psl = jnp.concatenate([pin_scr[j] for j in range(SL_IN // LANE)], axis=-1).reshape(Tc, B, SL_IN)
    u = psl[:, :, 0:S5_WIDTH].reshape(R, S5_WIDTH)
    gate_in = psl[:, :, S5_WIDTH + LRU_WIDTH:SL_IN].reshape(R, LRU_WIDTH)

    ub = u.astype(BF16)
    half = S5_BLOCK_STATES
    for j in range(S5_BLOCKS):
        x = _dot(ub[:, LANE * j:LANE * (j + 1)], win_ref[j])
        xr_scr[:, :, half * j:half * (j + 1)] = x[:, 0:half].reshape(Tc, B, half)
        xi_scr[:, :, half * j:half * (j + 1)] = x[:, half:2 * half].reshape(Tc, B, half)

    xc_scr[CONV_HIST:CONV_HIST + Tc] = psl[:, :, S5_WIDTH:S5_WIDTH + LRU_WIDTH]
    cw = cw_ref[...]
    xc = _row(cb_ref, l) + xc_scr[0:Tc] * cw[0:1]
    for j in range(1, CONV_WIDTH):
        xc = xc + xc_scr[j:j + Tc] * cw[j:j + 1]
    new_hist = xc_scr[Tc:Tc + CONV_HIST]
    xc_scr[0:CONV_HIST] = new_hist
    xc2 = xc.reshape(R, LRU_WIDTH)
    xcb = xc2.astype(BF16)
    gate_a = _sigmoid(_dot(xcb, wa_ref[...]) + _row(ba_ref, l))
    gate_x = _sigmoid(_dot(xcb, wx_ref[...]) + _row(bx_ref, l))
    log_a = LRU_C * gate_a * (-_softplus(-_row(lam_ref, l)))
    la_scr[...] = jnp.exp(log_a).reshape(Tc, B, LRU_WIDTH)
    lb_scr[...] = (jnp.sqrt(_neg_expm1(2.0 * log_a)) * (gate_x * xc2)).reshape(Tc, B, LRU_WIDTH)

    ar = jnp.broadcast_to(ab_ref[0:1, :], (B, S5_LANES))
    ai = jnp.broadcast_to(ab_ref[1:2, :], (B, S5_LANES))

    def step(t, carry):
        hr, hi, hl = carry
        nr = ar * hr - ai * hi + xr_scr[t]
        ni = ar * hi + ai * hr + xi_scr[t]
        xr_scr[t] = nr
        xi_scr[t] = ni
        nl = la_scr[t] * hl + lb_scr[t]
        lb_scr[t] = nl
        return nr, ni, nl

    hr, hi, hl = lax.fori_loop(0, Tc, step, (hs_scr[0], hs_scr[1], ls_scr[...]), unroll=unroll)
    hs_scr[0] = hr
    hs_scr[1] = hi
    ls_scr[...] = hl

    hrb = xr_scr[...].reshape(R, S5_LANES).astype(BF16)
    hib = xi_scr[...].reshape(R, S5_LANES).astype(BF16)
    ys = []
    for j in range(S5_BLOCKS):
        sl = slice(half * j, half * (j + 1))
        ys.append(_dot(hrb[:, sl], cre_ref[j]) - _dot(hib[:, sl], cim_ref[j]))
    y = jnp.concatenate(ys, axis=-1) + _row(d_ref, l) * u
    z = _gelu(y)
    o_s5 = z * _sigmoid(_dot(z.astype(BF16), gw_ref[...].astype(BF16)) + _row(gb_ref, l))
    o_lru = lb_scr[...].reshape(R, LRU_WIDTH) * _gelu(gate_in)
    out = jnp.concatenate([o_s5, o_lru], axis=-1)
    for j in range(SL_OUT // LANE):
        pout_scr[j] = out[:, LANE * j:LANE * (j + 1)]
    for b in range(B):
        for j in range(SL_OUT // LANE):
            o_ref[:, SL_OUT * b + LANE * j:SL_OUT * b + LANE * (j + 1)] = pout_scr[j, pl.ds(b, Tc, stride=B), :]

    @pl.when(c == pl.num_programs(0) - 1)
    def _():
        hT_ref[...] = hs_scr[...]
        lT_ref[...] = ls_scr[...]
        cT_ref[...] = xc_scr[0:CONV_HIST]


def _s5lru(psl2, B, w, l, st, Tc):
    L = psl2.shape[0]
    li = st["li"]
    unroll = True
    return pl.pallas_call(
        functools.partial(_s5lru_kernel, l=l, B=B, Tc=Tc, unroll=unroll),
        grid=(L // Tc,),
        in_specs=[
            pl.BlockSpec((Tc, B * SL_IN), lambda c: (c, 0)),
            _layer_spec(w["s5_ab"], l),
            _layer_spec(w["s5_win"], l),
            _layer_spec(w["s5_cre"], l),
            _layer_spec(w["s5_cim"], l),
            _full_spec(w["s5_d"]),
            _layer_spec(w["s5_glu_w"], l),
            _full_spec(w["s5_glu_b"]),
            _layer_spec(w["lru_conv_w"], l),
            _full_spec(w["lru_conv_b"]),
            _layer_spec(w["lru_wa"], l),
            _layer_spec(w["lru_wx"], l),
            _full_spec(w["lru_b_a"]),
            _full_spec(w["lru_b_x"]),
            _full_spec(w["lru_lambda"]),
            _layer_spec(st["s5"], li),
            _layer_spec(st["lru"], li),
            _layer_spec(st["conv"], li),
        ],
        out_specs=[
            pl.BlockSpec((Tc, B * SL_OUT), lambda c: (c, 0)),
            pl.BlockSpec((2, B, S5_LANES), lambda c: (0, 0, 0)),
            pl.BlockSpec((B, LRU_WIDTH), lambda c: (0, 0)),
            pl.BlockSpec((CONV_HIST, B, LRU_WIDTH), lambda c: (0, 0, 0)),
        ],
        out_shape=[
            jax.ShapeDtypeStruct((L, B * SL_OUT), F32),
            jax.ShapeDtypeStruct((2, B, S5_LANES), F32),
            jax.ShapeDtypeStruct((B, LRU_WIDTH), F32),
            jax.ShapeDtypeStruct((CONV_HIST, B, LRU_WIDTH), F32),
        ],
        scratch_shapes=[
            pltpu.VMEM((2, B, S5_LANES), F32),
            pltpu.VMEM((B, LRU_WIDTH), F32),
            pltpu.VMEM((Tc + CONV_HIST, B, LRU_WIDTH), F32),
            pltpu.VMEM((Tc, B, S5_LANES), F32),
            pltpu.VMEM((Tc, B, S5_LANES), F32),
            pltpu.VMEM((Tc, B, LRU_WIDTH), F32),
            pltpu.VMEM((Tc, B, LRU_WIDTH), F32),
            pltpu.VMEM((SL_IN // LANE, Tc * B, LANE), F32),
            pltpu.VMEM((SL_OUT // LANE, Tc * B, LANE), F32),
        ],
        compiler_params=pltpu.CompilerParams(
            dimension_semantics=("arbitrary",), vmem_limit_bytes=VMEM_LIMIT),
        name="s5_rglru_scan",
    )(psl2, w["s5_ab"], w["s5_win"], w["s5_cre"], w["s5_cim"], w["s5_d"], w["s5_glu_w"], w["s5_glu_b"],
      w["lru_conv_w"], w["lru_conv_b"], w["lru_wa"], w["lru_wx"], w["lru_b_a"], w["lru_b_x"], w["lru_lambda"],
      st["s5"], st["lru"], st["conv"])


_RWKV_PARAMS = ("rwkv_mu", "rwkv_w0", "rwkv_w_up", "rwkv_a0", "rwkv_a_up", "rwkv_g_up", "rwkv_k_k", "rwkv_k_a",
                "rwkv_r_k", "rwkv_ln_w", "rwkv_ln_b", "rwkv_ones")


def _rwkv_param_specs(w, l):
    layered = ("rwkv_w_up", "rwkv_a_up", "rwkv_g_up")
    return [_layer_spec(w[k], l) if k in layered else _full_spec(w[k]) for k in _RWKV_PARAMS]


def _rwkv_mats(prm):
    wup_ref, aup_ref, gup_ref, ones_ref = prm[2], prm[4], prm[5], prm[11]
    zpad = jnp.zeros((DECAY_RANK, RWKV_WIDTH), BF16)
    wup = jnp.concatenate([wup_ref[...].astype(BF16), zpad], axis=0)
    aup = jnp.concatenate([zpad, aup_ref[...].astype(BF16)], axis=0)
    return wup, aup, gup_ref[...].astype(BF16), ones_ref[...]


def _segsum(x, ones):
    xb = x.astype(BF16)
    return jnp.concatenate([_dot(xb[:, LANE * p:LANE * (p + 1)], ones) for p in range(RWKV_PAIRS)], axis=-1)


def _rwkv_token_math(p, prev, prm, mats, l):
    mu_ref, w0_ref, _, a0_ref, _, _, kk_ref, ka_ref, rk_ref = prm[0:9]
    wup, aup, gup, ones = mats
    W = RWKV_WIDTH
    xm = p + (prev - p) * _row(mu_ref, l)
    r = xm[:, 0:W]
    k = xm[:, W:2 * W]
    v = xm[:, 2 * W:3 * W]
    xwa = xm[:, 3 * W:3 * W + DECAY_RANK + ICL_RANK]
    xg = xm[:, 3 * W + DECAY_RANK + ICL_RANK:RWKV_PROJ]
    lw = _row(w0_ref, l) + _dot(jnp.tanh(xwa).astype(BF16), wup)
    ld = -math.exp(-0.5) * _sigmoid(lw)
    a = _sigmoid(_row(a0_ref, l) + _dot(xwa.astype(BF16), aup))
    g = _dot(_sigmoid(xg).astype(BF16), gup)
    kkr = k * _row(kk_ref, l)
    kk = kkr * lax.rsqrt(jnp.maximum(_segsum(kkr * kkr, ones), 1e-24))
    k2 = k * (1.0 + (a - 1.0) * _row(ka_ref, l))
    bonus = _segsum(r * k2 * _row(rk_ref, l), ones) * v
    return r, k2, v, kk, a, ld, g, bonus


def _rwkv_out(y, bonus, g, prm, ones, l):
    lnw_ref, lnb_ref = prm[9], prm[10]
    inv = 1.0 / RWKV_HEAD
    mean = _segsum(y, ones) * inv
    d = y - mean
    var = _segsum(d * d, ones) * inv
    yn = d * lax.rsqrt(var + RWKV_LN_EPS) * _row(lnw_ref, l) + _row(lnb_ref, l)
    return (yn + bonus) * g


RWKV_GROUP_HEADS = 4
RWKV_GROUP_LANES = RWKV_GROUP_HEADS * RWKV_HEAD


def _rwkv_groups(B):
    full = [[(b, 0, RWKV_GROUP_LANES)] for b in range(B)]
    rest = RWKV_WIDTH - RWKV_GROUP_LANES
    tail = [[(b, RWKV_GROUP_LANES, rest), (b + 1, RWKV_GROUP_LANES, rest)] for b in range(0, B, 2)]
    return full + tail


def _rwkv_chunk_kernel(*refs, l, B, Tc):
    n_prm = len(_RWKV_PARAMS)
    p_ref, s0_ref, sh0_ref = refs[0:3]
    prm = refs[3:3 + n_prm]
    tri_ref, o_ref, sT_ref = refs[3 + n_prm:6 + n_prm]
    st_scr, xs_scr, y_scr, pt_scr = refs[6 + n_prm:10 + n_prm]
    src = dict(zip(("a", "b", "k", "r", "v", "bp", "kp"), refs[10 + n_prm:]))
    c = pl.program_id(0)
    R = B * Tc
    W = RWKV_WIDTH
    GL = RWKV_GROUP_LANES
    GH = RWKV_GROUP_HEADS
    carry_row = SUBLANE - 1

    @pl.when(c == 0)
    def _():
        st_scr[...] = s0_ref[...]
        xs_scr[:, carry_row:SUBLANE, :] = sh0_ref[...]

    ps, prevs = [], []
    for b in range(B):
        pb = p_ref[b]
        xs_scr[b, SUBLANE:SUBLANE + Tc, :] = pb
        prevs.append(xs_scr[b, carry_row:carry_row + Tc, :])
        xs_scr[b, carry_row:SUBLANE, :] = pb[Tc - 1:Tc, :]
        ps.append(pb)
    mats = _rwkv_mats(prm)
    r, k2, v, kk, a, ld, g, bonus = _rwkv_token_math(
        jnp.concatenate(ps, axis=0), jnp.concatenate(prevs, axis=0), prm, mats, l)

    h1, h2, h3 = _split3(ld)
    tri = tri_ref[...]
    cls = []
    for b in range(B):
        rows = slice(b * Tc, (b + 1) * Tc)
        cls.append(_dot(tri, h1[rows]) + _dot(tri, h2[rows]) + _dot(tri, h3[rows]))
    cl = jnp.concatenate(cls, axis=0)
    pt = jnp.exp(cl.reshape(B, Tc, W)[:, Tc - 1:Tc, :])
    inv_p = jnp.exp(-cl)
    to_end = (pt * inv_p.reshape(B, Tc, W)).reshape(R, W)
    beta = kk * a
    src["a"][...] = (-kk * jnp.exp(cl - ld)).astype(BF16)
    src["b"][...] = (beta * inv_p).astype(BF16)
    src["k"][...] = (k2 * inv_p).astype(BF16)
    src["r"][...] = (r * jnp.exp(cl)).astype(BF16)
    src["v"][...] = v.astype(BF16)
    src["bp"][...] = (beta * to_end).astype(BF16)
    src["kp"][...] = (k2 * to_end).astype(BF16)
    pt_scr[...] = pt

    assert Tc & (Tc - 1) == 0
    n_doubling = max(1, (Tc - 1).bit_length())
    groups = _rwkv_groups(B)

    def block_masks(blk):
        lane = lax.broadcasted_iota(jnp.int32, (1, GH * blk), 1)
        return [((lane >= blk * h) & (lane < blk * (h + 1))).astype(BF16) for h in range(GH)]

    head_masks = block_masks(RWKV_HEAD)
    time_masks = block_masks(Tc)

    def stack(x, masks):
        return jnp.concatenate([x * m for m in masks], axis=0)

    ti = lax.broadcasted_iota(jnp.int32, (Tc, GH * Tc), 0)
    si = lax.broadcasted_iota(jnp.int32, (Tc, GH * Tc), 1) & (Tc - 1)
    strict = (si < ti).astype(BF16)
    incl = (si <= ti).astype(BF16)
    same_head = ((lax.broadcasted_iota(jnp.int32, (GL, GL), 0) // RWKV_HEAD)
                 == (lax.broadcasted_iota(jnp.int32, (GL, GL), 1) // RWKV_HEAD)).astype(F32)

    def operand(name, grp):
        parts = [src[name][b * Tc:(b + 1) * Tc, lo:lo + n] for b, lo, n in grp]
        return parts[0] if len(parts) == 1 else jnp.concatenate(parts, axis=-1)

    v4s, gss, ns, wrbs, wrks, xs = [], [], [], [], [], []
    for gi, grp in enumerate(groups):
        lhs = jnp.concatenate([operand("a", grp), operand("r", grp)], axis=0)
        v4 = stack(operand("v", grp), head_masks)
        gb = _dot_nt(lhs, stack(operand("b", grp), head_masks)).astype(BF16)
        gk = _dot_nt(lhs, stack(operand("k", grp), head_masks)).astype(BF16)
        gs = _dot_nt(lhs, st_scr[gi].astype(BF16))
        v4s.append(v4)
        gss.append(gs[Tc:2 * Tc])
        ns.append(gb[0:Tc] * strict)
        wrbs.append(gb[Tc:2 * Tc] * incl)
        wrks.append(gk[Tc:2 * Tc] * incl)
        xs.append(gs[0:Tc] + _dot(gk[0:Tc] * strict, v4))
    for j in range(n_doubling):
        for i in range(len(groups)):
            xs[i] = xs[i] + _dot(ns[i], stack(xs[i].astype(BF16), head_masks))
        if j + 1 < n_doubling:
            for i in range(len(groups)):
                ns[i] = _dot(ns[i], stack(ns[i], time_masks)).astype(BF16)
    for gi, grp in enumerate(groups):
        sab = xs[gi].astype(BF16)
        y = gss[gi] + _dot(wrbs[gi], stack(sab, head_masks)) + _dot(wrks[gi], v4s[gi])
        lane0 = 0
        for b, lo, n in grp:
            y_scr[b * Tc:(b + 1) * Tc, lo:lo + n] = y[:, lane0:lane0 + n]
            lane0 += n
        upd = _dot_tn(jnp.concatenate([sab, operand("v", grp)], axis=0),
                      jnp.concatenate([operand("bp", grp), operand("kp", grp)], axis=0))
        ptg = [pt_scr[b, :, lo:lo + n] for b, lo, n in grp]
        ptg = ptg[0] if len(ptg) == 1 else jnp.concatenate(ptg, axis=-1)
        st_scr[gi] = st_scr[gi] * ptg + upd * same_head

    out = _rwkv_out(y_scr[...], bonus, g, prm, mats[3], l)
    for b in range(B):
        o_ref[b] = out[b * Tc:(b + 1) * Tc]

    @pl.when(c == pl.num_programs(0) - 1)
    def _():
        sT_ref[...] = st_scr[...]


def _rwkv_chunk(prw, w, l, s0, sh0, Tc):
    B, L, _ = prw.shape
    R = B * Tc
    W = RWKV_WIDTH
    GL = RWKV_GROUP_LANES
    NG = len(_rwkv_groups(B))
    tri = jnp.tril(jnp.ones((Tc, Tc), F32)).astype(BF16)
    return pl.pallas_call(
        functools.partial(_rwkv_chunk_kernel, l=l, B=B, Tc=Tc),
        grid=(L // Tc,),
        in_specs=[
            pl.BlockSpec((B, Tc, RWKV_PROJ), lambda c: (0, c, 0)),
            _full_spec(s0),
            _full_spec(sh0),
        ] + _rwkv_param_specs(w, l) + [_full_spec(tri)],
        out_specs=[
            pl.BlockSpec((B, Tc, W), lambda c: (0, c, 0)),
            pl.BlockSpec((NG, GL, GL), lambda c: (0, 0, 0)),
        ],
        out_shape=[
            jax.ShapeDtypeStruct((B, L, W), F32),
            jax.ShapeDtypeStruct((NG, GL, GL), F32),
        ],
        scratch_shapes=[
            pltpu.VMEM((NG, GL, GL), F32),
            pltpu.VMEM((B, Tc + SUBLANE, RWKV_PROJ), F32),
            pltpu.VMEM((R, W), F32),
            pltpu.VMEM((B, 1, W), F32),
        ] + [pltpu.VMEM((R, W), BF16)] * 7,
        compiler_params=pltpu.CompilerParams(
            dimension_semantics=("arbitrary",), vmem_limit_bytes=VMEM_LIMIT),
        name="rwkv7_chunked",
    )(prw, s0, sh0, *[w[k] for k in _RWKV_PARAMS], tri)


def _rwkv_step_kernel(*refs, l):
    n_prm = len(_RWKV_PARAMS)
    p_ref, sh_ref, s_ref = refs[0:3]
    prm = refs[3:3 + n_prm]
    o_ref, sn_ref = refs[3 + n_prm:5 + n_prm]
    nkk_t, w_t, beta_t, v_t, k2_t, r_t, y_t, g_scr, bonus_scr = refs[5 + n_prm:]
    h = pl.program_id(0)
    N = RWKV_HEAD

    @pl.when(h == 0)
    def _():
        r, k2, v, kk, a, ld, g, bonus = _rwkv_token_math(p_ref[...], sh_ref[...], prm, _rwkv_mats(prm), l)
        nkk_t[...] = (-kk).T
        w_t[...] = jnp.exp(ld).T
        beta_t[...] = (kk * a).T
        v_t[...] = v.T
        k2_t[...] = k2.T
        r_t[...] = r.T
        g_scr[...] = g
        bonus_scr[...] = bonus

    rows = pl.ds(pl.multiple_of(h * N, N), N)
    s = s_ref[...]
    sa = jnp.sum(s * nkk_t[rows, :][None], axis=1)
    sn = s * w_t[rows, :][None] + sa[:, None, :] * beta_t[rows, :][None] + v_t[rows, :][:, None, :] * k2_t[rows, :][None]
    sn_ref[...] = sn
    y_t[rows, :] = jnp.sum(sn * r_t[rows, :][None], axis=1)

    @pl.when(h == pl.num_programs(0) - 1)
    def _():
        o_ref[...] = _rwkv_out(y_t[...].T, bonus_scr[...], g_scr[...], prm, prm[11][...], l)


def _rwkv_step(prw, w, l, s_all, sh_all):
    B = prw.shape[0]
    W = RWKV_WIDTH
    N = RWKV_HEAD
    return pl.pallas_call(
        functools.partial(_rwkv_step_kernel, l=l),
        grid=(RWKV_HEADS,),
        in_specs=[
            _full_spec(prw),
            _layer_spec(sh_all, l),
            pl.BlockSpec((None, None, N, N, B), lambda h: (l, h, 0, 0, 0)),
        ] + _rwkv_param_specs(w, l),
        out_specs=[
            pl.BlockSpec((B, W), lambda h: (0, 0)),
            pl.BlockSpec((None, N, N, B), lambda h: (h, 0, 0, 0)),
        ],
        out_shape=[
            jax.ShapeDtypeStruct((B, W), F32),
            jax.ShapeDtypeStruct((RWKV_HEADS, N, N, B), F32),
        ],
        scratch_shapes=[pltpu.VMEM((W, B), F32)] * 7 + [pltpu.VMEM((B, W), F32)] * 2,
        compiler_params=pltpu.CompilerParams(
            dimension_semantics=("arbitrary",), vmem_limit_bytes=VMEM_LIMIT),
        name="rwkv7_step",
    )(prw, sh_all, s_all, *[w[k] for k in _RWKV_PARAMS])


def _block_diag(blocks):
    *lead, n, r, c = blocks.shape
    eye = jnp.eye(n, dtype=blocks.dtype)
    out = blocks[..., :, :, None, :] * eye[:, None, :, None]
    return out.reshape(*lead, n * r, n * c)


def _prepare_weights(a):
    G, C, N = S5_GROUPS, S5_GROUP, S5_STATE
    abr, abi, bbr, bbi = _s5_prep(a["s5_lambda_re"], a["s5_lambda_im"], a["s5_log_dt"], a["s5_b_re"], a["s5_b_im"])

    def in_side(bb):
        return _block_diag(bb.reshape(DEPTH, S5_BLOCKS, S5_GROUPS_PER_BLOCK, C, N))

    def out_side(cc):
        ct = jnp.swapaxes(cc, 2, 3)
        return _block_diag(ct.reshape(DEPTH, S5_BLOCKS, S5_GROUPS_PER_BLOCK, N, C))

    w = {k: a[k] for k in (
        "s5_d", "s5_glu_w", "s5_glu_b", "lru_conv_w", "lru_conv_b", "lru_b_a", "lru_b_x", "lru_lambda",
        "rwkv_mu", "rwkv_w0", "rwkv_w_up", "rwkv_a0", "rwkv_a_up", "rwkv_g_up", "rwkv_k_k", "rwkv_k_a",
        "rwkv_ln_w", "rwkv_ln_b")}
    w.update({
        "n1": a["ffn1_norm"], "nm": a["mix_norm"], "n2": a["ffn2_norm"],
        "fnorm": a["final_norm"].reshape(1, D_MODEL),
        "s5_ab": jnp.stack([abr.reshape(DEPTH, S5_LANES), abi.reshape(DEPTH, S5_LANES)], axis=1),
        "s5_win": jnp.concatenate([in_side(bbr), in_side(bbi)], axis=-1).astype(BF16),
        "s5_cre": out_side(a["s5_c_re"]).astype(BF16),
        "s5_cim": out_side(a["s5_c_im"]).astype(BF16),
        "lru_wa": _block_diag(a["lru_w_a"]).astype(BF16),
        "lru_wx": _block_diag(a["lru_w_x"]).astype(BF16),
        "rwkv_r_k": a["rwkv_r_k"].reshape(DEPTH, RWKV_WIDTH),
        "rwkv_ones": _block_diag(jnp.ones((2, RWKV_HEAD, RWKV_HEAD), F32)).astype(BF16),
    })
    return w


def _groups_to_heads(sg, B):
    N = RWKV_HEAD
    diag = jnp.stack([sg[:, N * i:N * (i + 1), N * i:N * (i + 1)] for i in range(RWKV_GROUP_HEADS)], axis=1)
    first = diag[:B]
    rest = diag[B:].reshape(B, RWKV_HEADS - RWKV_GROUP_HEADS, N, N)
    return jnp.concatenate([first, rest], axis=1)


def _seq_mixers(prw, psl2, st, w, l, Tc):
    L = prw.shape[1]
    osl, hT, lT, cT = _s5lru(psl2, prw.shape[0], w, l, st, Tc)
    orw, sT = _rwkv_chunk(prw, w, l, st["rwkv"], st["shift"], Tc)
    new = {"s5": hT[None], "lru": lT[None], "conv": cT[None], "li": 0,
           "rwkv": sT, "shift": prw[:, L - 1:L, :]}
    return osl, orw, new


def kernel(x_prompt, x_sample, state_s5_re, state_s5_im, state_rwkv, state_rwkv_shift, state_lru, state_lru_conv, meta_tokens, ffn1_norm, ffn1_w_gate, ffn1_w_up, ffn1_w_down, mix_norm, w_in, s5_lambda_re, s5_lambda_im, s5_log_dt, s5_b_re, s5_b_im, s5_c_re, s5_c_im, s5_d, s5_glu_w, s5_glu_b, rwkv_mu, rwkv_w0, rwkv_w_up, rwkv_a0, rwkv_a_up, rwkv_g_up, rwkv_k_k, rwkv_k_a, rwkv_r_k, rwkv_ln_w, rwkv_ln_b, lru_conv_w, lru_conv_b, lru_w_a, lru_b_a, lru_w_x, lru_b_x, lru_lambda, w_out, ffn2_norm, ffn2_w_gate, ffn2_w_up, ffn2_w_down, final_norm):
    a = dict(locals())
    w = _prepare_weights(a)
    BP, SEQ, _ = x_prompt.shape
    BS = x_sample.shape[0]
    assert SEQ % PROMPT_TOKEN_TILE == 0 and SEQ % PROMPT_CHUNK == 0 and BP % 2 == 0

    zero = {
        "s5": jnp.zeros((1, 2, BP, S5_LANES), F32),
        "lru": jnp.zeros((1, BP, LRU_WIDTH), F32),
        "conv": jnp.zeros((1, CONV_HIST, BP, LRU_WIDTH), F32),
        "li": 0,
        "rwkv": jnp.zeros((len(_rwkv_groups(BP)), RWKV_GROUP_LANES, RWKV_GROUP_LANES), F32),
        "shift": jnp.zeros((BP, 1, RWKV_PROJ), F32),
    }
    sample_in = {
        "s5": jnp.stack([state_s5_re.reshape(DEPTH, BS, S5_LANES), state_s5_im.reshape(DEPTH, BS, S5_LANES)], axis=1),
        "lru": state_lru,
        "conv": jnp.swapaxes(state_lru_conv, 1, 2),
    }

    rwkv_in = jnp.transpose(state_rwkv, (0, 2, 3, 4, 1))

    xp = x_prompt
    xs = jnp.concatenate([meta_tokens.astype(F32), x_sample.reshape(BS, D_MODEL)], axis=0)
    p_states = []
    s_states = []
    for l in range(DEPTH):
        final = l == DEPTH - 1
        hs, prw_s, psl_s, w["wg1"], w["wu1"], w["wd1"], w["win"] = _k1_stream(xs, a, l)
        prw_m = jnp.broadcast_to(prw_s[None, :N_META], (BP, N_META, RWKV_PROJ))
        psl_m = jnp.tile(psl_s[:N_META], (1, BP))
        osl_m, orw_m, st_meta = _seq_mixers(prw_m, psl_m, zero, w, l, N_META)

        st_l = dict(sample_in, li=l)
        osl_s, hT, lT, cT = _s5lru(psl_s[N_META:].reshape(1, BS * SL_IN), BS, w, l, st_l, 1)
        orw_s, sT = _rwkv_step(prw_s[N_META:], w, l, rwkv_in, state_rwkv_shift)
        s_states.append({"s5": hT, "lru": lT, "conv": cT, "rwkv": sT, "shift": prw_s[N_META:]})
        if final:
            rows = (hs[N_META:], osl_s.reshape(BS, SL_OUT), orw_s)
        else:
            rows = (hs, jnp.concatenate([osl_m[:, 0:SL_OUT], osl_s.reshape(BS, SL_OUT)], axis=0),
                    jnp.concatenate([orw_m[0], orw_s], axis=0))
        xs, w["wout"], w["wg2"], w["wu2"], w["wd2"] = _k3_stream(*rows, a, w["fnorm"], l, final)

        hp, prw_p, psl_p = _k1(xp, w, l, PROMPT_TOKEN_TILE)
        osl_p, orw_p, st_p = _seq_mixers(prw_p, psl_p, st_meta, w, l, PROMPT_CHUNK)
        p_states.append(st_p)
        xp = _k3(hp, osl_p, orw_p, w, l, PROMPT_TOKEN_TILE, final)

    p_out = (
        jnp.stack([s["s5"][0, 0].reshape(BP, S5_GROUPS, S5_STATE) for s in p_states]),
        jnp.stack([s["s5"][0, 1].reshape(BP, S5_GROUPS, S5_STATE) for s in p_states]),
        jnp.stack([_groups_to_heads(s["rwkv"], BP) for s in p_states]),
        jnp.stack([s["shift"][:, 0, :] for s in p_states]),
        jnp.stack([s["lru"][0] for s in p_states]),
        jnp.stack([jnp.swapaxes(s["conv"][0], 0, 1) for s in p_states]),
    )
    s_out = (
        jnp.stack([s["s5"][0].reshape(BS, S5_GROUPS, S5_STATE) for s in s_states]),
        jnp.stack([s["s5"][1].reshape(BS, S5_GROUPS, S5_STATE) for s in s_states]),
        jnp.transpose(jnp.stack([s["rwkv"] for s in s_states]), (0, 4, 1, 2, 3)),
        jnp.stack([s["shift"] for s in s_states]),
        jnp.stack([s["lru"] for s in s_states]),
        jnp.stack([jnp.swapaxes(s["conv"], 0, 1) for s in s_states]),
    )
    return (xp, xs.reshape(BS, 1, D_MODEL)) + p_out + s_out
```

```python
import functools
import math

import jax
import jax.numpy as jnp
from jax import lax
from jax.experimental import pallas as pl
from jax.experimental.pallas import tpu as pltpu

F32 = jnp.float32
BF16 = jnp.bfloat16

D_MODEL = 1024
DEPTH = 2
N_META = 16
D_FF = 2816
NORM_EPS = 1e-6
FFN_RES_SCALE = 0.5
S5_WIDTH = 384
S5_GROUP = 16
S5_GROUPS = 24
S5_STATE = 64
S5_LANES = S5_GROUPS * S5_STATE
RWKV_HEAD = 64
RWKV_WIDTH = 384
RWKV_HEADS = 6
RWKV_PAIRS = 3
DECAY_RANK = 64
ICL_RANK = 64
RWKV_PROJ = 1408
RWKV_LN_EPS = 64e-5
LRU_WIDTH = 256
CONV_WIDTH = 4
CONV_HIST = CONV_WIDTH - 1
LRU_C = 8.0
D_IN = 2304
SL_IN = S5_WIDTH + 2 * LRU_WIDTH
SL_OUT = S5_WIDTH + LRU_WIDTH
LANE = 128
SUBLANE = 8
S5_GROUPS_PER_BLOCK = LANE // S5_GROUP
S5_BLOCKS = S5_GROUPS // S5_GROUPS_PER_BLOCK
S5_BLOCK_STATES = S5_GROUPS_PER_BLOCK * S5_STATE

V7X_VMEM_BYTES = 64 * 2 ** 20
VMEM_LIMIT = (V7X_VMEM_BYTES * 7) // 8

PROMPT_TOKEN_TILE = 512
WEIGHT_CHUNK = 256
ROW_SUBTILE = 256
PROMPT_CHUNK = 64


def _dot(a, b):
    return jnp.dot(a, b, preferred_element_type=F32)


def _dot_nt(a, b):
    bt = jnp.transpose(b.astype(F32)).astype(BF16)
    return jnp.dot(a, bt, preferred_element_type=F32)


def _dot_tn(a, b):
    return lax.dot_general(a, b, (((0,), (0,)), ((), ())), preferred_element_type=F32)


def _split3(x):
    hi = x.astype(BF16)
    r = x - hi.astype(F32)
    mid = r.astype(BF16)
    lo = (r - mid.astype(F32)).astype(BF16)
    return hi, mid, lo


def _sigmoid(x):
    return 0.5 * (jnp.tanh(0.5 * x) + 1.0)


def _gelu(x):
    c = math.sqrt(2.0 / math.pi)
    return 0.5 * x * (1.0 + jnp.tanh(c * (x + 0.044715 * (x * x * x))))


def _neg_expm1(z):
    t = jnp.tanh(0.5 * z)
    return -2.0 * t / (1.0 - t)


def _softplus(x):
    return jnp.maximum(x, 0.0) + jnp.log1p(jnp.exp(-jnp.abs(x)))


def _rms(x, g):
    return x * lax.rsqrt(jnp.mean(x * x, axis=-1, keepdims=True) + NORM_EPS) * g


def _swiglu_res(x, g_norm, wg_ref, wu_ref, wd_ref):
    xn = _rms(x, g_norm).astype(BF16)
    yield
    g = _dot(xn, wg_ref[...])
    u = _dot(xn, wu_ref[...])
    yield
    a = (g * _sigmoid(g) * u).astype(BF16)
    yield
    d = _dot(a, wd_ref[...])
    yield
    return x + FFN_RES_SCALE * d


def _staggered(gens):
    pending = list(gens)
    active = []
    while pending or active:
        if pending:
            active.append(pending.pop(0))
        for g in list(active):
            try:
                next(g)
            except StopIteration:
                active.remove(g)


def _row_tiles(n_rows):
    sub = ROW_SUBTILE if n_rows % ROW_SUBTILE == 0 else n_rows
    return [slice(r, r + sub) for r in range(0, n_rows, sub)]


def _full_spec(arr):
    nd = arr.ndim
    return pl.BlockSpec(arr.shape, lambda *_: (0,) * nd, pipeline_mode=pl.Buffered(1))


def _layer_spec(arr, l):
    nd = arr.ndim - 1
    return pl.BlockSpec((None,) + arr.shape[1:], lambda *_: (l,) + (0,) * nd, pipeline_mode=pl.Buffered(1))


def _row(ref, l):
    return ref[l:l + 1, :]


def _k1_kernel(x_ref, n1_ref, wg_ref, wu_ref, wd_ref, nm_ref, win_ref, h_ref, prw_ref, psl_ref, *, l):
    def rows_pipeline(rows):
        h = yield from _swiglu_res(x_ref[rows, :], _row(n1_ref, l), wg_ref, wu_ref, wd_ref)
        h_ref[rows, :] = h
        hn = _rms(h, _row(nm_ref, l)).astype(BF16)
        yield
        cut0, cut1 = S5_WIDTH + LANE, S5_WIDTH + RWKV_PROJ
        head = _dot(hn, win_ref[:, 0:cut0])
        psl_ref[rows, 0:S5_WIDTH] = head[:, 0:S5_WIDTH]
        prw_ref[rows, 0:LANE] = head[:, S5_WIDTH:cut0]
        prw_ref[rows, LANE:RWKV_PROJ] = _dot(hn, win_ref[:, cut0:cut1])
        psl_ref[rows, S5_WIDTH:SL_IN] = _dot(hn, win_ref[:, cut1:D_IN])

    _staggered([rows_pipeline(rows) for rows in _row_tiles(x_ref.shape[0])])


def _k1(x, w, l, tl):
    B, L, _ = x.shape
    return pl.pallas_call(
        functools.partial(_k1_kernel, l=l),
        grid=(B, L // tl),
        in_specs=[
            pl.BlockSpec((None, tl, D_MODEL), lambda b, i: (b, i, 0)),
            _full_spec(w["n1"]),
            _full_spec(w["wg1"]),
            _full_spec(w["wu1"]),
            _full_spec(w["wd1"]),
            _full_spec(w["nm"]),
            _full_spec(w["win"]),
        ],
        out_specs=[
            pl.BlockSpec((None, tl, D_MODEL), lambda b, i: (b, i, 0)),
            pl.BlockSpec((None, tl, RWKV_PROJ), lambda b, i: (b, i, 0)),
            pl.BlockSpec((tl, SL_IN), lambda b, i: (i, b)),
        ],
        out_shape=[
            jax.ShapeDtypeStruct((B, L, D_MODEL), F32),
            jax.ShapeDtypeStruct((B, L, RWKV_PROJ), F32),
            jax.ShapeDtypeStruct((L, B * SL_IN), F32),
        ],
        compiler_params=pltpu.CompilerParams(
            dimension_semantics=("parallel", "parallel"), vmem_limit_bytes=VMEM_LIMIT),
        name="ffn1_inproj",
    )(x, w["n1"], w["wg1"], w["wu1"], w["wd1"], w["nm"], w["win"])


def _k3_kernel(h_ref, osl_ref, orw_ref, wout_ref, n2_ref, wg_ref, wu_ref, wd_ref, fn_ref, o_ref, *, l, final):
    def rows_pipeline(rows):
        osl = osl_ref[rows, :]
        mixed = jnp.concatenate([osl[:, 0:S5_WIDTH], orw_ref[rows, :], osl[:, S5_WIDTH:SL_OUT]], axis=-1)
        mix = _dot(mixed.astype(BF16), wout_ref[...])
        yield
        h3 = yield from _swiglu_res(h_ref[rows, :] + mix, _row(n2_ref, l), wg_ref, wu_ref, wd_ref)
        if final:
            h3 = _rms(h3, fn_ref[...])
        o_ref[rows, :] = h3

    _staggered([rows_pipeline(rows) for rows in _row_tiles(h_ref.shape[0])])


def _k3(h, osl, orw, w, l, tl, final):
    B, L, _ = h.shape
    return pl.pallas_call(
        functools.partial(_k3_kernel, l=l, final=final),
        grid=(B, L // tl),
        in_specs=[
            pl.BlockSpec((None, tl, D_MODEL), lambda b, i: (b, i, 0)),
            pl.BlockSpec((tl, SL_OUT), lambda b, i: (i, b)),
            pl.BlockSpec((None, tl, RWKV_WIDTH), lambda b, i: (b, i, 0)),
            _full_spec(w["wout"]),
            _full_spec(w["n2"]),
            _full_spec(w["wg2"]),
            _full_spec(w["wu2"]),
            _full_spec(w["wd2"]),
            _full_spec(w["fnorm"]),
        ],
        out_specs=pl.BlockSpec((None, tl, D_MODEL), lambda b, i: (b, i, 0)),
        out_shape=jax.ShapeDtypeStruct((B, L, D_MODEL), F32),
        compiler_params=pltpu.CompilerParams(
            dimension_semantics=("parallel", "parallel"), vmem_limit_bytes=VMEM_LIMIT),
        name="outproj_ffn2",
    )(h, osl, orw, w["wout"], w["n2"], w["wg2"], w["wu2"], w["wd2"], w["fnorm"])


def _stream_spec(arr, l, axis, first, count):
    shape = [None] + list(arr.shape[1:])
    shape[axis] = WEIGHT_CHUNK

    def index(c):
        idx = [l] + [0] * (arr.ndim - 1)
        idx[axis] = jnp.clip(c - first, 0, count - 1)
        return tuple(idx)

    return pl.BlockSpec(tuple(shape), index)


def _cast_spec(rows, cols, axis, first, count):
    shape = [rows, cols]
    shape[axis] = WEIGHT_CHUNK

    def index(c):
        idx = [0, 0]
        idx[axis] = jnp.clip(c - first, 0, count - 1)
        return tuple(idx)

    return pl.BlockSpec(tuple(shape), index)


def _ffn_chunk(c, first, count, xn_scr, acc_scr, wg_ref, wu_ref, wd_ref, wgb_ref, wub_ref, wdb_ref):
    @pl.when((c >= first) & (c < first + count))
    def _():
        wg = wg_ref[...].astype(BF16)
        wu = wu_ref[...].astype(BF16)
        wd = wd_ref[...].astype(BF16)
        wgb_ref[...] = wg
        wub_ref[...] = wu
        wdb_ref[...] = wd
        xn = xn_scr[...]
        g = _dot(xn, wg)
        u = _dot(xn, wu)
        a = (g * _sigmoid(g) * u).astype(BF16)
        acc_scr[...] += _dot(a, wd)


def _k1_stream_kernel(x_ref, n1_ref, nm_ref, wg_ref, wu_ref, wd_ref, win_ref,
                      h_ref, prw_ref, psl_ref, wgb_ref, wub_ref, wdb_ref, winb_ref,
                      xn_scr, acc_scr, proj_scr, *, l, nf, ni):
    c = pl.program_id(0)

    @pl.when(c == 0)
    def _():
        xn_scr[...] = _rms(x_ref[...], _row(n1_ref, l)).astype(BF16)
        acc_scr[...] = jnp.zeros(acc_scr.shape, F32)

    _ffn_chunk(c, 0, nf, xn_scr, acc_scr, wg_ref, wu_ref, wd_ref, wgb_ref, wub_ref, wdb_ref)

    @pl.when(c == nf - 1)
    def _():
        h = x_ref[...] + FFN_RES_SCALE * acc_scr[...]
        h_ref[...] = h
        xn_scr[...] = _rms(h, _row(nm_ref, l)).astype(BF16)

    @pl.when(c >= nf)
    def _():
        win = win_ref[...].astype(BF16)
        winb_ref[...] = win
        col = pl.multiple_of((c - nf) * WEIGHT_CHUNK, WEIGHT_CHUNK)
        proj_scr[:, pl.ds(col, WEIGHT_CHUNK)] = _dot(xn_scr[...], win)

    @pl.when(c == nf + ni - 1)
    def _():
        psl_ref[:, 0:S5_WIDTH] = proj_scr[:, 0:S5_WIDTH]
        prw_ref[...] = proj_scr[:, S5_WIDTH:S5_WIDTH + RWKV_PROJ]
        psl_ref[:, S5_WIDTH:SL_IN] = proj_scr[:, S5_WIDTH + RWKV_PROJ:D_IN]


def _k1_stream(x, a, l):
    NS = x.shape[0]
    nf, ni = D_FF // WEIGHT_CHUNK, D_IN // WEIGHT_CHUNK
    const = lambda shape: pl.BlockSpec(shape, lambda c: (0,) * len(shape))
    return pl.pallas_call(
        functools.partial(_k1_stream_kernel, l=l, nf=nf, ni=ni),
        grid=(nf + ni,),
        in_specs=[
            const((NS, D_MODEL)),
            _full_spec(a["ffn1_norm"]),
            _full_spec(a["mix_norm"]),
            _stream_spec(a["ffn1_w_gate"], l, 2, 0, nf),
            _stream_spec(a["ffn1_w_up"], l, 2, 0, nf),
            _stream_spec(a["ffn1_w_down"], l, 1, 0, nf),
            _stream_spec(a["w_in"], l, 2, nf, ni),
        ],
        out_specs=[
            const((NS, D_MODEL)),
            const((NS, RWKV_PROJ)),
            const((NS, SL_IN)),
            _cast_spec(D_MODEL, D_FF, 1, 0, nf),
            _cast_spec(D_MODEL, D_FF, 1, 0, nf),
            _cast_spec(D_FF, D_MODEL, 0, 0, nf),
            _cast_spec(D_MODEL, D_IN, 1, nf, ni),
        ],
        out_shape=[
            jax.ShapeDtypeStruct((NS, D_MODEL), F32),
            jax.ShapeDtypeStruct((NS, RWKV_PROJ), F32),
            jax.ShapeDtypeStruct((NS, SL_IN), F32),
            jax.ShapeDtypeStruct((D_MODEL, D_FF), BF16),
            jax.ShapeDtypeStruct((D_MODEL, D_FF), BF16),
            jax.ShapeDtypeStruct((D_FF, D_MODEL), BF16),
            jax.ShapeDtypeStruct((D_MODEL, D_IN), BF16),
        ],
        scratch_shapes=[
            pltpu.VMEM((NS, D_MODEL), BF16),
            pltpu.VMEM((NS, D_MODEL), F32),
            pltpu.VMEM((NS, D_IN), F32),
        ],
        compiler_params=pltpu.CompilerParams(dimension_semantics=("arbitrary",), vmem_limit_bytes=VMEM_LIMIT),
        name="ffn1_inproj_stream",
    )(x, a["ffn1_norm"], a["mix_norm"], a["ffn1_w_gate"], a["ffn1_w_up"], a["ffn1_w_down"], a["w_in"])


def _k3_stream_kernel(h_ref, osl_ref, orw_ref, n2_ref, fn_ref, wout_ref, wg_ref, wu_ref, wd_ref,
                      o_ref, woutb_ref, wgb_ref, wub_ref, wdb_ref,
                      mix_scr, h2_scr, xn_scr, acc_scr, *, l, no, nf, final):
    c = pl.program_id(0)

    @pl.when(c == 0)
    def _():
        osl = osl_ref[...]
        mix_scr[...] = jnp.concatenate(
            [osl[:, 0:S5_WIDTH], orw_ref[...], osl[:, S5_WIDTH:SL_OUT]], axis=-1).astype(BF16)
        h2_scr[...] = h_ref[...]

    @pl.when(c < no)
    def _():
        wout = wout_ref[...].astype(BF16)
        woutb_ref[...] = wout
        row = pl.multiple_of(c * WEIGHT_CHUNK, WEIGHT_CHUNK)
        h2_scr[...] += _dot(mix_scr[:, pl.ds(row, WEIGHT_CHUNK)], wout)

    @pl.when(c == no - 1)
    def _():
        xn_scr[...] = _rms(h2_scr[...], _row(n2_ref, l)).astype(BF16)
        acc_scr[...] = jnp.zeros(acc_scr.shape, F32)

    _ffn_chunk(c, no, nf, xn_scr, acc_scr, wg_ref, wu_ref, wd_ref, wgb_ref, wub_ref, wdb_ref)

    @pl.when(c == no + nf - 1)
    def _():
        h3 = h2_scr[...] + FFN_RES_SCALE * acc_scr[...]
        if final:
            h3 = _rms(h3, fn_ref[...])
        o_ref[...] = h3


def _k3_stream(h, osl, orw, a, fnorm, l, final):
    NS = h.shape[0]
    no, nf = D_MODEL // WEIGHT_CHUNK, D_FF // WEIGHT_CHUNK
    const = lambda shape: pl.BlockSpec(shape, lambda c: (0,) * len(shape))
    return pl.pallas_call(
        functools.partial(_k3_stream_kernel, l=l, no=no, nf=nf, final=final),
        grid=(no + nf,),
        in_specs=[
            const((NS, D_MODEL)),
            const((NS, SL_OUT)),
            const((NS, RWKV_WIDTH)),
            _full_spec(a["ffn2_norm"]),
            _full_spec(fnorm),
            _stream_spec(a["w_out"], l, 1, 0, no),
            _stream_spec(a["ffn2_w_gate"], l, 2, no, nf),
            _stream_spec(a["ffn2_w_up"], l, 2, no, nf),
            _stream_spec(a["ffn2_w_down"], l, 1, no, nf),
        ],
        out_specs=[
            const((NS, D_MODEL)),
            _cast_spec(D_MODEL, D_MODEL, 0, 0, no),
            _cast_spec(D_MODEL, D_FF, 1, no, nf),
            _cast_spec(D_MODEL, D_FF, 1, no, nf),
            _cast_spec(D_FF, D_MODEL, 0, no, nf),
        ],
        out_shape=[
            jax.ShapeDtypeStruct((NS, D_MODEL), F32),
            jax.ShapeDtypeStruct((D_MODEL, D_MODEL), BF16),
            jax.ShapeDtypeStruct((D_MODEL, D_FF), BF16),
            jax.ShapeDtypeStruct((D_MODEL, D_FF), BF16),
            jax.ShapeDtypeStruct((D_FF, D_MODEL), BF16),
        ],
        scratch_shapes=[
            pltpu.VMEM((NS, D_MODEL), BF16),
            pltpu.VMEM((NS, D_MODEL), F32),
            pltpu.VMEM((NS, D_MODEL), BF16),
            pltpu.VMEM((NS, D_MODEL), F32),
        ],
        compiler_params=pltpu.CompilerParams(dimension_semantics=("arbitrary",), vmem_limit_bytes=VMEM_LIMIT),
        name="outproj_ffn2_stream",
    )(h, osl, orw, a["ffn2_norm"], fnorm, a["w_out"], a["ffn2_w_gate"], a["ffn2_w_up"], a["ffn2_w_down"])


def _s5_prep_kernel(lr_ref, li_ref, ldt_ref, br_ref, bi_ref, abr_ref, abi_ref, bbr_ref, bbi_ref):
    lr = lr_ref[...]
    li = li_ref[...]
    dt = jnp.exp(ldt_ref[...])
    mag = jnp.exp(lr * dt)
    ab_re = mag * jnp.cos(li * dt)
    ab_im = mag * jnp.sin(li * dt)
    den = lr * lr + li * li
    f_re = ((ab_re - 1.0) * lr + ab_im * li) / den
    f_im = (ab_im * lr - (ab_re - 1.0) * li) / den
    abr_ref[...] = ab_re
    abi_ref[...] = ab_im
    br = br_ref[...]
    bi = bi_ref[...]
    bbr_ref[...] = f_re[:, :, None, :] * br - f_im[:, :, None, :] * bi
    bbi_ref[...] = f_re[:, :, None, :] * bi + f_im[:, :, None, :] * br


def _s5_prep(lam_re, lam_im, log_dt, b_re, b_im):
    G, N, C = S5_GROUPS, S5_STATE, S5_GROUP
    ldt = jnp.broadcast_to(log_dt[:, :, None], (DEPTH, G, N))
    br = jnp.swapaxes(b_re, 2, 3)
    bi = jnp.swapaxes(b_im, 2, 3)
    return pl.pallas_call(
        _s5_prep_kernel,
        out_shape=[
            jax.ShapeDtypeStruct((DEPTH, G, N), F32),
            jax.ShapeDtypeStruct((DEPTH, G, N), F32),
            jax.ShapeDtypeStruct((DEPTH, G, C, N), F32),
            jax.ShapeDtypeStruct((DEPTH, G, C, N), F32),
        ],
        name="s5_discretise",
    )(lam_re, lam_im, ldt, br, bi)


def _s5lru_kernel(psl_ref, ab_ref, win_ref, cre_ref, cim_ref, d_ref, gw_ref, gb_ref,
                  cw_ref, cb_ref, wa_ref, wx_ref, ba_ref, bx_ref, lam_ref,
                  h0_ref, l0_ref, c0_ref,
                  o_ref, hT_ref, lT_ref, cT_ref,
                  hs_scr, ls_scr, xc_scr, xr_scr, xi_scr, la_scr, lb_scr, pin_scr, pout_scr, *, l, B, Tc, unroll):
    c = pl.program_id(0)
    R = Tc * B

    @pl.when(c == 0)
    def _():
        hs_scr[...] = h0_ref[...]
        ls_scr[...] = l0_ref[...]
        xc_scr[0:CONV_HIST] = c0_ref[...]

    for b in range(B):
        for j in range(SL_IN // LANE):
            pin_scr[j, pl.ds(b, Tc, stride=B), :] = psl_ref[:, SL_IN * b + LANE * j:SL_IN * b + LANE * (j + 1)]
    psl = jnp.concatenate([pin_scr[j] for j in range(SL_IN // LANE)], axis=-1).reshape(Tc, B, SL_IN)
    u = psl[:, :, 0:S5_WIDTH].reshape(R, S5_WIDTH)
    gate_in = psl[:, :, S5_WIDTH + LRU_WIDTH:SL_IN].reshape(R, LRU_WIDTH)

    ub = u.astype(BF16)
    half = S5_BLOCK_STATES
    for j in range(S5_BLOCKS):
        x = _dot(ub[:, LANE * j:LANE * (j + 1)], win_ref[j])
        xr_scr[:, :, half * j:half * (j + 1)] = x[:, 0:half].reshape(Tc, B, half)
        xi_scr[:, :, half * j:half * (j + 1)] = x[:, half:2 * half].reshape(Tc, B, half)

    xc_scr[CONV_HIST:CONV_HIST + Tc] = psl[:, :, S5_WIDTH:S5_WIDTH + LRU_WIDTH]
    cw = cw_ref[...]
    xc = _row(cb_ref, l) + xc_scr[0:Tc] * cw[0:1]
    for j in range(1, CONV_WIDTH):
        xc = xc + xc_scr[j:j + Tc] * cw[j:j + 1]
    new_hist = xc_scr[Tc:Tc + CONV_HIST]
    xc_scr[0:CONV_HIST] = new_hist
    xc2 = xc.reshape(R, LRU_WIDTH)
    xcb = xc2.astype(BF16)
    gate_a = _sigmoid(_dot(xcb, wa_ref[...]) + _row(ba_ref, l))
    gate_x = _sigmoid(_dot(xcb, wx_ref[...]) + _row(bx_ref, l))
    log_a = LRU_C * gate_a * (-_softplus(-_row(lam_ref, l)))
    la_scr[...] = jnp.exp(log_a).reshape(Tc, B, LRU_WIDTH)
    lb_scr[...] = (jnp.sqrt(_neg_expm1(2.0 * log_a)) * (gate_x * xc2)).reshape(Tc, B, LRU_WIDTH)

    ar = jnp.broadcast_to(ab_ref[0:1, :], (B, S5_LANES))
    ai = jnp.broadcast_to(ab_ref[1:2, :], (B, S5_LANES))

    def step(t, carry):
        hr, hi, hl = carry
        nr = ar * hr - ai * hi + xr_scr[t]
        ni = ar * hi + ai * hr + xi_scr[t]
        xr_scr[t] = nr
        xi_scr[t] = ni
        nl = la_scr[t] * hl + lb_scr[t]
        lb_scr[t] = nl
        return nr, ni, nl

    hr, hi, hl = lax.fori_loop(0, Tc, step, (hs_scr[0], hs_scr[1], ls_scr[...]), unroll=unroll)
    hs_scr[0] = hr
    hs_scr[1] = hi
    ls_scr[...] = hl

    hrb = xr_scr[...].reshape(R, S5_LANES).astype(BF16)
    hib = xi_scr[...].reshape(R, S5_LANES).astype(BF16)
    ys = []
    for j in range(S5_BLOCKS):
        sl = slice(half * j, half * (j + 1))
        ys.append(_dot(hrb[:, sl], cre_ref[j]) - _dot(hib[:, sl], cim_ref[j]))
    y = jnp.concatenate(ys, axis=-1) + _row(d_ref, l) * u
    z = _gelu(y)
    o_s5 = z * _sigmoid(_dot(z.astype(BF16), gw_ref[...].astype(BF16)) + _row(gb_ref, l))
    o_lru = lb_scr[...].reshape(R, LRU_WIDTH) * _gelu(gate_in)
    out = jnp.concatenate([o_s5, o_lru], axis=-1)
    for j in range(SL_OUT // LANE):
        pout_scr[j] = out[:, LANE * j:LANE * (j + 1)]
    for b in range(B):
        for j in range(SL_OUT // LANE):
            o_ref[:, SL_OUT * b + LANE * j:SL_OUT * b + LANE * (j + 1)] = pout_scr[j, pl.ds(b, Tc, stride=B), :]

    @pl.when(c == pl.num_programs(0) - 1)
    def _():
        hT_ref[...] = hs_scr[...]
        lT_ref[...] = ls_scr[...]
        cT_ref[...] = xc_scr[0:CONV_HIST]


def _s5lru(psl2, B, w, l, st, Tc):
    L = psl2.shape[0]
    li = st["li"]
    unroll = True
    return pl.pallas_call(
        functools.partial(_s5lru_kernel, l=l, B=B, Tc=Tc, unroll=unroll),
        grid=(L // Tc,),
        in_specs=[
            pl.BlockSpec((Tc, B * SL_IN), lambda c: (c, 0)),
            _layer_spec(w["s5_ab"], l),
            _layer_spec(w["s5_win"], l),
            _layer_spec(w["s5_cre"], l),
            _layer_spec(w["s5_cim"], l),
            _full_spec(w["s5_d"]),
            _layer_spec(w["s5_glu_w"], l),
            _full_spec(w["s5_glu_b"]),
            _layer_spec(w["lru_conv_w"], l),
            _full_spec(w["lru_conv_b"]),
            _layer_spec(w["lru_wa"], l),
            _layer_spec(w["lru_wx"], l),
            _full_spec(w["lru_b_a"]),
            _full_spec(w["lru_b_x"]),
            _full_spec(w["lru_lambda"]),
            _layer_spec(st["s5"], li),
            _layer_spec(st["lru"], li),
            _layer_spec(st["conv"], li),
        ],
        out_specs=[
            pl.BlockSpec((Tc, B * SL_OUT), lambda c: (c, 0)),
            pl.BlockSpec((2, B, S5_LANES), lambda c: (0, 0, 0)),
            pl.BlockSpec((B, LRU_WIDTH), lambda c: (0, 0)),
            pl.BlockSpec((CONV_HIST, B, LRU_WIDTH), lambda c: (0, 0, 0)),
        ],
        out_shape=[
            jax.ShapeDtypeStruct((L, B * SL_OUT), F32),
            jax.ShapeDtypeStruct((2, B, S5_LANES), F32),
            jax.ShapeDtypeStruct((B, LRU_WIDTH), F32),
            jax.ShapeDtypeStruct((CONV_HIST, B, LRU_WIDTH), F32),
        ],
        scratch_shapes=[
            pltpu.VMEM((2, B, S5_LANES), F32),
            pltpu.VMEM((B, LRU_WIDTH), F32),
            pltpu.VMEM((Tc + CONV_HIST, B, LRU_WIDTH), F32),
            pltpu.VMEM((Tc, B, S5_LANES), F32),
            pltpu.VMEM((Tc, B, S5_LANES), F32),
            pltpu.VMEM((Tc, B, LRU_WIDTH), F32),
            pltpu.VMEM((Tc, B, LRU_WIDTH), F32),
            pltpu.VMEM((SL_IN // LANE, Tc * B, LANE), F32),
            pltpu.VMEM((SL_OUT // LANE, Tc * B, LANE), F32),
        ],
        compiler_params=pltpu.CompilerParams(
            dimension_semantics=("arbitrary",), vmem_limit_bytes=VMEM_LIMIT),
        name="s5_rglru_scan",
    )(psl2, w["s5_ab"], w["s5_win"], w["s5_cre"], w["s5_cim"], w["s5_d"], w["s5_glu_w"], w["s5_glu_b"],
      w["lru_conv_w"], w["lru_conv_b"], w["lru_wa"], w["lru_wx"], w["lru_b_a"], w["lru_b_x"], w["lru_lambda"],
      st["s5"], st["lru"], st["conv"])


_RWKV_PARAMS = ("rwkv_mu", "rwkv_w0", "rwkv_w_up", "rwkv_a0", "rwkv_a_up", "rwkv_g_up", "rwkv_k_k", "rwkv_k_a",
                "rwkv_r_k", "rwkv_ln_w", "rwkv_ln_b", "rwkv_ones")


def _rwkv_param_specs(w, l):
    layered = ("rwkv_w_up", "rwkv_a_up", "rwkv_g_up")
    return [_layer_spec(w[k], l) if k in layered else _full_spec(w[k]) for k in _RWKV_PARAMS]


def _rwkv_mats(prm):
    wup_ref, aup_ref, gup_ref, ones_ref = prm[2], prm[4], prm[5], prm[11]
    zpad = jnp.zeros((DECAY_RANK, RWKV_WIDTH), BF16)
    wup = jnp.concatenate([wup_ref[...].astype(BF16), zpad], axis=0)
    aup = jnp.concatenate([zpad, aup_ref[...].astype(BF16)], axis=0)
    return wup, aup, gup_ref[...].astype(BF16), ones_ref[...]


def _segsum(x, ones):
    xb = x.astype(BF16)
    return jnp.concatenate([_dot(xb[:, LANE * p:LANE * (p + 1)], ones) for p in range(RWKV_PAIRS)], axis=-1)


def _rwkv_token_math(p, prev, prm, mats, l):
    mu_ref, w0_ref, _, a0_ref, _, _, kk_ref, ka_ref, rk_ref = prm[0:9]
    wup, aup, gup, ones = mats
    W = RWKV_WIDTH
    xm = p + (prev - p) * _row(mu_ref, l)
    r = xm[:, 0:W]
    k = xm[:, W:2 * W]
    v = xm[:, 2 * W:3 * W]
    xwa = xm[:, 3 * W:3 * W + DECAY_RANK + ICL_RANK]
    xg = xm[:, 3 * W + DECAY_RANK + ICL_RANK:RWKV_PROJ]
    lw = _row(w0_ref, l) + _dot(jnp.tanh(xwa).astype(BF16), wup)
    ld = -math.exp(-0.5) * _sigmoid(lw)
    a = _sigmoid(_row(a0_ref, l) + _dot(xwa.astype(BF16), aup))
    g = _dot(_sigmoid(xg).astype(BF16), gup)
    kkr = k * _row(kk_ref, l)
    kk = kkr * lax.rsqrt(jnp.maximum(_segsum(kkr * kkr, ones), 1e-24))
    k2 = k * (1.0 + (a - 1.0) * _row(ka_ref, l))
    bonus = _segsum(r * k2 * _row(rk_ref, l), ones) * v
    return r, k2, v, kk, a, ld, g, bonus


def _rwkv_out(y, bonus, g, prm, ones, l):
    lnw_ref, lnb_ref = prm[9], prm[10]
    inv = 1.0 / RWKV_HEAD
    mean = _segsum(y, ones) * inv
    d = y - mean
    var = _segsum(d * d, ones) * inv
    yn = d * lax.rsqrt(var + RWKV_LN_EPS) * _row(lnw_ref, l) + _row(lnb_ref, l)
    return (yn + bonus) * g


RWKV_GROUP_HEADS = 4
RWKV_GROUP_LANES = RWKV_GROUP_HEADS * RWKV_HEAD


def _rwkv_groups(B):
    full = [[(b, 0, RWKV_GROUP_LANES)] for b in range(B)]
    rest = RWKV_WIDTH - RWKV_GROUP_LANES
    tail = [[(b, RWKV_GROUP_LANES, rest), (b + 1, RWKV_GROUP_LANES, rest)] for b in range(0, B, 2)]
    return full + tail


def _rwkv_chunk_kernel(*refs, l, B, Tc):
    n_prm = len(_RWKV_PARAMS)
    p_ref, s0_ref, sh0_ref = refs[0:3]
    prm = refs[3:3 + n_prm]
    tri_ref, o_ref, sT_ref = refs[3 + n_prm:6 + n_prm]
    st_scr, xs_scr, y_scr, pt_scr = refs[6 + n_prm:10 + n_prm]
    src = dict(zip(("a", "b", "k", "r", "v", "bp", "kp"), refs[10 + n_prm:]))
    c = pl.program_id(0)
    R = B * Tc
    W = RWKV_WIDTH
    GL = RWKV_GROUP_LANES
    GH = RWKV_GROUP_HEADS
    carry_row = SUBLANE - 1

    @pl.when(c == 0)
    def _():
        st_scr[...] = s0_ref[...]
        xs_scr[:, carry_row:SUBLANE, :] = sh0_ref[...]

    ps, prevs = [], []
    for b in range(B):
        pb = p_ref[b]
        xs_scr[b, SUBLANE:SUBLANE + Tc, :] = pb
        prevs.append(xs_scr[b, carry_row:carry_row + Tc, :])
        xs_scr[b, carry_row:SUBLANE, :] = pb[Tc - 1:Tc, :]
        ps.append(pb)
    mats = _rwkv_mats(prm)
    r, k2, v, kk, a, ld, g, bonus = _rwkv_token_math(
        jnp.concatenate(ps, axis=0), jnp.concatenate(prevs, axis=0), prm, mats, l)

    h1, h2, h3 = _split3(ld)
    tri = tri_ref[...]
    cls = []
    for b in range(B):
        rows = slice(b * Tc, (b + 1) * Tc)
        cls.append(_dot(tri, h1[rows]) + _dot(tri, h2[rows]) + _dot(tri, h3[rows]))
    cl = jnp.concatenate(cls, axis=0)
    pt = jnp.exp(cl.reshape(B, Tc, W)[:, Tc - 1:Tc, :])
    inv_p = jnp.exp(-cl)
    to_end = (pt * inv_p.reshape(B, Tc, W)).reshape(R, W)
    beta = kk * a
    src["a"][...] = (-kk * jnp.exp(cl - ld)).astype(BF16)
    src["b"][...] = (beta * inv_p).astype(BF16)
    src["k"][...] = (k2 * inv_p).astype(BF16)
    src["r"][...] = (r * jnp.exp(cl)).astype(BF16)
    src["v"][...] = v.astype(BF16)
    src["bp"][...] = (beta * to_end).astype(BF16)
    src["kp"][...] = (k2 * to_end).astype(BF16)
    pt_scr[...] = pt

    assert Tc & (Tc - 1) == 0
    n_doubling = max(1, (Tc - 1).bit_length())
    groups = _rwkv_groups(B)

    def block_masks(blk):
        lane = lax.broadcasted_iota(jnp.int32, (1, GH * blk), 1)
        return [((lane >= blk * h) & (lane < blk * (h + 1))).astype(BF16) for h in range(GH)]

    head_masks = block_masks(RWKV_HEAD)
    time_masks = block_masks(Tc)

    def stack(x, masks):
        return jnp.concatenate([x * m for m in masks], axis=0)

    ti = lax.broadcasted_iota(jnp.int32, (Tc, GH * Tc), 0)
    si = lax.broadcasted_iota(jnp.int32, (Tc, GH * Tc), 1) & (Tc - 1)
    strict = (si < ti).astype(BF16)
    incl = (si <= ti).astype(BF16)
    same_head = ((lax.broadcasted_iota(jnp.int32, (GL, GL), 0) // RWKV_HEAD)
                 == (lax.broadcasted_iota(jnp.int32, (GL, GL), 1) // RWKV_HEAD)).astype(F32)

    def operand(name, grp):
        parts = [src[name][b * Tc:(b + 1) * Tc, lo:lo + n] for b, lo, n in grp]
        return parts[0] if len(parts) == 1 else jnp.concatenate(parts, axis=-1)

    v4s, gss, ns, wrbs, wrks, xs = [], [], [], [], [], []
    for gi, grp in enumerate(groups):
        lhs = jnp.concatenate([operand("a", grp), operand("r", grp)], axis=0)
        v4 = stack(operand("v", grp), head_masks)
        gb = _dot_nt(lhs, stack(operand("b", grp), head_masks)).astype(BF16)
        gk = _dot_nt(lhs, stack(operand("k", grp), head_masks)).astype(BF16)
        gs = _dot_nt(lhs, st_scr[gi].astype(BF16))
        v4s.append(v4)
        gss.append(gs[Tc:2 * Tc])
        ns.append(gb[0:Tc] * strict)
        wrbs.append(gb[Tc:2 * Tc] * incl)
        wrks.append(gk[Tc:2 * Tc] * incl)
        xs.append(gs[0:Tc] + _dot(gk[0:Tc] * strict, v4))
    for j in range(n_doubling):
        for i in range(len(groups)):
            xs[i] = xs[i] + _dot(ns[i], stack(xs[i].astype(BF16), head_masks))
        if j + 1 < n_doubling:
            for i in range(len(groups)):
                ns[i] = _dot(ns[i], stack(ns[i], time_masks)).astype(BF16)
    for gi, grp in enumerate(groups):
        sab = xs[gi].astype(BF16)
        y = gss[gi] + _dot(wrbs[gi], stack(sab, head_masks)) + _dot(wrks[gi], v4s[gi])
        lane0 = 0
        for b, lo, n in grp:
            y_scr[b * Tc:(b + 1) * Tc, lo:lo + n] = y[:, lane0:lane0 + n]
            lane0 += n
        upd = _dot_tn(jnp.concatenate([sab, operand("v", grp)], axis=0),
                      jnp.concatenate([operand("bp", grp), operand("kp", grp)], axis=0))
        ptg = [pt_scr[b, :, lo:lo + n] for b, lo, n in grp]
        ptg = ptg[0] if len(ptg) == 1 else jnp.concatenate(ptg, axis=-1)
        st_scr[gi] = st_scr[gi] * ptg + upd * same_head

    out = _rwkv_out(y_scr[...], bonus, g, prm, mats[3], l)
    for b in range(B):
        o_ref[b] = out[b * Tc:(b + 1) * Tc]

    @pl.when(c == pl.num_programs(0) - 1)
    def _():
        sT_ref[...] = st_scr[...]


def _rwkv_chunk(prw, w, l, s0, sh0, Tc):
    B, L, _ = prw.shape
    R = B * Tc
    W = RWKV_WIDTH
    GL = RWKV_GROUP_LANES
    NG = len(_rwkv_groups(B))
    tri = jnp.tril(jnp.ones((Tc, Tc), F32)).astype(BF16)
    return pl.pallas_call(
        functools.partial(_rwkv_chunk_kernel, l=l, B=B, Tc=Tc),
        grid=(L // Tc,),
        in_specs=[
            pl.BlockSpec((B, Tc, RWKV_PROJ), lambda c: (0, c, 0)),
            _full_spec(s0),
            _full_spec(sh0),
        ] + _rwkv_param_specs(w, l) + [_full_spec(tri)],
        out_specs=[
            pl.BlockSpec((B, Tc, W), lambda c: (0, c, 0)),
            pl.BlockSpec((NG, GL, GL), lambda c: (0, 0, 0)),
        ],
        out_shape=[
            jax.ShapeDtypeStruct((B, L, W), F32),
            jax.ShapeDtypeStruct((NG, GL, GL), F32),
        ],
        scratch_shapes=[
            pltpu.VMEM((NG, GL, GL), F32),
            pltpu.VMEM((B, Tc + SUBLANE, RWKV_PROJ), F32),
            pltpu.VMEM((R, W), F32),
            pltpu.VMEM((B, 1, W), F32),
        ] + [pltpu.VMEM((R, W), BF16)] * 7,
        compiler_params=pltpu.CompilerParams(
            dimension_semantics=("arbitrary",), vmem_limit_bytes=VMEM_LIMIT),
        name="rwkv7_chunked",
    )(prw, s0, sh0, *[w[k] for k in _RWKV_PARAMS], tri)


def _rwkv_step_kernel(*refs, l):
    n_prm = len(_RWKV_PARAMS)
    p_ref, sh_ref, s_ref = refs[0:3]
    prm = refs[3:3 + n_prm]
    o_ref, sn_ref = refs[3 + n_prm:5 + n_prm]
    nkk_t, w_t, beta_t, v_t, k2_t, r_t, y_t, g_scr, bonus_scr = refs[5 + n_prm:]
    h = pl.program_id(0)
    N = RWKV_HEAD

    @pl.when(h == 0)
    def _():
        r, k2, v, kk, a, ld, g, bonus = _rwkv_token_math(p_ref[...], sh_ref[...], prm, _rwkv_mats(prm), l)
        nkk_t[...] = (-kk).T
        w_t[...] = jnp.exp(ld).T
        beta_t[...] = (kk * a).T
        v_t[...] = v.T
        k2_t[...] = k2.T
        r_t[...] = r.T
        g_scr[...] = g
        bonus_scr[...] = bonus

    rows = pl.ds(pl.multiple_of(h * N, N), N)
    s = s_ref[...]
    sa = jnp.sum(s * nkk_t[rows, :][None], axis=1)
    sn = s * w_t[rows, :][None] + sa[:, None, :] * beta_t[rows, :][None] + v_t[rows, :][:, None, :] * k2_t[rows, :][None]
    sn_ref[...] = sn
    y_t[rows, :] = jnp.sum(sn * r_t[rows, :][None], axis=1)

    @pl.when(h == pl.num_programs(0) - 1)
    def _():
        o_ref[...] = _rwkv_out(y_t[...].T, bonus_scr[...], g_scr[...], prm, prm[11][...], l)


def _rwkv_step(prw, w, l, s_all, sh_all):
    B = prw.shape[0]
    W = RWKV_WIDTH
    N = RWKV_HEAD
    return pl.pallas_call(
        functools.partial(_rwkv_step_kernel, l=l),
        grid=(RWKV_HEADS,),
        in_specs=[
            _full_spec(prw),
            _layer_spec(sh_all, l),
            pl.BlockSpec((None, None, N, N, B), lambda h: (l, h, 0, 0, 0)),
        ] + _rwkv_param_specs(w, l),
        out_specs=[
            pl.BlockSpec((B, W), lambda h: (0, 0)),
            pl.BlockSpec((None, N, N, B), lambda h: (h, 0, 0, 0)),
        ],
        out_shape=[
            jax.ShapeDtypeStruct((B, W), F32),
            jax.ShapeDtypeStruct((RWKV_HEADS, N, N, B), F32),
        ],
        scratch_shapes=[pltpu.VMEM((W, B), F32)] * 7 + [pltpu.VMEM((B, W), F32)] * 2,
        compiler_params=pltpu.CompilerParams(
            dimension_semantics=("arbitrary",), vmem_limit_bytes=VMEM_LIMIT),
        name="rwkv7_step",
    )(prw, sh_all, s_all, *[w[k] for k in _RWKV_PARAMS])


def _block_diag(blocks):
    *lead, n, r, c = blocks.shape
    eye = jnp.eye(n, dtype=blocks.dtype)
    out = blocks[..., :, :, None, :] * eye[:, None, :, None]
    return out.reshape(*lead, n * r, n * c)


def _prepare_weights(a):
    G, C, N = S5_GROUPS, S5_GROUP, S5_STATE
    abr, abi, bbr, bbi = _s5_prep(a["s5_lambda_re"], a["s5_lambda_im"], a["s5_log_dt"], a["s5_b_re"], a["s5_b_im"])

    def in_side(bb):
        return _block_diag(bb.reshape(DEPTH, S5_BLOCKS, S5_GROUPS_PER_BLOCK, C, N))

    def out_side(cc):
        ct = jnp.swapaxes(cc, 2, 3)
        return _block_diag(ct.reshape(DEPTH, S5_BLOCKS, S5_GROUPS_PER_BLOCK, N, C))

    w = {k: a[k] for k in (
        "s5_d", "s5_glu_w", "s5_glu_b", "lru_conv_w", "lru_conv_b", "lru_b_a", "lru_b_x", "lru_lambda",
        "rwkv_mu", "rwkv_w0", "rwkv_w_up", "rwkv_a0", "rwkv_a_up", "rwkv_g_up", "rwkv_k_k", "rwkv_k_a",
        "rwkv_ln_w", "rwkv_ln_b")}
    w.update({
        "n1": a["ffn1_norm"], "nm": a["mix_norm"], "n2": a["ffn2_norm"],
        "fnorm": a["final_norm"].reshape(1, D_MODEL),
        "s5_ab": jnp.stack([abr.reshape(DEPTH, S5_LANES), abi.reshape(DEPTH, S5_LANES)], axis=1),
        "s5_win": jnp.concatenate([in_side(bbr), in_side(bbi)], axis=-1).astype(BF16),
        "s5_cre": out_side(a["s5_c_re"]).astype(BF16),
        "s5_cim": out_side(a["s5_c_im"]).astype(BF16),
        "lru_wa": _block_diag(a["lru_w_a"]).astype(BF16),
        "lru_wx": _block_diag(a["lru_w_x"]).astype(BF16),
        "rwkv_r_k": a["rwkv_r_k"].reshape(DEPTH, RWKV_WIDTH),
        "rwkv_ones": _block_diag(jnp.ones((2, RWKV_HEAD, RWKV_HEAD), F32)).astype(BF16),
    })
    return w


def _groups_to_heads(sg, B):
    N = RWKV_HEAD
    diag = jnp.stack([sg[:, N * i:N * (i + 1), N * i:N * (i + 1)] for i in range(RWKV_GROUP_HEADS)], axis=1)
    first = diag[:B]
    rest = diag[B:].reshape(B, RWKV_HEADS - RWKV_GROUP_HEADS, N, N)
    return jnp.concatenate([first, rest], axis=1)


def _seq_mixers(prw, psl2, st, w, l, Tc):
    L = prw.shape[1]
    osl, hT, lT, cT = _s5lru(psl2, prw.shape[0], w, l, st, Tc)
    orw, sT = _rwkv_chunk(prw, w, l, st["rwkv"], st["shift"], Tc)
    new = {"s5": hT[None], "lru": lT[None], "conv": cT[None], "li": 0,
           "rwkv": sT, "shift": prw[:, L - 1:L, :]}
    return osl, orw, new


def kernel(x_prompt, x_sample, state_s5_re, state_s5_im, state_rwkv, state_rwkv_shift, state_lru, state_lru_conv, meta_tokens, ffn1_norm, ffn1_w_gate, ffn1_w_up, ffn1_w_down, mix_norm, w_in, s5_lambda_re, s5_lambda_im, s5_log_dt, s5_b_re, s5_b_im, s5_c_re, s5_c_im, s5_d, s5_glu_w, s5_glu_b, rwkv_mu, rwkv_w0, rwkv_w_up, rwkv_a0, rwkv_a_up, rwkv_g_up, rwkv_k_k, rwkv_k_a, rwkv_r_k, rwkv_ln_w, rwkv_ln_b, lru_conv_w, lru_conv_b, lru_w_a, lru_b_a, lru_w_x, lru_b_x, lru_lambda, w_out, ffn2_norm, ffn2_w_gate, ffn2_w_up, ffn2_w_down, final_norm):
    a = dict(locals())
    w = _prepare_weights(a)
    BP, SEQ, _ = x_prompt.shape
    BS = x_sample.shape[0]
    assert SEQ % PROMPT_TOKEN_TILE == 0 and SEQ % PROMPT_CHUNK == 0 and BP % 2 == 0

    zero = {
        "s5": jnp.zeros((1, 2, BP, S5_LANES), F32),
        "lru": jnp.zeros((1, BP, LRU_WIDTH), F32),
        "conv": jnp.zeros((1, CONV_HIST, BP, LRU_WIDTH), F32),
        "li": 0,
        "rwkv": jnp.zeros((len(_rwkv_groups(BP)), RWKV_GROUP_LANES, RWKV_GROUP_LANES), F32),
        "shift": jnp.zeros((BP, 1, RWKV_PROJ), F32),
    }
    sample_in = {
        "s5": jnp.stack([state_s5_re.reshape(DEPTH, BS, S5_LANES), state_s5_im.reshape(DEPTH, BS, S5_LANES)], axis=1),
        "lru": state_lru,
        "conv": jnp.swapaxes(state_lru_conv, 1, 2),
    }

    rwkv_in = jnp.transpose(state_rwkv, (0, 2, 3, 4, 1))

    xp = x_prompt
    xs = jnp.concatenate([meta_tokens.astype(F32), x_sample.reshape(BS, D_MODEL)], axis=0)
    p_states = []
    s_states = []
    for l in range(DEPTH):
        final = l == DEPTH - 1
        hs, prw_s, psl_s, w["wg1"], w["wu1"], w["wd1"], w["win"] = _k1_stream(xs, a, l)
        prw_m = jnp.broadcast_to(prw_s[None, :N_META], (BP, N_META, RWKV_PROJ))
        psl_m = jnp.tile(psl_s[:N_META], (1, BP))
        osl_m, orw_m, st_meta = _seq_mixers(prw_m, psl_m, zero, w, l, N_META)

        st_l = dict(sample_in, li=l)
        osl_s, hT, lT, cT = _s5lru(psl_s[N_META:].reshape(1, BS * SL_IN), BS, w, l, st_l, 1)
        orw_s, sT = _rwkv_step(prw_s[N_META:], w, l, rwkv_in, state_rwkv_shift)
        s_states.append({"s5": hT, "lru": lT, "conv": cT, "rwkv": sT, "shift": prw_s[N_META:]})
        if final:
            rows = (hs[N_META:], osl_s.reshape(BS, SL_OUT), orw_s)
        else:
            rows = (hs, jnp.concatenate([osl_m[:, 0:SL_OUT], osl_s.reshape(BS, SL_OUT)], axis=0),
                    jnp.concatenate([orw_m[0], orw_s], axis=0))
        xs, w["wout"], w["wg2"], w["wu2"], w["wd2"] = _k3_stream(*rows, a, w["fnorm"], l, final)

        hp, prw_p, psl_p = _k1(xp, w, l, PROMPT_TOKEN_TILE)
        osl_p, orw_p, st_p = _seq_mixers(prw_p, psl_p, st_meta, w, l, PROMPT_CHUNK)
        p_states.append(st_p)
        xp = _k3(hp, osl_p, orw_p, w, l, PROMPT_TOKEN_TILE, final)

    p_out = (
        jnp.stack([s["s5"][0, 0].reshape(BP, S5_GROUPS, S5_STATE) for s in p_states]),
        jnp.stack([s["s5"][0, 1].reshape(BP, S5_GROUPS, S5_STATE) for s in p_states]),
        jnp.stack([_groups_to_heads(s["rwkv"], BP) for s in p_states]),
        jnp.stack([s["shift"][:, 0, :] for s in p_states]),
        jnp.stack([s["lru"][0] for s in p_states]),
        jnp.stack([jnp.swapaxes(s["conv"][0], 0, 1) for s in p_states]),
    )
    s_out = (
        jnp.stack([s["s5"][0].reshape(BS, S5_GROUPS, S5_STATE) for s in s_states]),
        jnp.stack([s["s5"][1].reshape(BS, S5_GROUPS, S5_STATE) for s in s_states]),
        jnp.transpose(jnp.stack([s["rwkv"] for s in s_states]), (0, 4, 1, 2, 3)),
        jnp.stack([s["shift"] for s in s_states]),
        jnp.stack([s["lru"] for s in s_states]),
        jnp.stack([jnp.swapaxes(s["conv"], 0, 1) for s in s_states]),
    )
    return (xp, xs.reshape(BS, 1, D_MODEL)) + p_out + s_out
```

```python
import functools
import math

import jax
import jax.numpy as jnp
from jax import lax
from jax.experimental import pallas as pl
from jax.experimental.pallas import tpu as pltpu

F32 = jnp.float32
BF16 = jnp.bfloat16

D_MODEL = 1024
DEPTH = 2
N_META = 16
D_FF = 2816
NORM_EPS = 1e-6
FFN_RES_SCALE = 0.5
S5_WIDTH = 384
S5_GROUP = 16
S5_GROUPS = 24
S5_STATE = 64
S5_LANES = S5_GROUPS * S5_STATE
RWKV_HEAD = 64
RWKV_WIDTH = 384
RWKV_HEADS = 6
RWKV_PAIRS = 3
DECAY_RANK = 64
ICL_RANK = 64
RWKV_PROJ = 1408
RWKV_LN_EPS = 64e-5
LRU_WIDTH = 256
CONV_WIDTH = 4
CONV_HIST = CONV_WIDTH - 1
LRU_C = 8.0
D_IN = 2304
SL_IN = S5_WIDTH + 2 * LRU_WIDTH
SL_OUT = S5_WIDTH + LRU_WIDTH
LANE = 128
SUBLANE = 8
S5_GROUPS_PER_BLOCK = LANE // S5_GROUP
S5_BLOCKS = S5_GROUPS // S5_GROUPS_PER_BLOCK
S5_BLOCK_STATES = S5_GROUPS_PER_BLOCK * S5_STATE

V7X_VMEM_BYTES = 64 * 2 ** 20
VMEM_LIMIT = (V7X_VMEM_BYTES * 7) // 8

PROMPT_TOKEN_TILE = 512
WEIGHT_CHUNK = 256
ROW_SUBTILE = 256
PROMPT_CHUNK = 64


def _dot(a, b):
    return jnp.dot(a, b, preferred_element_type=F32)


def _dot_nt(a, b):
    bt = jnp.transpose(b.astype(F32)).astype(BF16)
    return jnp.dot(a, bt, preferred_element_type=F32)


def _dot_tn(a, b):
    return lax.dot_general(a, b, (((0,), (0,)), ((), ())), preferred_element_type=F32)


def _split3(x):
    hi = x.astype(BF16)
    r = x - hi.astype(F32)
    mid = r.astype(BF16)
    lo = (r - mid.astype(F32)).astype(BF16)
    return hi, mid, lo


def _sigmoid(x):
    return 0.5 * (jnp.tanh(0.5 * x) + 1.0)


def _gelu(x):
    c = math.sqrt(2.0 / math.pi)
    return 0.5 * x * (1.0 + jnp.tanh(c * (x + 0.044715 * (x * x * x))))


def _neg_expm1(z):
    t = jnp.tanh(0.5 * z)
    return -2.0 * t / (1.0 - t)


def _softplus(x):
    return jnp.maximum(x, 0.0) + jnp.log1p(jnp.exp(-jnp.abs(x)))


def _rms(x, g):
    return x * lax.rsqrt(jnp.mean(x * x, axis=-1, keepdims=True) + NORM_EPS) * g


def _swiglu_res(x, g_norm, wg_ref, wu_ref, wd_ref):
    xn = _rms(x, g_norm).astype(BF16)
    yield
    g = _dot(xn, wg_ref[...])
    u = _dot(xn, wu_ref[...])
    yield
    a = (g * _sigmoid(g) * u).astype(BF16)
    yield
    d = _dot(a, wd_ref[...])
    yield
    return x + FFN_RES_SCALE * d


def _staggered(gens):
    pending = list(gens)
    active = []
    while pending or active:
        if pending:
            active.append(pending.pop(0))
        for g in list(active):
            try:
                next(g)
            except StopIteration:
                active.remove(g)


def _row_tiles(n_rows):
    sub = ROW_SUBTILE if n_rows % ROW_SUBTILE == 0 else n_rows
    return [slice(r, r + sub) for r in range(0, n_rows, sub)]


def _full_spec(arr):
    nd = arr.ndim
    return pl.BlockSpec(arr.shape, lambda *_: (0,) * nd, pipeline_mode=pl.Buffered(1))


def _layer_spec(arr, l):
    nd = arr.ndim - 1
    return pl.BlockSpec((None,) + arr.shape[1:], lambda *_: (l,) + (0,) * nd, pipeline_mode=pl.Buffered(1))


def _row(ref, l):
    return ref[l:l + 1, :]


def _k1_kernel(x_ref, n1_ref, wg_ref, wu_ref, wd_ref, nm_ref, win_ref, h_ref, prw_ref, psl_ref, *, l):
    def rows_pipeline(rows):
        h = yield from _swiglu_res(x_ref[rows, :], _row(n1_ref, l), wg_ref, wu_ref, wd_ref)
        h_ref[rows, :] = h
        hn = _rms(h, _row(nm_ref, l)).astype(BF16)
        yield
        cut0, cut1 = S5_WIDTH + LANE, S5_WIDTH + RWKV_PROJ
        head = _dot(hn, win_ref[:, 0:cut0])
        psl_ref[rows, 0:S5_WIDTH] = head[:, 0:S5_WIDTH]
        prw_ref[rows, 0:LANE] = head[:, S5_WIDTH:cut0]
        prw_ref[rows, LANE:RWKV_PROJ] = _dot(hn, win_ref[:, cut0:cut1])
        psl_ref[rows, S5_WIDTH:SL_IN] = _dot(hn, win_ref[:, cut1:D_IN])

    _staggered([rows_pipeline(rows) for rows in _row_tiles(x_ref.shape[0])])


def _k1(x, w, l, tl):
    B, L, _ = x.shape
    return pl.pallas_call(
        functools.partial(_k1_kernel, l=l),
        grid=(B, L // tl),
        in_specs=[
            pl.BlockSpec((None, tl, D_MODEL), lambda b, i: (b, i, 0)),
            _full_spec(w["n1"]),
            _full_spec(w["wg1"]),
            _full_spec(w["wu1"]),
            _full_spec(w["wd1"]),
            _full_spec(w["nm"]),
            _full_spec(w["win"]),
        ],
        out_specs=[
            pl.BlockSpec((None, tl, D_MODEL), lambda b, i: (b, i, 0)),
            pl.BlockSpec((None, tl, RWKV_PROJ), lambda b, i: (b, i, 0)),
            pl.BlockSpec((tl, SL_IN), lambda b, i: (i, b)),
        ],
        out_shape=[
            jax.ShapeDtypeStruct((B, L, D_MODEL), F32),
            jax.ShapeDtypeStruct((B, L, RWKV_PROJ), F32),
            jax.ShapeDtypeStruct((L, B * SL_IN), F32),
        ],
        compiler_params=pltpu.CompilerParams(
            dimension_semantics=("parallel", "parallel"), vmem_limit_bytes=VMEM_LIMIT),
        name="ffn1_inproj",
    )(x, w["n1"], w["wg1"], w["wu1"], w["wd1"], w["nm"], w["win"])


def _k3_kernel(h_ref, osl_ref, orw_ref, wout_ref, n2_ref, wg_ref, wu_ref, wd_ref, fn_ref, o_ref, *, l, final):
    def rows_pipeline(rows):
        osl = osl_ref[rows, :]
        mixed = jnp.concatenate([osl[:, 0:S5_WIDTH], orw_ref[rows, :], osl[:, S5_WIDTH:SL_OUT]], axis=-1)
        mix = _dot(mixed.astype(BF16), wout_ref[...])
        yield
        h3 = yield from _swiglu_res(h_ref[rows, :] + mix, _row(n2_ref, l), wg_ref, wu_ref, wd_ref)
        if final:
            h3 = _rms(h3, fn_ref[...])
        o_ref[rows, :] = h3

    _staggered([rows_pipeline(rows) for rows in _row_tiles(h_ref.shape[0])])


def _k3(h, osl, orw, w, l, tl, final):
    B, L, _ = h.shape
    return pl.pallas_call(
        functools.partial(_k3_kernel, l=l, final=final),
        grid=(B, L // tl),
        in_specs=[
            pl.BlockSpec((None, tl, D_MODEL), lambda b, i: (b, i, 0)),
            pl.BlockSpec((tl, SL_OUT), lambda b, i: (i, b)),
            pl.BlockSpec((None, tl, RWKV_WIDTH), lambda b, i: (b, i, 0)),
            _full_spec(w["wout"]),
            _full_spec(w["n2"]),
            _full_spec(w["wg2"]),
            _full_spec(w["wu2"]),
            _full_spec(w["wd2"]),
            _full_spec(w["fnorm"]),
        ],
        out_specs=pl.BlockSpec((None, tl, D_MODEL), lambda b, i: (b, i, 0)),
        out_shape=jax.ShapeDtypeStruct((B, L, D_MODEL), F32),
        compiler_params=pltpu.CompilerParams(
            dimension_semantics=("parallel", "parallel"), vmem_limit_bytes=VMEM_LIMIT),
        name="outproj_ffn2",
    )(h, osl, orw, w["wout"], w["n2"], w["wg2"], w["wu2"], w["wd2"], w["fnorm"])


def _stream_spec(arr, l, axis, first, count):
    shape = [None] + list(arr.shape[1:])
    shape[axis] = WEIGHT_CHUNK

    def index(c):
        idx = [l] + [0] * (arr.ndim - 1)
        idx[axis] = jnp.clip(c - first, 0, count - 1)
        return tuple(idx)

    return pl.BlockSpec(tuple(shape), index)


def _cast_spec(rows, cols, axis, first, count):
    shape = [rows, cols]
    shape[axis] = WEIGHT_CHUNK

    def index(c):
        idx = [0, 0]
        idx[axis] = jnp.clip(c - first, 0, count - 1)
        return tuple(idx)

    return pl.BlockSpec(tuple(shape), index)


def _ffn_chunk(c, first, count, xn_scr, acc_scr, wg_ref, wu_ref, wd_ref, wgb_ref, wub_ref, wdb_ref):
    @pl.when((c >= first) & (c < first + count))
    def _():
        wg = wg_ref[...].astype(BF16)
        wu = wu_ref[...].astype(BF16)
        wd = wd_ref[...].astype(BF16)
        wgb_ref[...] = wg
        wub_ref[...] = wu
        wdb_ref[...] = wd
        xn = xn_scr[...]
        g = _dot(xn, wg)
        u = _dot(xn, wu)
        a = (g * _sigmoid(g) * u).astype(BF16)
        acc_scr[...] += _dot(a, wd)


def _k1_stream_kernel(x_ref, n1_ref, nm_ref, wg_ref, wu_ref, wd_ref, win_ref,
                      h_ref, prw_ref, psl_ref, wgb_ref, wub_ref, wdb_ref, winb_ref,
                      xn_scr, acc_scr, proj_scr, *, l, nf, ni):
    c = pl.program_id(0)

    @pl.when(c == 0)
    def _():
        xn_scr[...] = _rms(x_ref[...], _row(n1_ref, l)).astype(BF16)
        acc_scr[...] = jnp.zeros(acc_scr.shape, F32)

    _ffn_chunk(c, 0, nf, xn_scr, acc_scr, wg_ref, wu_ref, wd_ref, wgb_ref, wub_ref, wdb_ref)

    @pl.when(c == nf - 1)
    def _():
        h = x_ref[...] + FFN_RES_SCALE * acc_scr[...]
        h_ref[...] = h
        xn_scr[...] = _rms(h, _row(nm_ref, l)).astype(BF16)

    @pl.when(c >= nf)
    def _():
        win = win_ref[...].astype(BF16)
        winb_ref[...] = win
        col = pl.multiple_of((c - nf) * WEIGHT_CHUNK, WEIGHT_CHUNK)
        proj_scr[:, pl.ds(col, WEIGHT_CHUNK)] = _dot(xn_scr[...], win)

    @pl.when(c == nf + ni - 1)
    def _():
        psl_ref[:, 0:S5_WIDTH] = proj_scr[:, 0:S5_WIDTH]
        prw_ref[...] = proj_scr[:, S5_WIDTH:S5_WIDTH + RWKV_PROJ]
        psl_ref[:, S5_WIDTH:SL_IN] = proj_scr[:, S5_WIDTH + RWKV_PROJ:D_IN]


def _k1_stream(x, a, l):
    NS = x.shape[0]
    nf, ni = D_FF // WEIGHT_CHUNK, D_IN // WEIGHT_CHUNK
    const = lambda shape: pl.BlockSpec(shape, lambda c: (0,) * len(shape))
    return pl.pallas_call(
        functools.partial(_k1_stream_kernel, l=l, nf=nf, ni=ni),
        grid=(nf + ni,),
        in_specs=[
            const((NS, D_MODEL)),
            _full_spec(a["ffn1_norm"]),
            _full_spec(a["mix_norm"]),
            _stream_spec(a["ffn1_w_gate"], l, 2, 0, nf),
            _stream_spec(a["ffn1_w_up"], l, 2, 0, nf),
            _stream_spec(a["ffn1_w_down"], l, 1, 0, nf),
            _stream_spec(a["w_in"], l, 2, nf, ni),
        ],
        out_specs=[
            const((NS, D_MODEL)),
            const((NS, RWKV_PROJ)),
            const((NS, SL_IN)),
            _cast_spec(D_MODEL, D_FF, 1, 0, nf),
            _cast_spec(D_MODEL, D_FF, 1, 0, nf),
            _cast_spec(D_FF, D_MODEL, 0, 0, nf),
            _cast_spec(D_MODEL, D_IN, 1, nf, ni),
        ],
        out_shape=[
            jax.ShapeDtypeStruct((NS, D_MODEL), F32),
            jax.ShapeDtypeStruct((NS, RWKV_PROJ), F32),
            jax.ShapeDtypeStruct((NS, SL_IN), F32),
            jax.ShapeDtypeStruct((D_MODEL, D_FF), BF16),
            jax.ShapeDtypeStruct((D_MODEL, D_FF), BF16),
            jax.ShapeDtypeStruct((D_FF, D_MODEL), BF16),
            jax.ShapeDtypeStruct((D_MODEL, D_IN), BF16),
        ],
        scratch_shapes=[
            pltpu.VMEM((NS, D_MODEL), BF16),
            pltpu.VMEM((NS, D_MODEL), F32),
            pltpu.VMEM((NS, D_IN), F32),
        ],
        compiler_params=pltpu.CompilerParams(dimension_semantics=("arbitrary",), vmem_limit_bytes=VMEM_LIMIT),
        name="ffn1_inproj_stream",
    )(x, a["ffn1_norm"], a["mix_norm"], a["ffn1_w_gate"], a["ffn1_w_up"], a["ffn1_w_down"], a["w_in"])


def _k3_stream_kernel(h_ref, osl_ref, orw_ref, n2_ref, fn_ref, wout_ref, wg_ref, wu_ref, wd_ref,
                      o_ref, woutb_ref, wgb_ref, wub_ref, wdb_ref,
                      mix_scr, h2_scr, xn_scr, acc_scr, *, l, no, nf, final):
    c = pl.program_id(0)

    @pl.when(c == 0)
    def _():
        osl = osl_ref[...]
        mix_scr[...] = jnp.concatenate(
            [osl[:, 0:S5_WIDTH], orw_ref[...], osl[:, S5_WIDTH:SL_OUT]], axis=-1).astype(BF16)
        h2_scr[...] = h_ref[...]

    @pl.when(c < no)
    def _():
        wout = wout_ref[...].astype(BF16)
        woutb_ref[...] = wout
        row = pl.multiple_of(c * WEIGHT_CHUNK, WEIGHT_CHUNK)
        h2_scr[...] += _dot(mix_scr[:, pl.ds(row, WEIGHT_CHUNK)], wout)

    @pl.when(c == no - 1)
    def _():
        xn_scr[...] = _rms(h2_scr[...], _row(n2_ref, l)).astype(BF16)
        acc_scr[...] = jnp.zeros(acc_scr.shape, F32)

    _ffn_chunk(c, no, nf, xn_scr, acc_scr, wg_ref, wu_ref, wd_ref, wgb_ref, wub_ref, wdb_ref)

    @pl.when(c == no + nf - 1)
    def _():
        h3 = h2_scr[...] + FFN_RES_SCALE * acc_scr[...]
        if final:
            h3 = _rms(h3, fn_ref[...])
        o_ref[...] = h3


def _k3_stream(h, osl, orw, a, fnorm, l, final):
    NS = h.shape[0]
    no, nf = D_MODEL // WEIGHT_CHUNK, D_FF // WEIGHT_CHUNK
    const = lambda shape: pl.BlockSpec(shape, lambda c: (0,) * len(shape))
    return pl.pallas_call(
        functools.partial(_k3_stream_kernel, l=l, no=no, nf=nf, final=final),
        grid=(no + nf,),
        in_specs=[
            const((NS, D_MODEL)),
            const((NS, SL_OUT)),
            const((NS, RWKV_WIDTH)),
            _full_spec(a["ffn2_norm"]),
            _full_spec(fnorm),
            _stream_spec(a["w_out"], l, 1, 0, no),
            _stream_spec(a["ffn2_w_gate"], l, 2, no, nf),
            _stream_spec(a["ffn2_w_up"], l, 2, no, nf),
            _stream_spec(a["ffn2_w_down"], l, 1, no, nf),
        ],
        out_specs=[
            const((NS, D_MODEL)),
            _cast_spec(D_MODEL, D_MODEL, 0, 0, no),
            _cast_spec(D_MODEL, D_FF, 1, no, nf),
            _cast_spec(D_MODEL, D_FF, 1, no, nf),
            _cast_spec(D_FF, D_MODEL, 0, no, nf),
        ],
        out_shape=[
            jax.ShapeDtypeStruct((NS, D_MODEL), F32),
            jax.ShapeDtypeStruct((D_MODEL, D_MODEL), BF16),
            jax.ShapeDtypeStruct((D_MODEL, D_FF), BF16),
            jax.ShapeDtypeStruct((D_MODEL, D_FF), BF16),
            jax.ShapeDtypeStruct((D_FF, D_MODEL), BF16),
        ],
        scratch_shapes=[
            pltpu.VMEM((NS, D_MODEL), BF16),
            pltpu.VMEM((NS, D_MODEL), F32),
            pltpu.VMEM((NS, D_MODEL), BF16),
            pltpu.VMEM((NS, D_MODEL), F32),
        ],
        compiler_params=pltpu.CompilerParams(dimension_semantics=("arbitrary",), vmem_limit_bytes=VMEM_LIMIT),
        name="outproj_ffn2_stream",
    )(h, osl, orw, a["ffn2_norm"], fnorm, a["w_out"], a["ffn2_w_gate"], a["ffn2_w_up"], a["ffn2_w_down"])


def _s5_prep_kernel(lr_ref, li_ref, ldt_ref, br_ref, bi_ref, abr_ref, abi_ref, bbr_ref, bbi_ref):
    lr = lr_ref[...]
    li = li_ref[...]
    dt = jnp.exp(ldt_ref[...])
    mag = jnp.exp(lr * dt)
    ab_re = mag * jnp.cos(li * dt)
    ab_im = mag * jnp.sin(li * dt)
    den = lr * lr + li * li
    f_re = ((ab_re - 1.0) * lr + ab_im * li) / den
    f_im = (ab_im * lr - (ab_re - 1.0) * li) / den
    abr_ref[...] = ab_re
    abi_ref[...] = ab_im
    br = br_ref[...]
    bi = bi_ref[...]
    bbr_ref[...] = f_re[:, :, None, :] * br - f_im[:, :, None, :] * bi
    bbi_ref[...] = f_re[:, :, None, :] * bi + f_im[:, :, None, :] * br


def _s5_prep(lam_re, lam_im, log_dt, b_re, b_im):
    G, N, C = S5_GROUPS, S5_STATE, S5_GROUP
    ldt = jnp.broadcast_to(log_dt[:, :, None], (DEPTH, G, N))
    br = jnp.swapaxes(b_re, 2, 3)
    bi = jnp.swapaxes(b_im, 2, 3)
    return pl.pallas_call(
        _s5_prep_kernel,
        out_shape=[
            jax.ShapeDtypeStruct((DEPTH, G, N), F32),
            jax.ShapeDtypeStruct((DEPTH, G, N), F32),
            jax.ShapeDtypeStruct((DEPTH, G, C, N), F32),
            jax.ShapeDtypeStruct((DEPTH, G, C, N), F32),
        ],
        name="s5_discretise",
    )(lam_re, lam_im, ldt, br, bi)


def _s5lru_kernel(psl_ref, ab_ref, win_ref, cre_ref, cim_ref, d_ref, gw_ref, gb_ref,
                  cw_ref, cb_ref, wa_ref, wx_ref, ba_ref, bx_ref, lam_ref,
                  h0_ref, l0_ref, c0_ref,
                  o_ref, hT_ref, lT_ref, cT_ref,
                  hs_scr, ls_scr, xc_scr, xr_scr, xi_scr, la_scr, lb_scr, pin_scr, pout_scr, *, l, B, Tc, unroll):
    c = pl.program_id(0)
    R = Tc * B

    @pl.when(c == 0)
    def _():
        hs_scr[...] = h0_ref[...]
        ls_scr[...] = l0_ref[...]
        xc_scr[0:CONV_HIST] = c0_ref[...]

    for b in range(B):
        for j in range(SL_IN // LANE):
            pin_scr[j, pl.ds(b, Tc, stride=B), :] = psl_ref[:, SL_IN * b + LANE * j:SL_IN * b + LANE * (j + 1)]
    psl = jnp.concatenate([pin_scr[j] for j in range(SL_IN // LANE)], axis=-1).reshape(Tc, B, SL_IN)
    u = psl[:, :, 0:S5_WIDTH].reshape(R, S5_WIDTH)
    gate_in = psl[:, :, S5_WIDTH + LRU_WIDTH:SL_IN].reshape(R, LRU_WIDTH)

    ub = u.astype(BF16)
    half = S5_BLOCK_STATES
    for j in range(S5_BLOCKS):
        x = _dot(ub[:, LANE * j:LANE * (j + 1)], win_ref[j])
        xr_scr[:, :, half * j:half * (j + 1)] = x[:, 0:half].reshape(Tc, B, half)
        xi_scr[:, :, half * j:half * (j + 1)] = x[:, half:2 * half].reshape(Tc, B, half)

    xc_scr[CONV_HIST:CONV_HIST + Tc] = psl[:, :, S5_WIDTH:S5_WIDTH + LRU_WIDTH]
    cw = cw_ref[...]
    xc = _row(cb_ref, l) + xc_scr[0:Tc] * cw[0:1]
    for j in range(1, CONV_WIDTH):
        xc = xc + xc_scr[j:j + Tc] * cw[j:j + 1]
    new_hist = xc_scr[Tc:Tc + CONV_HIST]
    xc_scr[0:CONV_HIST] = new_hist
    xc2 = xc.reshape(R, LRU_WIDTH)
    xcb = xc2.astype(BF16)
    gate_a = _sigmoid(_dot(xcb, wa_ref[...]) + _row(ba_ref, l))
    gate_x = _sigmoid(_dot(xcb, wx_ref[...]) + _row(bx_ref, l))
    log_a = LRU_C * gate_a * (-_softplus(-_row(lam_ref, l)))
    la_scr[...] = jnp.exp(log_a).reshape(Tc, B, LRU_WIDTH)
    lb_scr[...] = (jnp.sqrt(_neg_expm1(2.0 * log_a)) * (gate_x * xc2)).reshape(Tc, B, LRU_WIDTH)

    ar = jnp.broadcast_to(ab_ref[0:1, :], (B, S5_LANES))
    ai = jnp.broadcast_to(ab_ref[1:2, :], (B, S5_LANES))

    def step(t, carry):
        hr, hi, hl = carry
        nr = ar * hr - ai * hi + xr_scr[t]
        ni = ar * hi + ai * hr + xi_scr[t]
        xr_scr[t] = nr
        xi_scr[t] = ni
        nl = la_scr[t] * hl + lb_scr[t]
        lb_scr[t] = nl
        return nr, ni, nl

    hr, hi, hl = lax.fori_loop(0, Tc, step, (hs_scr[0], hs_scr[1], ls_scr[...]), unroll=unroll)
    hs_scr[0] = hr
    hs_scr[1] = hi
    ls_scr[...] = hl

    hrb = xr_scr[...].reshape(R, S5_LANES).astype(BF16)
    hib = xi_scr[...].reshape(R, S5_LANES).astype(BF16)
    ys = []
    for j in range(S5_BLOCKS):
        sl = slice(half * j, half * (j + 1))
        ys.append(_dot(hrb[:, sl], cre_ref[j]) - _dot(hib[:, sl], cim_ref[j]))
    y = jnp.concatenate(ys, axis=-1) + _row(d_ref, l) * u
    z = _gelu(y)
    o_s5 = z * _sigmoid(_dot(z.astype(BF16), gw_ref[...].astype(BF16)) + _row(gb_ref, l))
    o_lru = lb_scr[...].reshape(R, LRU_WIDTH) * _gelu(gate_in)
    out = jnp.concatenate([o_s5, o_lru], axis=-1)
    for j in range(SL_OUT // LANE):
        pout_scr[j] = out[:, LANE * j:LANE * (j + 1)]
    for b in range(B):
        for j in range(SL_OUT // LANE):
            o_ref[:, SL_OUT * b + LANE * j:SL_OUT * b + LANE * (j + 1)] = pout_scr[j, pl.ds(b, Tc, stride=B), :]

    @pl.when(c == pl.num_programs(0) - 1)
    def _():
        hT_ref[...] = hs_scr[...]
        lT_ref[...] = ls_scr[...]
        cT_ref[...] = xc_scr[0:CONV_HIST]


def _s5lru(psl2, B, w, l, st, Tc):
    L = psl2.shape[0]
    li = st["li"]
    unroll = True
    return pl.pallas_call(
        functools.partial(_s5lru_kernel, l=l, B=B, Tc=Tc, unroll=unroll),
        grid=(L // Tc,),
        in_specs=[
            pl.BlockSpec((Tc, B * SL_IN), lambda c: (c, 0)),
            _layer_spec(w["s5_ab"], l),
            _layer_spec(w["s5_win"], l),
            _layer_spec(w["s5_cre"], l),
            _layer_spec(w["s5_cim"], l),
            _full_spec(w["s5_d"]),
            _layer_spec(w["s5_glu_w"], l),
            _full_spec(w["s5_glu_b"]),
            _layer_spec(w["lru_conv_w"], l),
            _full_spec(w["lru_conv_b"]),
            _layer_spec(w["lru_wa"], l),
            _layer_spec(w["lru_wx"], l),
            _full_spec(w["lru_b_a"]),
            _full_spec(w["lru_b_x"]),
            _full_spec(w["lru_lambda"]),
            _layer_spec(st["s5"], li),
            _layer_spec(st["lru"], li),
            _layer_spec(st["conv"], li),
        ],
        out_specs=[
            pl.BlockSpec((Tc, B * SL_OUT), lambda c: (c, 0)),
            pl.BlockSpec((2, B, S5_LANES), lambda c: (0, 0, 0)),
            pl.BlockSpec((B, LRU_WIDTH), lambda c: (0, 0)),
            pl.BlockSpec((CONV_HIST, B, LRU_WIDTH), lambda c: (0, 0, 0)),
        ],
        out_shape=[
            jax.ShapeDtypeStruct((L, B * SL_OUT), F32),
            jax.ShapeDtypeStruct((2, B, S5_LANES), F32),
            jax.ShapeDtypeStruct((B, LRU_WIDTH), F32),
            jax.ShapeDtypeStruct((CONV_HIST, B, LRU_WIDTH), F32),
        ],
        scratch_shapes=[
            pltpu.VMEM((2, B, S5_LANES), F32),
            pltpu.VMEM((B, LRU_WIDTH), F32),
            pltpu.VMEM((Tc + CONV_HIST, B, LRU_WIDTH), F32),
            pltpu.VMEM((Tc, B, S5_LANES), F32),
            pltpu.VMEM((Tc, B, S5_LANES), F32),
            pltpu.VMEM((Tc, B, LRU_WIDTH), F32),
            pltpu.VMEM((Tc, B, LRU_WIDTH), F32),
            pltpu.VMEM((SL_IN // LANE, Tc * B, LANE), F32),
            pltpu.VMEM((SL_OUT // LANE, Tc * B, LANE), F32),
        ],
        compiler_params=pltpu.CompilerParams(
            dimension_semantics=("arbitrary",), vmem_limit_bytes=VMEM_LIMIT),
        name="s5_rglru_scan",
    )(psl2, w["s5_ab"], w["s5_win"], w["s5_cre"], w["s5_cim"], w["s5_d"], w["s5_glu_w"], w["s5_glu_b"],
      w["lru_conv_w"], w["lru_conv_b"], w["lru_wa"], w["lru_wx"], w["lru_b_a"], w["lru_b_x"], w["lru_lambda"],
      st["s5"], st["lru"], st["conv"])


_RWKV_PARAMS = ("rwkv_mu", "rwkv_w0", "rwkv_w_up", "rwkv_a0", "rwkv_a_up", "rwkv_g_up", "rwkv_k_k", "rwkv_k_a",
                "rwkv_r_k", "rwkv_ln_w", "rwkv_ln_b", "rwkv_ones")


def _rwkv_param_specs(w, l):
    layered = ("rwkv_w_up", "rwkv_a_up", "rwkv_g_up")
    return [_layer_spec(w[k], l) if k in layered else _full_spec(w[k]) for k in _RWKV_PARAMS]


def _rwkv_mats(prm):
    wup_ref, aup_ref, gup_ref, ones_ref = prm[2], prm[4], prm[5], prm[11]
    zpad = jnp.zeros((DECAY_RANK, RWKV_WIDTH), BF16)
    wup = jnp.concatenate([wup_ref[...].astype(BF16), zpad], axis=0)
    aup = jnp.concatenate([zpad, aup_ref[...].astype(BF16)], axis=0)
    return wup, aup, gup_ref[...].astype(BF16), ones_ref[...]


def _segsum(x, ones):
    xb = x.astype(BF16)
    return jnp.concatenate([_dot(xb[:, LANE * p:LANE * (p + 1)], ones) for p in range(RWKV_PAIRS)], axis=-1)


def _rwkv_token_math(p, prev, prm, mats, l):
    mu_ref, w0_ref, _, a0_ref, _, _, kk_ref, ka_ref, rk_ref = prm[0:9]
    wup, aup, gup, ones = mats
    W = RWKV_WIDTH
    xm = p + (prev - p) * _row(mu_ref, l)
    r = xm[:, 0:W]
    k = xm[:, W:2 * W]
    v = xm[:, 2 * W:3 * W]
    xwa = xm[:, 3 * W:3 * W + DECAY_RANK + ICL_RANK]
    xg = xm[:, 3 * W + DECAY_RANK + ICL_RANK:RWKV_PROJ]
    lw = _row(w0_ref, l) + _dot(jnp.tanh(xwa).astype(BF16), wup)
    ld = -math.exp(-0.5) * _sigmoid(lw)
    a = _sigmoid(_row(a0_ref, l) + _dot(xwa.astype(BF16), aup))
    g = _dot(_sigmoid(xg).astype(BF16), gup)
    kkr = k * _row(kk_ref, l)
    kk = kkr * lax.rsqrt(jnp.maximum(_segsum(kkr * kkr, ones), 1e-24))
    k2 = k * (1.0 + (a - 1.0) * _row(ka_ref, l))
    bonus = _segsum(r * k2 * _row(rk_ref, l), ones) * v
    return r, k2, v, kk, a, ld, g, bonus


def _rwkv_out(y, bonus, g, prm, ones, l):
    lnw_ref, lnb_ref = prm[9], prm[10]
    inv = 1.0 / RWKV_HEAD
    mean = _segsum(y, ones) * inv
    d = y - mean
    var = _segsum(d * d, ones) * inv
    yn = d * lax.rsqrt(var + RWKV_LN_EPS) * _row(lnw_ref, l) + _row(lnb_ref, l)
    return (yn + bonus) * g


RWKV_GROUP_HEADS = 4
RWKV_GROUP_LANES = RWKV_GROUP_HEADS * RWKV_HEAD
_RWKV_OPERANDS = ("a", "b", "k", "r", "v", "bp", "kp")


def _rwkv_groups(B):
    full = [[(b, 0, RWKV_GROUP_LANES)] for b in range(B)]
    rest = RWKV_WIDTH - RWKV_GROUP_LANES
    tail = [[(b, RWKV_GROUP_LANES, rest), (b + 1, RWKV_GROUP_LANES, rest)] for b in range(0, B, 2)]
    return full + tail


def _rwkv_chunk_kernel(*refs, l, B, Tc):
    n_prm = len(_RWKV_PARAMS)
    p_ref, s0_ref, sh0_ref = refs[0:3]
    prm = refs[3:3 + n_prm]
    tri_ref, o_ref, sT_ref = refs[3 + n_prm:6 + n_prm]
    st_scr, xs_scr, y_scr, pt_scr = refs[6 + n_prm:10 + n_prm]
    src = dict(zip(_RWKV_OPERANDS, refs[10 + n_prm:]))
    c = pl.program_id(0)
    R = B * Tc
    W = RWKV_WIDTH
    GL = RWKV_GROUP_LANES
    GH = RWKV_GROUP_HEADS
    carry_row = SUBLANE - 1

    @pl.when(c == 0)
    def _():
        st_scr[...] = s0_ref[...]
        xs_scr[:, carry_row:SUBLANE, :] = sh0_ref[...]

    ps, prevs = [], []
    for b in range(B):
        pb = p_ref[b]
        prevs.append(jnp.concatenate([xs_scr[b, carry_row:SUBLANE, :], pb[0:Tc - 1, :]], axis=0))
        xs_scr[b, carry_row:SUBLANE, :] = pb[Tc - 1:Tc, :]
        ps.append(pb)
    mats = _rwkv_mats(prm)
    r, k2, v, kk, a, ld, g, bonus = _rwkv_token_math(
        jnp.concatenate(ps, axis=0), jnp.concatenate(prevs, axis=0), prm, mats, l)

    h1, h2, h3 = _split3(ld)
    tri = tri_ref[...]
    cls = []
    for b in range(B):
        rows = slice(b * Tc, (b + 1) * Tc)
        cls.append(_dot(tri, h1[rows]) + _dot(tri, h2[rows]) + _dot(tri, h3[rows]))
    cl = jnp.concatenate(cls, axis=0)
    pt = jnp.exp(cl.reshape(B, Tc, W)[:, Tc - 1:Tc, :])
    inv_p = jnp.exp(-cl)
    to_end = (pt * inv_p.reshape(B, Tc, W)).reshape(R, W)
    beta = kk * a
    src["a"][...] = (-kk * jnp.exp(cl - ld)).astype(BF16)
    src["b"][...] = beta * inv_p
    src["k"][...] = k2 * inv_p
    src["r"][...] = (r * jnp.exp(cl)).astype(BF16)
    src["v"][...] = v.astype(BF16)
    src["bp"][...] = (beta * to_end).astype(BF16)
    src["kp"][...] = (k2 * to_end).astype(BF16)
    pt_scr[...] = pt

    assert Tc & (Tc - 1) == 0
    n_doubling = max(1, (Tc - 1).bit_length())
    groups = _rwkv_groups(B)

    def block_masks(blk):
        lane = lax.broadcasted_iota(jnp.int32, (1, GH * blk), 1)
        return [((lane >= blk * h) & (lane < blk * (h + 1))).astype(BF16) for h in range(GH)]

    head_masks = block_masks(RWKV_HEAD)
    time_masks = block_masks(Tc)

    def stack(x, masks):
        return jnp.concatenate([x * m.astype(x.dtype) for m in masks], axis=0)

    ti = lax.broadcasted_iota(jnp.int32, (Tc, GH * Tc), 0)
    si = lax.broadcasted_iota(jnp.int32, (Tc, GH * Tc), 1) & (Tc - 1)
    strict = (si < ti).astype(BF16)
    incl = (si <= ti).astype(BF16)
    same_head = ((lax.broadcasted_iota(jnp.int32, (GL, GL), 0) // RWKV_HEAD)
                 == (lax.broadcasted_iota(jnp.int32, (GL, GL), 1) // RWKV_HEAD)).astype(F32)

    def operand(name, grp):
        parts = [src[name][b * Tc:(b + 1) * Tc, lo:lo + n] for b, lo, n in grp]
        return parts[0] if len(parts) == 1 else jnp.concatenate(parts, axis=-1)

    v4s, gss, ns, wrbs, wrks, xs = [], [], [], [], [], []
    for gi, grp in enumerate(groups):
        lhs = jnp.concatenate([operand("a", grp), operand("r", grp)], axis=0)
        v4 = stack(operand("v", grp), head_masks)
        gb = _dot_nt(lhs, stack(operand("b", grp), head_masks)).astype(BF16)
        gk = _dot_nt(lhs, stack(operand("k", grp), head_masks)).astype(BF16)
        gs = _dot_nt(lhs, st_scr[gi])
        v4s.append(v4)
        gss.append(gs[Tc:2 * Tc])
        ns.append(gb[0:Tc] * strict)
        wrbs.append(gb[Tc:2 * Tc] * incl)
        wrks.append(gk[Tc:2 * Tc] * incl)
        xs.append(gs[0:Tc] + _dot(gk[0:Tc] * strict, v4))
    for j in range(n_doubling):
        for i in range(len(groups)):
            xs[i] = xs[i] + _dot(ns[i], stack(xs[i].astype(BF16), head_masks))
        if j + 1 < n_doubling:
            for i in range(len(groups)):
                ns[i] = _dot(ns[i], stack(ns[i], time_masks)).astype(BF16)
    for gi, grp in enumerate(groups):
        sab = xs[gi].astype(BF16)
        y = gss[gi] + _dot(wrbs[gi], stack(sab, head_masks)) + _dot(wrks[gi], v4s[gi])
        lane0 = 0
        for b, lo, n in grp:
            y_scr[b * Tc:(b + 1) * Tc, lo:lo + n] = y[:, lane0:lane0 + n]
            lane0 += n
        upd = _dot_tn(jnp.concatenate([sab, operand("v", grp)], axis=0),
                      jnp.concatenate([operand("bp", grp), operand("kp", grp)], axis=0))
        ptg = [pt_scr[b, :, lo:lo + n] for b, lo, n in grp]
        ptg = ptg[0] if len(ptg) == 1 else jnp.concatenate(ptg, axis=-1)
        st_scr[gi] = st_scr[gi] * ptg + upd * same_head

    out = _rwkv_out(y_scr[...], bonus, g, prm, mats[3], l)
    for b in range(B):
        o_ref[b] = out[b * Tc:(b + 1) * Tc]

    @pl.when(c == pl.num_programs(0) - 1)
    def _():
        sT_ref[...] = st_scr[...]


def _rwkv_chunk(prw, w, l, s0, sh0, Tc):
    B, L, _ = prw.shape
    R = B * Tc
    W = RWKV_WIDTH
    GL = RWKV_GROUP_LANES
    NG = len(_rwkv_groups(B))
    tri = jnp.tril(jnp.ones((Tc, Tc), F32)).astype(BF16)
    return pl.pallas_call(
        functools.partial(_rwkv_chunk_kernel, l=l, B=B, Tc=Tc),
        grid=(L // Tc,),
        in_specs=[
            pl.BlockSpec((B, Tc, RWKV_PROJ), lambda c: (0, c, 0)),
            _full_spec(s0),
            _full_spec(sh0),
        ] + _rwkv_param_specs(w, l) + [_full_spec(tri)],
        out_specs=[
            pl.BlockSpec((B, Tc, W), lambda c: (0, c, 0)),
            pl.BlockSpec((NG, GL, GL), lambda c: (0, 0, 0)),
        ],
        out_shape=[
            jax.ShapeDtypeStruct((B, L, W), F32),
            jax.ShapeDtypeStruct((NG, GL, GL), F32),
        ],
        scratch_shapes=[
            pltpu.VMEM((NG, GL, GL), F32),
            pltpu.VMEM((B, SUBLANE, RWKV_PROJ), F32),
            pltpu.VMEM((R, W), F32),
            pltpu.VMEM((B, 1, W), F32),
        ] + [pltpu.VMEM((R, W), F32 if name in ("b", "k") else BF16) for name in _RWKV_OPERANDS],
        compiler_params=pltpu.CompilerParams(
            dimension_semantics=("arbitrary",), vmem_limit_bytes=VMEM_LIMIT),
        name="rwkv7_chunked",
    )(prw, s0, sh0, *[w[k] for k in _RWKV_PARAMS], tri)


def _rwkv_step_kernel(*refs, l):
    n_prm = len(_RWKV_PARAMS)
    p_ref, sh_ref, s_ref = refs[0:3]
    prm = refs[3:3 + n_prm]
    o_ref, sn_ref = refs[3 + n_prm:5 + n_prm]
    nkk_t, w_t, beta_t, v_t, k2_t, r_t, y_t, g_scr, bonus_scr = refs[5 + n_prm:]
    h = pl.program_id(0)
    N = RWKV_HEAD

    @pl.when(h == 0)
    def _():
        r, k2, v, kk, a, ld, g, bonus = _rwkv_token_math(p_ref[...], sh_ref[...], prm, _rwkv_mats(prm), l)
        nkk_t[...] = (-kk).T
        w_t[...] = jnp.exp(ld).T
        beta_t[...] = (kk * a).T
        v_t[...] = v.T
        k2_t[...] = k2.T
        r_t[...] = r.T
        g_scr[...] = g
        bonus_scr[...] = bonus

    rows = pl.ds(pl.multiple_of(h * N, N), N)
    s = s_ref[...]
    sa = jnp.sum(s * nkk_t[rows, :][None], axis=1)
    sn = s * w_t[rows, :][None] + sa[:, None, :] * beta_t[rows, :][None] + v_t[rows, :][:, None, :] * k2_t[rows, :][None]
    sn_ref[...] = sn
    y_t[rows, :] = jnp.sum(sn * r_t[rows, :][None], axis=1)

    @pl.when(h == pl.num_programs(0) - 1)
    def _():
        o_ref[...] = _rwkv_out(y_t[...].T, bonus_scr[...], g_scr[...], prm, prm[11][...], l)


def _rwkv_step(prw, w, l, s_all, sh_all):
    B = prw.shape[0]
    W = RWKV_WIDTH
    N = RWKV_HEAD
    return pl.pallas_call(
        functools.partial(_rwkv_step_kernel, l=l),
        grid=(RWKV_HEADS,),
        in_specs=[
            _full_spec(prw),
            _layer_spec(sh_all, l),
            pl.BlockSpec((None, None, N, N, B), lambda h: (l, h, 0, 0, 0)),
        ] + _rwkv_param_specs(w, l),
        out_specs=[
            pl.BlockSpec((B, W), lambda h: (0, 0)),
            pl.BlockSpec((None, N, N, B), lambda h: (h, 0, 0, 0)),
        ],
        out_shape=[
            jax.ShapeDtypeStruct((B, W), F32),
            jax.ShapeDtypeStruct((RWKV_HEADS, N, N, B), F32),
        ],
        scratch_shapes=[pltpu.VMEM((W, B), F32)] * 7 + [pltpu.VMEM((B, W), F32)] * 2,
        compiler_params=pltpu.CompilerParams(
            dimension_semantics=("arbitrary",), vmem_limit_bytes=VMEM_LIMIT),
        name="rwkv7_step",
    )(prw, sh_all, s_all, *[w[k] for k in _RWKV_PARAMS])


def _block_diag(blocks):
    *lead, n, r, c = blocks.shape
    eye = jnp.eye(n, dtype=blocks.dtype)
    out = blocks[..., :, :, None, :] * eye[:, None, :, None]
    return out.reshape(*lead, n * r, n * c)


def _prepare_weights(a):
    G, C, N = S5_GROUPS, S5_GROUP, S5_STATE
    abr, abi, bbr, bbi = _s5_prep(a["s5_lambda_re"], a["s5_lambda_im"], a["s5_log_dt"], a["s5_b_re"], a["s5_b_im"])

    def in_side(bb):
        return _block_diag(bb.reshape(DEPTH, S5_BLOCKS, S5_GROUPS_PER_BLOCK, C, N))

    def out_side(cc):
        ct = jnp.swapaxes(cc, 2, 3)
        return _block_diag(ct.reshape(DEPTH, S5_BLOCKS, S5_GROUPS_PER_BLOCK, N, C))

    w = {k: a[k] for k in (
        "s5_d", "s5_glu_w", "s5_glu_b", "lru_conv_w", "lru_conv_b", "lru_b_a", "lru_b_x", "lru_lambda",
        "rwkv_mu", "rwkv_w0", "rwkv_w_up", "rwkv_a0", "rwkv_a_up", "rwkv_g_up", "rwkv_k_k", "rwkv_k_a",
        "rwkv_ln_w", "rwkv_ln_b")}
    w.update({
        "n1": a["ffn1_norm"], "nm": a["mix_norm"], "n2": a["ffn2_norm"],
        "fnorm": a["final_norm"].reshape(1, D_MODEL),
        "s5_ab": jnp.stack([abr.reshape(DEPTH, S5_LANES), abi.reshape(DEPTH, S5_LANES)], axis=1),
        "s5_win": jnp.concatenate([in_side(bbr), in_side(bbi)], axis=-1).astype(BF16),
        "s5_cre": out_side(a["s5_c_re"]).astype(BF16),
        "s5_cim": out_side(a["s5_c_im"]).astype(BF16),
        "lru_wa": _block_diag(a["lru_w_a"]).astype(BF16),
        "lru_wx": _block_diag(a["lru_w_x"]).astype(BF16),
        "rwkv_r_k": a["rwkv_r_k"].reshape(DEPTH, RWKV_WIDTH),
        "rwkv_ones": _block_diag(jnp.ones((2, RWKV_HEAD, RWKV_HEAD), F32)).astype(BF16),
    })
    return w


def _groups_to_heads(sg, B):
    N = RWKV_HEAD
    diag = jnp.stack([sg[:, N * i:N * (i + 1), N * i:N * (i + 1)] for i in range(RWKV_GROUP_HEADS)], axis=1)
    first = diag[:B]
    rest = diag[B:].reshape(B, RWKV_HEADS - RWKV_GROUP_HEADS, N, N)
    return jnp.concatenate([first, rest], axis=1)


def _seq_mixers(prw, psl2, st, w, l, Tc):
    L = prw.shape[1]
    osl, hT, lT, cT = _s5lru(psl2, prw.shape[0], w, l, st, Tc)
    orw, sT = _rwkv_chunk(prw, w, l, st["rwkv"], st["shift"], Tc)
    new = {"s5": hT[None], "lru": lT[None], "conv": cT[None], "li": 0,
           "rwkv": sT, "shift": prw[:, L - 1:L, :]}
    return osl, orw, new


def kernel(x_prompt, x_sample, state_s5_re, state_s5_im, state_rwkv, state_rwkv_shift, state_lru, state_lru_conv, meta_tokens, ffn1_norm, ffn1_w_gate, ffn1_w_up, ffn1_w_down, mix_norm, w_in, s5_lambda_re, s5_lambda_im, s5_log_dt, s5_b_re, s5_b_im, s5_c_re, s5_c_im, s5_d, s5_glu_w, s5_glu_b, rwkv_mu, rwkv_w0, rwkv_w_up, rwkv_a0, rwkv_a_up, rwkv_g_up, rwkv_k_k, rwkv_k_a, rwkv_r_k, rwkv_ln_w, rwkv_ln_b, lru_conv_w, lru_conv_b, lru_w_a, lru_b_a, lru_w_x, lru_b_x, lru_lambda, w_out, ffn2_norm, ffn2_w_gate, ffn2_w_up, ffn2_w_down, final_norm):
    a = dict(locals())
    w = _prepare_weights(a)
    BP, SEQ, _ = x_prompt.shape
    BS = x_sample.shape[0]
    assert SEQ % PROMPT_TOKEN_TILE == 0 and SEQ % PROMPT_CHUNK == 0 and BP % 2 == 0

    zero = {
        "s5": jnp.zeros((1, 2, BP, S5_LANES), F32),
        "lru": jnp.zeros((1, BP, LRU_WIDTH), F32),
        "conv": jnp.zeros((1, CONV_HIST, BP, LRU_WIDTH), F32),
        "li": 0,
        "rwkv": jnp.zeros((len(_rwkv_groups(BP)), RWKV_GROUP_LANES, RWKV_GROUP_LANES), F32),
        "shift": jnp.zeros((BP, 1, RWKV_PROJ), F32),
    }
    sample_in = {
        "s5": jnp.stack([state_s5_re.reshape(DEPTH, BS, S5_LANES), state_s5_im.reshape(DEPTH, BS, S5_LANES)], axis=1),
        "lru": state_lru,
        "conv": jnp.swapaxes(state_lru_conv, 1, 2),
    }

    rwkv_in = jnp.transpose(state_rwkv, (0, 2, 3, 4, 1))

    xp = x_prompt
    xs = jnp.concatenate([meta_tokens.astype(F32), x_sample.reshape(BS, D_MODEL)], axis=0)
    p_states = []
    s_states = []
    for l in range(DEPTH):
        final = l == DEPTH - 1
        hs, prw_s, psl_s, w["wg1"], w["wu1"], w["wd1"], w["win"] = _k1_stream(xs, a, l)
        prw_m = jnp.broadcast_to(prw_s[None, :N_META], (BP, N_META, RWKV_PROJ))
        psl_m = jnp.tile(psl_s[:N_META], (1, BP))
        osl_m, orw_m, st_meta = _seq_mixers(prw_m, psl_m, zero, w, l, N_META)

        st_l = dict(sample_in, li=l)
        osl_s, hT, lT, cT = _s5lru(psl_s[N_META:].reshape(1, BS * SL_IN), BS, w, l, st_l, 1)
        orw_s, sT = _rwkv_step(prw_s[N_META:], w, l, rwkv_in, state_rwkv_shift)
        s_states.append({"s5": hT, "lru": lT, "conv": cT, "rwkv": sT, "shift": prw_s[N_META:]})
        if final:
            rows = (hs[N_META:], osl_s.reshape(BS, SL_OUT), orw_s)
        else:
            rows = (hs, jnp.concatenate([osl_m[:, 0:SL_OUT], osl_s.reshape(BS, SL_OUT)], axis=0),
                    jnp.concatenate([orw_m[0], orw_s], axis=0))
        xs, w["wout"], w["wg2"], w["wu2"], w["wd2"] = _k3_stream(*rows, a, w["fnorm"], l, final)

        hp, prw_p, psl_p = _k1(xp, w, l, PROMPT_TOKEN_TILE)
        osl_p, orw_p, st_p = _seq_mixers(prw_p, psl_p, st_meta, w, l, PROMPT_CHUNK)
        p_states.append(st_p)
        xp = _k3(hp, osl_p, orw_p, w, l, PROMPT_TOKEN_TILE, final)

    p_out = (
        jnp.stack([s["s5"][0, 0].reshape(BP, S5_GROUPS, S5_STATE) for s in p_states]),
        jnp.stack([s["s5"][0, 1].reshape(BP, S5_GROUPS, S5_STATE) for s in p_states]),
        jnp.stack([_groups_to_heads(s["rwkv"], BP) for s in p_states]),
        jnp.stack([s["shift"][:, 0, :] for s in p_states]),
        jnp.stack([s["lru"][0] for s in p_states]),
        jnp.stack([jnp.swapaxes(s["conv"][0], 0, 1) for s in p_states]),
    )
    s_out = (
        jnp.stack([s["s5"][0].reshape(BS, S5_GROUPS, S5_STATE) for s in s_states]),
        jnp.stack([s["s5"][1].reshape(BS, S5_GROUPS, S5_STATE) for s in s_states]),
        jnp.transpose(jnp.stack([s["rwkv"] for s in s_states]), (0, 4, 1, 2, 3)),
        jnp.stack([s["shift"] for s in s_states]),
        jnp.stack([s["lru"] for s in s_states]),
        jnp.stack([jnp.swapaxes(s["conv"], 0, 1) for s in s_states]),
    )
    return (xp, xs.reshape(BS, 1, D_MODEL)) + p_out + s_out
```

```python
import functools
import math

import jax
import jax.numpy as jnp
from jax import lax
from jax.experimental import pallas as pl
from jax.experimental.pallas import tpu as pltpu

F32 = jnp.float32
BF16 = jnp.bfloat16

D_MODEL = 1024
DEPTH = 2
N_META = 16
D_FF = 2816
NORM_EPS = 1e-6
FFN_RES_SCALE = 0.5
S5_WIDTH = 384
S5_GROUP = 16
S5_GROUPS = 24
S5_STATE = 64
S5_LANES = S5_GROUPS * S5_STATE
RWKV_HEAD = 64
RWKV_WIDTH = 384
RWKV_HEADS = 6
RWKV_PAIRS = 3
DECAY_RANK = 64
ICL_RANK = 64
RWKV_PROJ = 1408
RWKV_LN_EPS = 64e-5
LRU_WIDTH = 256
CONV_WIDTH = 4
CONV_HIST = CONV_WIDTH - 1
LRU_C = 8.0
D_IN = 2304
SL_IN = S5_WIDTH + 2 * LRU_WIDTH
SL_OUT = S5_WIDTH + LRU_WIDTH
LANE = 128
SUBLANE = 8
S5_GROUPS_PER_BLOCK = LANE // S5_GROUP
S5_BLOCKS = S5_GROUPS // S5_GROUPS_PER_BLOCK
S5_BLOCK_STATES = S5_GROUPS_PER_BLOCK * S5_STATE

V7X_VMEM_BYTES = 64 * 2 ** 20
VMEM_LIMIT = (V7X_VMEM_BYTES * 7) // 8

PROMPT_TOKEN_TILE = 512
WEIGHT_CHUNK = 256
ROW_SUBTILE = 256
PROMPT_CHUNK = 64


def _dot(a, b):
    return jnp.dot(a, b, preferred_element_type=F32)


def _dot_nt(a, b):
    bt = jnp.transpose(b.astype(F32)).astype(BF16)
    return jnp.dot(a, bt, preferred_element_type=F32)


def _dot_tn(a, b):
    return lax.dot_general(a, b, (((0,), (0,)), ((), ())), preferred_element_type=F32)


def _split3(x):
    hi = x.astype(BF16)
    r = x - hi.astype(F32)
    mid = r.astype(BF16)
    lo = (r - mid.astype(F32)).astype(BF16)
    return hi, mid, lo


def _sigmoid(x):
    return 0.5 * (jnp.tanh(0.5 * x) + 1.0)


def _gelu(x):
    c = math.sqrt(2.0 / math.pi)
    return 0.5 * x * (1.0 + jnp.tanh(c * (x + 0.044715 * (x * x * x))))


def _neg_expm1(z):
    t = jnp.tanh(0.5 * z)
    return -2.0 * t / (1.0 - t)


def _softplus(x):
    return jnp.maximum(x, 0.0) + jnp.log1p(jnp.exp(-jnp.abs(x)))


def _rms(x, g):
    return x * lax.rsqrt(jnp.mean(x * x, axis=-1, keepdims=True) + NORM_EPS) * g


def _swiglu_res(x, g_norm, wg_ref, wu_ref, wd_ref):
    xn = _rms(x, g_norm).astype(BF16)
    yield
    g = _dot(xn, wg_ref[...])
    u = _dot(xn, wu_ref[...])
    yield
    a = (g * _sigmoid(g) * u).astype(BF16)
    yield
    d = _dot(a, wd_ref[...])
    yield
    return x + FFN_RES_SCALE * d


def _staggered(gens):
    pending = list(gens)
    active = []
    while pending or active:
        if pending:
            active.append(pending.pop(0))
        for g in list(active):
            try:
                next(g)
            except StopIteration:
                active.remove(g)


def _row_tiles(n_rows):
    sub = ROW_SUBTILE if n_rows % ROW_SUBTILE == 0 else n_rows
    return [slice(r, r + sub) for r in range(0, n_rows, sub)]


def _full_spec(arr):
    nd = arr.ndim
    return pl.BlockSpec(arr.shape, lambda *_: (0,) * nd, pipeline_mode=pl.Buffered(1))


def _layer_spec(arr, l):
    nd = arr.ndim - 1
    return pl.BlockSpec((None,) + arr.shape[1:], lambda *_: (l,) + (0,) * nd, pipeline_mode=pl.Buffered(1))


def _row(ref, l):
    return ref[l:l + 1, :]


def _k1_kernel(x_ref, n1_ref, wg_ref, wu_ref, wd_ref, nm_ref, win_ref, h_ref, prw_ref, psl_ref, *, l):
    def rows_pipeline(rows):
        h = yield from _swiglu_res(x_ref[rows, :], _row(n1_ref, l), wg_ref, wu_ref, wd_ref)
        h_ref[rows, :] = h
        hn = _rms(h, _row(nm_ref, l)).astype(BF16)
        yield
        cut0, cut1 = S5_WIDTH + LANE, S5_WIDTH + RWKV_PROJ
        head = _dot(hn, win_ref[:, 0:cut0])
        psl_ref[rows, 0:S5_WIDTH] = head[:, 0:S5_WIDTH]
        prw_ref[rows, 0:LANE] = head[:, S5_WIDTH:cut0]
        prw_ref[rows, LANE:RWKV_PROJ] = _dot(hn, win_ref[:, cut0:cut1])
        psl_ref[rows, S5_WIDTH:SL_IN] = _dot(hn, win_ref[:, cut1:D_IN])

    _staggered([rows_pipeline(rows) for rows in _row_tiles(x_ref.shape[0])])


def _k1(x, w, l, tl):
    B, L, _ = x.shape
    return pl.pallas_call(
        functools.partial(_k1_kernel, l=l),
        grid=(B, L // tl),
        in_specs=[
            pl.BlockSpec((None, tl, D_MODEL), lambda b, i: (b, i, 0)),
            _full_spec(w["n1"]),
            _full_spec(w["wg1"]),
            _full_spec(w["wu1"]),
            _full_spec(w["wd1"]),
            _full_spec(w["nm"]),
            _full_spec(w["win"]),
        ],
        out_specs=[
            pl.BlockSpec((None, tl, D_MODEL), lambda b, i: (b, i, 0)),
            pl.BlockSpec((None, tl, RWKV_PROJ), lambda b, i: (b, i, 0)),
            pl.BlockSpec((tl, SL_IN), lambda b, i: (i, b)),
        ],
        out_shape=[
            jax.ShapeDtypeStruct((B, L, D_MODEL), F32),
            jax.ShapeDtypeStruct((B, L, RWKV_PROJ), F32),
            jax.ShapeDtypeStruct((L, B * SL_IN), F32),
        ],
        compiler_params=pltpu.CompilerParams(
            dimension_semantics=("parallel", "parallel"), vmem_limit_bytes=VMEM_LIMIT),
        name="ffn1_inproj",
    )(x, w["n1"], w["wg1"], w["wu1"], w["wd1"], w["nm"], w["win"])


def _k3_kernel(h_ref, osl_ref, orw_ref, wout_ref, n2_ref, wg_ref, wu_ref, wd_ref, fn_ref, o_ref, *, l, final):
    def rows_pipeline(rows):
        osl = osl_ref[rows, :]
        mixed = jnp.concatenate([osl[:, 0:S5_WIDTH], orw_ref[rows, :], osl[:, S5_WIDTH:SL_OUT]], axis=-1)
        mix = _dot(mixed.astype(BF16), wout_ref[...])
        yield
        h3 = yield from _swiglu_res(h_ref[rows, :] + mix, _row(n2_ref, l), wg_ref, wu_ref, wd_ref)
        if final:
            h3 = _rms(h3, fn_ref[...])
        o_ref[rows, :] = h3

    _staggered([rows_pipeline(rows) for rows in _row_tiles(h_ref.shape[0])])


def _k3(h, osl, orw, w, l, tl, final):
    B, L, _ = h.shape
    return pl.pallas_call(
        functools.partial(_k3_kernel, l=l, final=final),
        grid=(B, L // tl),
        in_specs=[
            pl.BlockSpec((None, tl, D_MODEL), lambda b, i: (b, i, 0)),
            pl.BlockSpec((tl, SL_OUT), lambda b, i: (i, b)),
            pl.BlockSpec((None, tl, RWKV_WIDTH), lambda b, i: (b, i, 0)),
            _full_spec(w["wout"]),
            _full_spec(w["n2"]),
            _full_spec(w["wg2"]),
            _full_spec(w["wu2"]),
            _full_spec(w["wd2"]),
            _full_spec(w["fnorm"]),
        ],
        out_specs=pl.BlockSpec((None, tl, D_MODEL), lambda b, i: (b, i, 0)),
        out_shape=jax.ShapeDtypeStruct((B, L, D_MODEL), F32),
        compiler_params=pltpu.CompilerParams(
            dimension_semantics=("parallel", "parallel"), vmem_limit_bytes=VMEM_LIMIT),
        name="outproj_ffn2",
    )(h, osl, orw, w["wout"], w["n2"], w["wg2"], w["wu2"], w["wd2"], w["fnorm"])


def _stream_spec(arr, l, axis, first, count):
    shape = [None] + list(arr.shape[1:])
    shape[axis] = WEIGHT_CHUNK

    def index(c):
        idx = [l] + [0] * (arr.ndim - 1)
        idx[axis] = jnp.clip(c - first, 0, count - 1)
        return tuple(idx)

    return pl.BlockSpec(tuple(shape), index)


def _cast_spec(rows, cols, axis, first, count):
    shape = [rows, cols]
    shape[axis] = WEIGHT_CHUNK

    def index(c):
        idx = [0, 0]
        idx[axis] = jnp.clip(c - first, 0, count - 1)
        return tuple(idx)

    return pl.BlockSpec(tuple(shape), index)


def _ffn_chunk(c, first, count, xn_scr, acc_scr, wg_ref, wu_ref, wd_ref, wgb_ref, wub_ref, wdb_ref):
    @pl.when((c >= first) & (c < first + count))
    def _():
        wg = wg_ref[...].astype(BF16)
        wu = wu_ref[...].astype(BF16)
        wd = wd_ref[...].astype(BF16)
        wgb_ref[...] = wg
        wub_ref[...] = wu
        wdb_ref[...] = wd
        xn = xn_scr[...]
        g = _dot(xn, wg)
        u = _dot(xn, wu)
        a = (g * _sigmoid(g) * u).astype(BF16)
        acc_scr[...] += _dot(a, wd)


def _k1_stream_kernel(x_ref, n1_ref, nm_ref, wg_ref, wu_ref, wd_ref, win_ref,
                      h_ref, prw_ref, psl_ref, wgb_ref, wub_ref, wdb_ref, winb_ref,
                      xn_scr, acc_scr, proj_scr, *, l, nf, ni):
    c = pl.program_id(0)

    @pl.when(c == 0)
    def _():
        xn_scr[...] = _rms(x_ref[...], _row(n1_ref, l)).astype(BF16)
        acc_scr[...] = jnp.zeros(acc_scr.shape, F32)

    _ffn_chunk(c, 0, nf, xn_scr, acc_scr, wg_ref, wu_ref, wd_ref, wgb_ref, wub_ref, wdb_ref)

    @pl.when(c == nf - 1)
    def _():
        h = x_ref[...] + FFN_RES_SCALE * acc_scr[...]
        h_ref[...] = h
        xn_scr[...] = _rms(h, _row(nm_ref, l)).astype(BF16)

    @pl.when(c >= nf)
    def _():
        win = win_ref[...].astype(BF16)
        winb_ref[...] = win
        col = pl.multiple_of((c - nf) * WEIGHT_CHUNK, WEIGHT_CHUNK)
        proj_scr[:, pl.ds(col, WEIGHT_CHUNK)] = _dot(xn_scr[...], win)

    @pl.when(c == nf + ni - 1)
    def _():
        psl_ref[:, 0:S5_WIDTH] = proj_scr[:, 0:S5_WIDTH]
        prw_ref[...] = proj_scr[:, S5_WIDTH:S5_WIDTH + RWKV_PROJ]
        psl_ref[:, S5_WIDTH:SL_IN] = proj_scr[:, S5_WIDTH + RWKV_PROJ:D_IN]


def _k1_stream(x, a, l):
    NS = x.shape[0]
    nf, ni = D_FF // WEIGHT_CHUNK, D_IN // WEIGHT_CHUNK
    const = lambda shape: pl.BlockSpec(shape, lambda c: (0,) * len(shape))
    return pl.pallas_call(
        functools.partial(_k1_stream_kernel, l=l, nf=nf, ni=ni),
        grid=(nf + ni,),
        in_specs=[
            const((NS, D_MODEL)),
            _full_spec(a["ffn1_norm"]),
            _full_spec(a["mix_norm"]),
            _stream_spec(a["ffn1_w_gate"], l, 2, 0, nf),
            _stream_spec(a["ffn1_w_up"], l, 2, 0, nf),
            _stream_spec(a["ffn1_w_down"], l, 1, 0, nf),
            _stream_spec(a["w_in"], l, 2, nf, ni),
        ],
        out_specs=[
            const((NS, D_MODEL)),
            const((NS, RWKV_PROJ)),
            const((NS, SL_IN)),
            _cast_spec(D_MODEL, D_FF, 1, 0, nf),
            _cast_spec(D_MODEL, D_FF, 1, 0, nf),
            _cast_spec(D_FF, D_MODEL, 0, 0, nf),
            _cast_spec(D_MODEL, D_IN, 1, nf, ni),
        ],
        out_shape=[
            jax.ShapeDtypeStruct((NS, D_MODEL), F32),
            jax.ShapeDtypeStruct((NS, RWKV_PROJ), F32),
            jax.ShapeDtypeStruct((NS, SL_IN), F32),
            jax.ShapeDtypeStruct((D_MODEL, D_FF), BF16),
            jax.ShapeDtypeStruct((D_MODEL, D_FF), BF16),
            jax.ShapeDtypeStruct((D_FF, D_MODEL), BF16),
            jax.ShapeDtypeStruct((D_MODEL, D_IN), BF16),
        ],
        scratch_shapes=[
            pltpu.VMEM((NS, D_MODEL), BF16),
            pltpu.VMEM((NS, D_MODEL), F32),
            pltpu.VMEM((NS, D_IN), F32),
        ],
        compiler_params=pltpu.CompilerParams(dimension_semantics=("arbitrary",), vmem_limit_bytes=VMEM_LIMIT),
        name="ffn1_inproj_stream",
    )(x, a["ffn1_norm"], a["mix_norm"], a["ffn1_w_gate"], a["ffn1_w_up"], a["ffn1_w_down"], a["w_in"])


def _k3_stream_kernel(h_ref, osl_ref, orw_ref, n2_ref, fn_ref, wout_ref, wg_ref, wu_ref, wd_ref,
                      o_ref, woutb_ref, wgb_ref, wub_ref, wdb_ref,
                      mix_scr, h2_scr, xn_scr, acc_scr, *, l, no, nf, final):
    c = pl.program_id(0)

    @pl.when(c == 0)
    def _():
        osl = osl_ref[...]
        mix_scr[...] = jnp.concatenate(
            [osl[:, 0:S5_WIDTH], orw_ref[...], osl[:, S5_WIDTH:SL_OUT]], axis=-1).astype(BF16)
        h2_scr[...] = h_ref[...]

    @pl.when(c < no)
    def _():
        wout = wout_ref[...].astype(BF16)
        woutb_ref[...] = wout
        row = pl.multiple_of(c * WEIGHT_CHUNK, WEIGHT_CHUNK)
        h2_scr[...] += _dot(mix_scr[:, pl.ds(row, WEIGHT_CHUNK)], wout)

    @pl.when(c == no - 1)
    def _():
        xn_scr[...] = _rms(h2_scr[...], _row(n2_ref, l)).astype(BF16)
        acc_scr[...] = jnp.zeros(acc_scr.shape, F32)

    _ffn_chunk(c, no, nf, xn_scr, acc_scr, wg_ref, wu_ref, wd_ref, wgb_ref, wub_ref, wdb_ref)

    @pl.when(c == no + nf - 1)
    def _():
        h3 = h2_scr[...] + FFN_RES_SCALE * acc_scr[...]
        if final:
            h3 = _rms(h3, fn_ref[...])
        o_ref[...] = h3


def _k3_stream(h, osl, orw, a, fnorm, l, final):
    NS = h.shape[0]
    no, nf = D_MODEL // WEIGHT_CHUNK, D_FF // WEIGHT_CHUNK
    const = lambda shape: pl.BlockSpec(shape, lambda c: (0,) * len(shape))
    return pl.pallas_call(
        functools.partial(_k3_stream_kernel, l=l, no=no, nf=nf, final=final),
        grid=(no + nf,),
        in_specs=[
            const((NS, D_MODEL)),
            const((NS, SL_OUT)),
            const((NS, RWKV_WIDTH)),
            _full_spec(a["ffn2_norm"]),
            _full_spec(fnorm),
            _stream_spec(a["w_out"], l, 1, 0, no),
            _stream_spec(a["ffn2_w_gate"], l, 2, no, nf),
            _stream_spec(a["ffn2_w_up"], l, 2, no, nf),
            _stream_spec(a["ffn2_w_down"], l, 1, no, nf),
        ],
        out_specs=[
            const((NS, D_MODEL)),
            _cast_spec(D_MODEL, D_MODEL, 0, 0, no),
            _cast_spec(D_MODEL, D_FF, 1, no, nf),
            _cast_spec(D_MODEL, D_FF, 1, no, nf),
            _cast_spec(D_FF, D_MODEL, 0, no, nf),
        ],
        out_shape=[
            jax.ShapeDtypeStruct((NS, D_MODEL), F32),
            jax.ShapeDtypeStruct((D_MODEL, D_MODEL), BF16),
            jax.ShapeDtypeStruct((D_MODEL, D_FF), BF16),
            jax.ShapeDtypeStruct((D_MODEL, D_FF), BF16),
            jax.ShapeDtypeStruct((D_FF, D_MODEL), BF16),
        ],
        scratch_shapes=[
            pltpu.VMEM((NS, D_MODEL), BF16),
            pltpu.VMEM((NS, D_MODEL), F32),
            pltpu.VMEM((NS, D_MODEL), BF16),
            pltpu.VMEM((NS, D_MODEL), F32),
        ],
        compiler_params=pltpu.CompilerParams(dimension_semantics=("arbitrary",), vmem_limit_bytes=VMEM_LIMIT),
        name="outproj_ffn2_stream",
    )(h, osl, orw, a["ffn2_norm"], fnorm, a["w_out"], a["ffn2_w_gate"], a["ffn2_w_up"], a["ffn2_w_down"])


def _s5_prep_kernel(lr_ref, li_ref, ldt_ref, br_ref, bi_ref, abr_ref, abi_ref, bbr_ref, bbi_ref):
    lr = lr_ref[...]
    li = li_ref[...]
    dt = jnp.exp(ldt_ref[...])
    mag = jnp.exp(lr * dt)
    ab_re = mag * jnp.cos(li * dt)
    ab_im = mag * jnp.sin(li * dt)
    den = lr * lr + li * li
    f_re = ((ab_re - 1.0) * lr + ab_im * li) / den
    f_im = (ab_im * lr - (ab_re - 1.0) * li) / den
    abr_ref[...] = ab_re
    abi_ref[...] = ab_im
    br = br_ref[...]
    bi = bi_ref[...]
    bbr_ref[...] = f_re[:, :, None, :] * br - f_im[:, :, None, :] * bi
    bbi_ref[...] = f_re[:, :, None, :] * bi + f_im[:, :, None, :] * br


def _s5_prep(lam_re, lam_im, log_dt, b_re, b_im):
    G, N, C = S5_GROUPS, S5_STATE, S5_GROUP
    ldt = jnp.broadcast_to(log_dt[:, :, None], (DEPTH, G, N))
    br = jnp.swapaxes(b_re, 2, 3)
    bi = jnp.swapaxes(b_im, 2, 3)
    return pl.pallas_call(
        _s5_prep_kernel,
        out_shape=[
            jax.ShapeDtypeStruct((DEPTH, G, N), F32),
            jax.ShapeDtypeStruct((DEPTH, G, N), F32),
            jax.ShapeDtypeStruct((DEPTH, G, C, N), F32),
            jax.ShapeDtypeStruct((DEPTH, G, C, N), F32),
        ],
        name="s5_discretise",
    )(lam_re, lam_im, ldt, br, bi)


def _s5lru_kernel(psl_ref, ab_ref, win_ref, cre_ref, cim_ref, d_ref, gw_ref, gb_ref,
                  cw_ref, cb_ref, wa_ref, wx_ref, ba_ref, bx_ref, lam_ref,
                  h0_ref, l0_ref, c0_ref,
                  o_ref, hT_ref, lT_ref, cT_ref,
                  hs_scr, ls_scr, xc_scr, xr_scr, xi_scr, la_scr, lb_scr, pin_scr, pout_scr, in_sem, out_sem,
                  *, l, B, Tc, unroll):
    c = pl.program_id(0)
    R = Tc * B

    @pl.when(c == 0)
    def _():
        hs_scr[...] = h0_ref[...]
        ls_scr[...] = l0_ref[...]
        xc_scr[0:CONV_HIST] = c0_ref[...]

    nc = pl.num_programs(0)
    slot = c % 2

    def in_copy(chunk, s, b):
        return pltpu.make_async_copy(
            psl_ref.at[pl.ds(chunk * Tc, Tc), pl.ds(SL_IN * b, SL_IN)], pin_scr.at[s, :, b, :], in_sem.at[s])

    def out_copy(chunk, s, b):
        return pltpu.make_async_copy(
            pout_scr.at[s, :, b, :], o_ref.at[pl.ds(chunk * Tc, Tc), pl.ds(SL_OUT * b, SL_OUT)], out_sem.at[s])

    @pl.when(c == 0)
    def _():
        for b in range(B):
            in_copy(0, 0, b).start()

    @pl.when(c + 1 < nc)
    def _():
        for b in range(B):
            in_copy(c + 1, 1 - slot, b).start()

    for b in range(B):
        in_copy(c, slot, b).wait()

    psl = pin_scr[slot]
    u = psl[:, :, 0:S5_WIDTH].reshape(R, S5_WIDTH)
    gate_in = psl[:, :, S5_WIDTH + LRU_WIDTH:SL_IN].reshape(R, LRU_WIDTH)

    ub = u.astype(BF16)
    half = S5_BLOCK_STATES
    for j in range(S5_BLOCKS):
        x = _dot(ub[:, LANE * j:LANE * (j + 1)], win_ref[j])
        xr_scr[:, :, half * j:half * (j + 1)] = x[:, 0:half].reshape(Tc, B, half)
        xi_scr[:, :, half * j:half * (j + 1)] = x[:, half:2 * half].reshape(Tc, B, half)

    xc_scr[CONV_HIST:CONV_HIST + Tc] = psl[:, :, S5_WIDTH:S5_WIDTH + LRU_WIDTH]
    cw = cw_ref[...]
    xc = _row(cb_ref, l) + xc_scr[0:Tc] * cw[0:1]
    for j in range(1, CONV_WIDTH):
        xc = xc + xc_scr[j:j + Tc] * cw[j:j + 1]
    new_hist = xc_scr[Tc:Tc + CONV_HIST]
    xc_scr[0:CONV_HIST] = new_hist
    xc2 = xc.reshape(R, LRU_WIDTH)
    xcb = xc2.astype(BF16)
    gate_a = _sigmoid(_dot(xcb, wa_ref[...]) + _row(ba_ref, l))
    gate_x = _sigmoid(_dot(xcb, wx_ref[...]) + _row(bx_ref, l))
    log_a = LRU_C * gate_a * (-_softplus(-_row(lam_ref, l)))
    la_scr[...] = jnp.exp(log_a).reshape(Tc, B, LRU_WIDTH)
    lb_scr[...] = (jnp.sqrt(_neg_expm1(2.0 * log_a)) * (gate_x * xc2)).reshape(Tc, B, LRU_WIDTH)

    ar = jnp.broadcast_to(ab_ref[0:1, :], (B, S5_LANES))
    ai = jnp.broadcast_to(ab_ref[1:2, :], (B, S5_LANES))

    def step(t, carry):
        hr, hi, hl = carry
        nr = ar * hr - ai * hi + xr_scr[t]
        ni = ar * hi + ai * hr + xi_scr[t]
        xr_scr[t] = nr
        xi_scr[t] = ni
        nl = la_scr[t] * hl + lb_scr[t]
        lb_scr[t] = nl
        return nr, ni, nl

    hr, hi, hl = lax.fori_loop(0, Tc, step, (hs_scr[0], hs_scr[1], ls_scr[...]), unroll=unroll)
    hs_scr[0] = hr
    hs_scr[1] = hi
    ls_scr[...] = hl

    hrb = xr_scr[...].reshape(R, S5_LANES).astype(BF16)
    hib = xi_scr[...].reshape(R, S5_LANES).astype(BF16)
    ys = []
    for j in range(S5_BLOCKS):
        sl = slice(half * j, half * (j + 1))
        ys.append(_dot(hrb[:, sl], cre_ref[j]) - _dot(hib[:, sl], cim_ref[j]))
    y = jnp.concatenate(ys, axis=-1) + _row(d_ref, l) * u
    z = _gelu(y)
    o_s5 = z * _sigmoid(_dot(z.astype(BF16), gw_ref[...].astype(BF16)) + _row(gb_ref, l))
    o_lru = lb_scr[...].reshape(R, LRU_WIDTH) * _gelu(gate_in)
    @pl.when(c >= 2)
    def _():
        for b in range(B):
            out_copy(c - 2, slot, b).wait()

    pout_scr[slot, :, :, 0:S5_WIDTH] = o_s5.reshape(Tc, B, S5_WIDTH)
    pout_scr[slot, :, :, S5_WIDTH:SL_OUT] = o_lru.reshape(Tc, B, LRU_WIDTH)
    for b in range(B):
        out_copy(c, slot, b).start()

    @pl.when(c == nc - 1)
    def _():
        hT_ref[...] = hs_scr[...]
        lT_ref[...] = ls_scr[...]
        cT_ref[...] = xc_scr[0:CONV_HIST]
        for b in range(B):
            out_copy(c, slot, b).wait()

    @pl.when((c == nc - 1) & (c >= 1))
    def _():
        for b in range(B):
            out_copy(c - 1, 1 - slot, b).wait()


def _s5lru(psl2, B, w, l, st, Tc):
    L = psl2.shape[0]
    li = st["li"]
    unroll = True
    return pl.pallas_call(
        functools.partial(_s5lru_kernel, l=l, B=B, Tc=Tc, unroll=unroll),
        grid=(L // Tc,),
        in_specs=[
            pl.BlockSpec(memory_space=pl.ANY),
            _layer_spec(w["s5_ab"], l),
            _layer_spec(w["s5_win"], l),
            _layer_spec(w["s5_cre"], l),
            _layer_spec(w["s5_cim"], l),
            _full_spec(w["s5_d"]),
            _layer_spec(w["s5_glu_w"], l),
            _full_spec(w["s5_glu_b"]),
            _layer_spec(w["lru_conv_w"], l),
            _full_spec(w["lru_conv_b"]),
            _layer_spec(w["lru_wa"], l),
            _layer_spec(w["lru_wx"], l),
            _full_spec(w["lru_b_a"]),
            _full_spec(w["lru_b_x"]),
            _full_spec(w["lru_lambda"]),
            _layer_spec(st["s5"], li),
            _layer_spec(st["lru"], li),
            _layer_spec(st["conv"], li),
        ],
        out_specs=[
            pl.BlockSpec(memory_space=pl.ANY),
            pl.BlockSpec((2, B, S5_LANES), lambda c: (0, 0, 0)),
            pl.BlockSpec((B, LRU_WIDTH), lambda c: (0, 0)),
            pl.BlockSpec((CONV_HIST, B, LRU_WIDTH), lambda c: (0, 0, 0)),
        ],
        out_shape=[
            jax.ShapeDtypeStruct((L, B * SL_OUT), F32),
            jax.ShapeDtypeStruct((2, B, S5_LANES), F32),
            jax.ShapeDtypeStruct((B, LRU_WIDTH), F32),
            jax.ShapeDtypeStruct((CONV_HIST, B, LRU_WIDTH), F32),
        ],
        scratch_shapes=[
            pltpu.VMEM((2, B, S5_LANES), F32),
            pltpu.VMEM((B, LRU_WIDTH), F32),
            pltpu.VMEM((Tc + CONV_HIST, B, LRU_WIDTH), F32),
            pltpu.VMEM((Tc, B, S5_LANES), F32),
            pltpu.VMEM((Tc, B, S5_LANES), F32),
            pltpu.VMEM((Tc, B, LRU_WIDTH), F32),
            pltpu.VMEM((Tc, B, LRU_WIDTH), F32),
            pltpu.VMEM((2, Tc, B, SL_IN), F32),
            pltpu.VMEM((2, Tc, B, SL_OUT), F32),
            pltpu.SemaphoreType.DMA((2,)),
            pltpu.SemaphoreType.DMA((2,)),
        ],
        compiler_params=pltpu.CompilerParams(
            dimension_semantics=("arbitrary",), vmem_limit_bytes=VMEM_LIMIT),
        name="s5_rglru_scan",
    )(psl2, w["s5_ab"], w["s5_win"], w["s5_cre"], w["s5_cim"], w["s5_d"], w["s5_glu_w"], w["s5_glu_b"],
      w["lru_conv_w"], w["lru_conv_b"], w["lru_wa"], w["lru_wx"], w["lru_b_a"], w["lru_b_x"], w["lru_lambda"],
      st["s5"], st["lru"], st["conv"])


_RWKV_PARAMS = ("rwkv_mu", "rwkv_w0", "rwkv_w_up", "rwkv_a0", "rwkv_a_up", "rwkv_g_up", "rwkv_k_k", "rwkv_k_a",
                "rwkv_r_k", "rwkv_ln_w", "rwkv_ln_b", "rwkv_ones")


def _rwkv_param_specs(w, l):
    layered = ("rwkv_w_up", "rwkv_a_up", "rwkv_g_up")
    return [_layer_spec(w[k], l) if k in layered else _full_spec(w[k]) for k in _RWKV_PARAMS]


def _rwkv_mats(prm):
    wup_ref, aup_ref, gup_ref, ones_ref = prm[2], prm[4], prm[5], prm[11]
    zpad = jnp.zeros((DECAY_RANK, RWKV_WIDTH), BF16)
    wup = jnp.concatenate([wup_ref[...].astype(BF16), zpad], axis=0)
    aup = jnp.concatenate([zpad, aup_ref[...].astype(BF16)], axis=0)
    return wup, aup, gup_ref[...].astype(BF16), ones_ref[...]


def _segsum(x, ones):
    xb = x.astype(BF16)
    return jnp.concatenate([_dot(xb[:, LANE * p:LANE * (p + 1)], ones) for p in range(RWKV_PAIRS)], axis=-1)


def _rwkv_token_math(p, prev, prm, mats, l):
    mu_ref, w0_ref, _, a0_ref, _, _, kk_ref, ka_ref, rk_ref = prm[0:9]
    wup, aup, gup, ones = mats
    W = RWKV_WIDTH
    xm = p + (prev - p) * _row(mu_ref, l)
    r = xm[:, 0:W]
    k = xm[:, W:2 * W]
    v = xm[:, 2 * W:3 * W]
    xwa = xm[:, 3 * W:3 * W + DECAY_RANK + ICL_RANK]
    xg = xm[:, 3 * W + DECAY_RANK + ICL_RANK:RWKV_PROJ]
    lw = _row(w0_ref, l) + _dot(jnp.tanh(xwa).astype(BF16), wup)
    ld = -math.exp(-0.5) * _sigmoid(lw)
    a = _sigmoid(_row(a0_ref, l) + _dot(xwa.astype(BF16), aup))
    g = _dot(_sigmoid(xg).astype(BF16), gup)
    kkr = k * _row(kk_ref, l)
    kk = kkr * lax.rsqrt(jnp.maximum(_segsum(kkr * kkr, ones), 1e-24))
    k2 = k * (1.0 + (a - 1.0) * _row(ka_ref, l))
    bonus = _segsum(r * k2 * _row(rk_ref, l), ones) * v
    return r, k2, v, kk, a, ld, g, bonus


def _rwkv_out(y, bonus, g, prm, ones, l):
    lnw_ref, lnb_ref = prm[9], prm[10]
    inv = 1.0 / RWKV_HEAD
    mean = _segsum(y, ones) * inv
    d = y - mean
    var = _segsum(d * d, ones) * inv
    yn = d * lax.rsqrt(var + RWKV_LN_EPS) * _row(lnw_ref, l) + _row(lnb_ref, l)
    return (yn + bonus) * g


RWKV_GROUP_HEADS = 4
RWKV_GROUP_LANES = RWKV_GROUP_HEADS * RWKV_HEAD
_RWKV_OPERANDS = ("a", "b", "k", "r", "v", "bp", "kp")


def _rwkv_groups(B):
    full = [[(b, 0, RWKV_GROUP_LANES)] for b in range(B)]
    rest = RWKV_WIDTH - RWKV_GROUP_LANES
    tail = [[(b, RWKV_GROUP_LANES, rest), (b + 1, RWKV_GROUP_LANES, rest)] for b in range(0, B, 2)]
    return full + tail


def _rwkv_chunk_kernel(*refs, l, B, Tc):
    n_prm = len(_RWKV_PARAMS)
    p_ref, s0_ref, sh0_ref = refs[0:3]
    prm = refs[3:3 + n_prm]
    tri_ref, o_ref, sT_ref = refs[3 + n_prm:6 + n_prm]
    st_scr, xs_scr, y_scr, pt_scr = refs[6 + n_prm:10 + n_prm]
    src = dict(zip(_RWKV_OPERANDS, refs[10 + n_prm:]))
    c = pl.program_id(0)
    R = B * Tc
    W = RWKV_WIDTH
    GL = RWKV_GROUP_LANES
    GH = RWKV_GROUP_HEADS
    carry_row = SUBLANE - 1

    @pl.when(c == 0)
    def _():
        st_scr[...] = s0_ref[...]
        xs_scr[:, carry_row:SUBLANE, :] = sh0_ref[...]

    ps, prevs = [], []
    for b in range(B):
        pb = p_ref[b]
        prevs.append(jnp.concatenate([xs_scr[b, carry_row:SUBLANE, :], pb[0:Tc - 1, :]], axis=0))
        xs_scr[b, carry_row:SUBLANE, :] = pb[Tc - 1:Tc, :]
        ps.append(pb)
    mats = _rwkv_mats(prm)
    r, k2, v, kk, a, ld, g, bonus = _rwkv_token_math(
        jnp.concatenate(ps, axis=0), jnp.concatenate(prevs, axis=0), prm, mats, l)

    h1, h2, h3 = _split3(ld)
    tri = tri_ref[...]
    cls = []
    for b in range(B):
        rows = slice(b * Tc, (b + 1) * Tc)
        cls.append(_dot(tri, h1[rows]) + _dot(tri, h2[rows]) + _dot(tri, h3[rows]))
    cl = jnp.concatenate(cls, axis=0)
    pt = jnp.exp(cl.reshape(B, Tc, W)[:, Tc - 1:Tc, :])
    inv_p = jnp.exp(-cl)
    to_end = (pt * inv_p.reshape(B, Tc, W)).reshape(R, W)
    beta = kk * a
    src["a"][...] = (-kk * jnp.exp(cl - ld)).astype(BF16)
    src["b"][...] = beta * inv_p
    src["k"][...] = k2 * inv_p
    src["r"][...] = (r * jnp.exp(cl)).astype(BF16)
    src["v"][...] = v.astype(BF16)
    src["bp"][...] = (beta * to_end).astype(BF16)
    src["kp"][...] = (k2 * to_end).astype(BF16)
    pt_scr[...] = pt

    assert Tc & (Tc - 1) == 0
    n_doubling = max(1, (Tc - 1).bit_length())
    groups = _rwkv_groups(B)

    def block_masks(blk):
        lane = lax.broadcasted_iota(jnp.int32, (1, GH * blk), 1)
        return [((lane >= blk * h) & (lane < blk * (h + 1))).astype(BF16) for h in range(GH)]

    head_masks = block_masks(RWKV_HEAD)
    time_masks = block_masks(Tc)

    def stack(x, masks):
        return jnp.concatenate([x * m.astype(x.dtype) for m in masks], axis=0)

    ti = lax.broadcasted_iota(jnp.int32, (Tc, GH * Tc), 0)
    si = lax.broadcasted_iota(jnp.int32, (Tc, GH * Tc), 1) & (Tc - 1)
    strict = (si < ti).astype(BF16)
    incl = (si <= ti).astype(BF16)
    same_head = ((lax.broadcasted_iota(jnp.int32, (GL, GL), 0) // RWKV_HEAD)
                 == (lax.broadcasted_iota(jnp.int32, (GL, GL), 1) // RWKV_HEAD)).astype(F32)

    def operand(name, grp):
        parts = [src[name][b * Tc:(b + 1) * Tc, lo:lo + n] for b, lo, n in grp]
        return parts[0] if len(parts) == 1 else jnp.concatenate(parts, axis=-1)

    v4s, gss, ns, wrbs, wrks, xs = [], [], [], [], [], []
    for gi, grp in enumerate(groups):
        lhs = jnp.concatenate([operand("a", grp), operand("r", grp)], axis=0)
        v4 = stack(operand("v", grp), head_masks)
        gb = _dot_nt(lhs, stack(operand("b", grp), head_masks)).astype(BF16)
        gk = _dot_nt(lhs, stack(operand("k", grp), head_masks)).astype(BF16)
        gs = _dot_nt(lhs, st_scr[gi])
        v4s.append(v4)
        gss.append(gs[Tc:2 * Tc])
        ns.append(gb[0:Tc] * strict)
        wrbs.append(gb[Tc:2 * Tc] * incl)
        wrks.append(gk[Tc:2 * Tc] * incl)
        xs.append(gs[0:Tc] + _dot(gk[0:Tc] * strict, v4))
    for j in range(n_doubling):
        for i in range(len(groups)):
            xs[i] = xs[i] + _dot(ns[i], stack(xs[i].astype(BF16), head_masks))
        if j + 1 < n_doubling:
            for i in range(len(groups)):
                ns[i] = _dot(ns[i], stack(ns[i], time_masks)).astype(BF16)
    for gi, grp in enumerate(groups):
        sab = xs[gi].astype(BF16)
        y = gss[gi] + _dot(wrbs[gi], stack(sab, head_masks)) + _dot(wrks[gi], v4s[gi])
        lane0 = 0
        for b, lo, n in grp:
            y_scr[b * Tc:(b + 1) * Tc, lo:lo + n] = y[:, lane0:lane0 + n]
            lane0 += n
        upd = _dot_tn(jnp.concatenate([sab, operand("v", grp)], axis=0),
                      jnp.concatenate([operand("bp", grp), operand("kp", grp)], axis=0))
        ptg = [pt_scr[b, :, lo:lo + n] for b, lo, n in grp]
        ptg = ptg[0] if len(ptg) == 1 else jnp.concatenate(ptg, axis=-1)
        st_scr[gi] = st_scr[gi] * ptg + upd * same_head

    out = _rwkv_out(y_scr[...], bonus, g, prm, mats[3], l)
    for b in range(B):
        o_ref[b] = out[b * Tc:(b + 1) * Tc]

    @pl.when(c == pl.num_programs(0) - 1)
    def _():
        sT_ref[...] = st_scr[...]


def _rwkv_chunk(prw, w, l, s0, sh0, Tc):
    B, L, _ = prw.shape
    R = B * Tc
    W = RWKV_WIDTH
    GL = RWKV_GROUP_LANES
    NG = len(_rwkv_groups(B))
    tri = jnp.tril(jnp.ones((Tc, Tc), F32)).astype(BF16)
    return pl.pallas_call(
        functools.partial(_rwkv_chunk_kernel, l=l, B=B, Tc=Tc),
        grid=(L // Tc,),
        in_specs=[
            pl.BlockSpec((B, Tc, RWKV_PROJ), lambda c: (0, c, 0)),
            _full_spec(s0),
            _full_spec(sh0),
        ] + _rwkv_param_specs(w, l) + [_full_spec(tri)],
        out_specs=[
            pl.BlockSpec((B, Tc, W), lambda c: (0, c, 0)),
            pl.BlockSpec((NG, GL, GL), lambda c: (0, 0, 0)),
        ],
        out_shape=[
            jax.ShapeDtypeStruct((B, L, W), F32),
            jax.ShapeDtypeStruct((NG, GL, GL), F32),
        ],
        scratch_shapes=[
            pltpu.VMEM((NG, GL, GL), F32),
            pltpu.VMEM((B, SUBLANE, RWKV_PROJ), F32),
            pltpu.VMEM((R, W), F32),
            pltpu.VMEM((B, 1, W), F32),
        ] + [pltpu.VMEM((R, W), F32 if name in ("b", "k") else BF16) for name in _RWKV_OPERANDS],
        compiler_params=pltpu.CompilerParams(
            dimension_semantics=("arbitrary",), vmem_limit_bytes=VMEM_LIMIT),
        name="rwkv7_chunked",
    )(prw, s0, sh0, *[w[k] for k in _RWKV_PARAMS], tri)


def _rwkv_step_kernel(*refs, l):
    n_prm = len(_RWKV_PARAMS)
    p_ref, sh_ref, s_ref = refs[0:3]
    prm = refs[3:3 + n_prm]
    o_ref, sn_ref = refs[3 + n_prm:5 + n_prm]
    nkk_t, w_t, beta_t, v_t, k2_t, r_t, y_t, g_scr, bonus_scr = refs[5 + n_prm:]
    h = pl.program_id(0)
    N = RWKV_HEAD

    @pl.when(h == 0)
    def _():
        r, k2, v, kk, a, ld, g, bonus = _rwkv_token_math(p_ref[...], sh_ref[...], prm, _rwkv_mats(prm), l)
        nkk_t[...] = (-kk).T
        w_t[...] = jnp.exp(ld).T
        beta_t[...] = (kk * a).T
        v_t[...] = v.T
        k2_t[...] = k2.T
        r_t[...] = r.T
        g_scr[...] = g
        bonus_scr[...] = bonus

    rows = pl.ds(pl.multiple_of(h * N, N), N)
    s = s_ref[...]
    sa = jnp.sum(s * nkk_t[rows, :][None], axis=1)
    sn = s * w_t[rows, :][None] + sa[:, None, :] * beta_t[rows, :][None] + v_t[rows, :][:, None, :] * k2_t[rows, :][None]
    sn_ref[...] = sn
    y_t[rows, :] = jnp.sum(sn * r_t[rows, :][None], axis=1)

    @pl.when(h == pl.num_programs(0) - 1)
    def _():
        o_ref[...] = _rwkv_out(y_t[...].T, bonus_scr[...], g_scr[...], prm, prm[11][...], l)


def _rwkv_step(prw, w, l, s_all, sh_all):
    B = prw.shape[0]
    W = RWKV_WIDTH
    N = RWKV_HEAD
    return pl.pallas_call(
        functools.partial(_rwkv_step_kernel, l=l),
        grid=(RWKV_HEADS,),
        in_specs=[
            _full_spec(prw),
            _layer_spec(sh_all, l),
            pl.BlockSpec((None, None, N, N, B), lambda h: (l, h, 0, 0, 0)),
        ] + _rwkv_param_specs(w, l),
        out_specs=[
            pl.BlockSpec((B, W), lambda h: (0, 0)),
            pl.BlockSpec((None, N, N, B), lambda h: (h, 0, 0, 0)),
        ],
        out_shape=[
            jax.ShapeDtypeStruct((B, W), F32),
            jax.ShapeDtypeStruct((RWKV_HEADS, N, N, B), F32),
        ],
        scratch_shapes=[pltpu.VMEM((W, B), F32)] * 7 + [pltpu.VMEM((B, W), F32)] * 2,
        compiler_params=pltpu.CompilerParams(
            dimension_semantics=("arbitrary",), vmem_limit_bytes=VMEM_LIMIT),
        name="rwkv7_step",
    )(prw, sh_all, s_all, *[w[k] for k in _RWKV_PARAMS])


def _block_diag(blocks):
    *lead, n, r, c = blocks.shape
    eye = jnp.eye(n, dtype=blocks.dtype)
    out = blocks[..., :, :, None, :] * eye[:, None, :, None]
    return out.reshape(*lead, n * r, n * c)


def _prepare_weights(a):
    G, C, N = S5_GROUPS, S5_GROUP, S5_STATE
    abr, abi, bbr, bbi = _s5_prep(a["s5_lambda_re"], a["s5_lambda_im"], a["s5_log_dt"], a["s5_b_re"], a["s5_b_im"])

    def in_side(bb):
        return _block_diag(bb.reshape(DEPTH, S5_BLOCKS, S5_GROUPS_PER_BLOCK, C, N))

    def out_side(cc):
        ct = jnp.swapaxes(cc, 2, 3)
        return _block_diag(ct.reshape(DEPTH, S5_BLOCKS, S5_GROUPS_PER_BLOCK, N, C))

    w = {k: a[k] for k in (
        "s5_d", "s5_glu_w", "s5_glu_b", "lru_conv_w", "lru_conv_b", "lru_b_a", "lru_b_x", "lru_lambda",
        "rwkv_mu", "rwkv_w0", "rwkv_w_up", "rwkv_a0", "rwkv_a_up", "rwkv_g_up", "rwkv_k_k", "rwkv_k_a",
        "rwkv_ln_w", "rwkv_ln_b")}
    w.update({
        "n1": a["ffn1_norm"], "nm": a["mix_norm"], "n2": a["ffn2_norm"],
        "fnorm": a["final_norm"].reshape(1, D_MODEL),
        "s5_ab": jnp.stack([abr.reshape(DEPTH, S5_LANES), abi.reshape(DEPTH, S5_LANES)], axis=1),
        "s5_win": jnp.concatenate([in_side(bbr), in_side(bbi)], axis=-1).astype(BF16),
        "s5_cre": out_side(a["s5_c_re"]).astype(BF16),
        "s5_cim": out_side(a["s5_c_im"]).astype(BF16),
        "lru_wa": _block_diag(a["lru_w_a"]).astype(BF16),
        "lru_wx": _block_diag(a["lru_w_x"]).astype(BF16),
        "rwkv_r_k": a["rwkv_r_k"].reshape(DEPTH, RWKV_WIDTH),
        "rwkv_ones": _block_diag(jnp.ones((2, RWKV_HEAD, RWKV_HEAD), F32)).astype(BF16),
    })
    return w


def _groups_to_heads(sg, B):
    N = RWKV_HEAD
    diag = jnp.stack([sg[:, N * i:N * (i + 1), N * i:N * (i + 1)] for i in range(RWKV_GROUP_HEADS)], axis=1)
    first = diag[:B]
    rest = diag[B:].reshape(B, RWKV_HEADS - RWKV_GROUP_HEADS, N, N)
    return jnp.concatenate([first, rest], axis=1)


def _seq_mixers(prw, psl2, st, w, l, Tc):
    L = prw.shape[1]
    osl, hT, lT, cT = _s5lru(psl2, prw.shape[0], w, l, st, Tc)
    orw, sT = _rwkv_chunk(prw, w, l, st["rwkv"], st["shift"], Tc)
    new = {"s5": hT[None], "lru": lT[None], "conv": cT[None], "li": 0,
           "rwkv": sT, "shift": prw[:, L - 1:L, :]}
    return osl, orw, new


def kernel(x_prompt, x_sample, state_s5_re, state_s5_im, state_rwkv, state_rwkv_shift, state_lru, state_lru_conv, meta_tokens, ffn1_norm, ffn1_w_gate, ffn1_w_up, ffn1_w_down, mix_norm, w_in, s5_lambda_re, s5_lambda_im, s5_log_dt, s5_b_re, s5_b_im, s5_c_re, s5_c_im, s5_d, s5_glu_w, s5_glu_b, rwkv_mu, rwkv_w0, rwkv_w_up, rwkv_a0, rwkv_a_up, rwkv_g_up, rwkv_k_k, rwkv_k_a, rwkv_r_k, rwkv_ln_w, rwkv_ln_b, lru_conv_w, lru_conv_b, lru_w_a, lru_b_a, lru_w_x, lru_b_x, lru_lambda, w_out, ffn2_norm, ffn2_w_gate, ffn2_w_up, ffn2_w_down, final_norm):
    a = dict(locals())
    w = _prepare_weights(a)
    BP, SEQ, _ = x_prompt.shape
    BS = x_sample.shape[0]
    assert SEQ % PROMPT_TOKEN_TILE == 0 and SEQ % PROMPT_CHUNK == 0 and BP % 2 == 0

    zero = {
        "s5": jnp.zeros((1, 2, BP, S5_LANES), F32),
        "lru": jnp.zeros((1, BP, LRU_WIDTH), F32),
        "conv": jnp.zeros((1, CONV_HIST, BP, LRU_WIDTH), F32),
        "li": 0,
        "rwkv": jnp.zeros((len(_rwkv_groups(BP)), RWKV_GROUP_LANES, RWKV_GROUP_LANES), F32),
        "shift": jnp.zeros((BP, 1, RWKV_PROJ), F32),
    }
    sample_in = {
        "s5": jnp.stack([state_s5_re.reshape(DEPTH, BS, S5_LANES), state_s5_im.reshape(DEPTH, BS, S5_LANES)], axis=1),
        "lru": state_lru,
        "conv": jnp.swapaxes(state_lru_conv, 1, 2),
    }

    rwkv_in = jnp.transpose(state_rwkv, (0, 2, 3, 4, 1))

    xp = x_prompt
    xs = jnp.concatenate([meta_tokens.astype(F32), x_sample.reshape(BS, D_MODEL)], axis=0)
    p_states = []
    s_states = []
    for l in range(DEPTH):
        final = l == DEPTH - 1
        hs, prw_s, psl_s, w["wg1"], w["wu1"], w["wd1"], w["win"] = _k1_stream(xs, a, l)
        prw_m = jnp.broadcast_to(prw_s[None, :N_META], (BP, N_META, RWKV_PROJ))
        psl_m = jnp.tile(psl_s[:N_META], (1, BP))
        osl_m, orw_m, st_meta = _seq_mixers(prw_m, psl_m, zero, w, l, N_META)

        st_l = dict(sample_in, li=l)
        osl_s, hT, lT, cT = _s5lru(psl_s[N_META:].reshape(1, BS * SL_IN), BS, w, l, st_l, 1)
        orw_s, sT = _rwkv_step(prw_s[N_META:], w, l, rwkv_in, state_rwkv_shift)
        s_states.append({"s5": hT, "lru": lT, "conv": cT, "rwkv": sT, "shift": prw_s[N_META:]})
        if final:
            rows = (hs[N_META:], osl_s.reshape(BS, SL_OUT), orw_s)
        else:
            rows = (hs, jnp.concatenate([osl_m[:, 0:SL_OUT], osl_s.reshape(BS, SL_OUT)], axis=0),
                    jnp.concatenate([orw_m[0], orw_s], axis=0))
        xs, w["wout"], w["wg2"], w["wu2"], w["wd2"] = _k3_stream(*rows, a, w["fnorm"], l, final)

        hp, prw_p, psl_p = _k1(xp, w, l, PROMPT_TOKEN_TILE)
        osl_p, orw_p, st_p = _seq_mixers(prw_p, psl_p, st_meta, w, l, PROMPT_CHUNK)
        p_states.append(st_p)
        xp = _k3(hp, osl_p, orw_p, w, l, PROMPT_TOKEN_TILE, final)

    p_out = (
        jnp.stack([s["s5"][0, 0].reshape(BP, S5_GROUPS, S5_STATE) for s in p_states]),
        jnp.stack([s["s5"][0, 1].reshape(BP, S5_GROUPS, S5_STATE) for s in p_states]),
        jnp.stack([_groups_to_heads(s["rwkv"], BP) for s in p_states]),
        jnp.stack([s["shift"][:, 0, :] for s in p_states]),
        jnp.stack([s["lru"][0] for s in p_states]),
        jnp.stack([jnp.swapaxes(s["conv"][0], 0, 1) for s in p_states]),
    )
    s_out = (
        jnp.stack([s["s5"][0].reshape(BS, S5_GROUPS, S5_STATE) for s in s_states]),
        jnp.stack([s["s5"][1].reshape(BS, S5_GROUPS, S5_STATE) for s in s_states]),
        jnp.transpose(jnp.stack([s["rwkv"] for s in s_states]), (0, 4, 1, 2, 3)),
        jnp.stack([s["shift"] for s in s_states]),
        jnp.stack([s["lru"] for s in s_states]),
        jnp.stack([jnp.swapaxes(s["conv"], 0, 1) for s in s_states]),
    )
    return (xp, xs.reshape(BS, 1, D_MODEL)) + p_out + s_out
```
